```python
import math
import jax, jax.numpy as jnp
from jax import lax
import numpy as np

D_MODEL = 1024
BATCH = 8
SEQ = 2048
DEPTH = 2
DEC_BATCH = 128
DEC_SEQ = 1
PAST_LEN = 16384
PAGE_SIZE = 128

D_RWKV = D_MODEL
RWKV_HEAD = 64
RWKV_HEADS = D_RWKV // RWKV_HEAD
LORA_W = 64
LORA_A = 64
LNX_EPS = 64e-5
D_MAMBA = D_MODEL
MAMBA_HEADDIM = 64
MAMBA_HEADS = D_MAMBA // MAMBA_HEADDIM
N_GROUPS = 2
HEADS_PER_GROUP = MAMBA_HEADS // N_GROUPS
D_STATE = 128
CONV_W = 4
CONV_DIM = D_MAMBA + 2 * N_GROUPS * D_STATE
CHUNK = 128
D_MIX = D_RWKV + D_MAMBA
R_SHIFT = 3 * D_RWKV + LORA_W + LORA_A
D_IN_RWKV = R_SHIFT + D_RWKV
D_IN_MAMBA = D_MAMBA + CONV_DIM + MAMBA_HEADS
D_IN = D_IN_RWKV + D_IN_MAMBA
NORM_EPS = 1e-5

kernel_name = "hybrid_rwkv7_mamba2_step"


def rmsnorm(x, w, eps=NORM_EPS):
    xf = x.astype(jnp.float32)
    return xf * lax.rsqrt(jnp.mean(xf * xf, axis=-1, keepdims=True) + eps) * w.astype(jnp.float32)


def rwkv7_mix(p_sh, prev_sh, z, S0, mu, w0, w2, a0, a2, k_k, k_a, r_k, lnx_w, lnx_b):
    Bsz, L = p_sh.shape[:2]
    shifted = jnp.concatenate([prev_sh, p_sh[:, :-1]], axis=1)
    xm = p_sh + (shifted - p_sh) * mu
    r, k, v, wl, al = jnp.split(xm, [D_RWKV, 2 * D_RWKV, 3 * D_RWKV, 3 * D_RWKV + LORA_W], axis=-1)
    w_pre = -jax.nn.softplus(-(w0 + jnp.tanh(wl) @ w2)) - 0.5
    decay = jnp.exp(-jnp.exp(w_pre))
    a = jax.nn.sigmoid(a0 + al @ a2)
    heads = lambda t: t.reshape(Bsz, L, RWKV_HEADS, RWKV_HEAD)
    r, k, v, decay, a = map(heads, (r, k, v, decay, a))
    kk = k * k_k.reshape(RWKV_HEADS, RWKV_HEAD)
    kk = kk / jnp.maximum(jnp.sqrt(jnp.sum(kk * kk, axis=-1, keepdims=True)), 1e-12)
    k = k * (1.0 + (a - 1.0) * k_a.reshape(RWKV_HEADS, RWKV_HEAD))

    def step(S, inp):
        r_t, w_t, k_t, v_t, kk_t, a_t = inp
        s_kk = jnp.einsum('bhvk,bhk->bhv', S, kk_t)
        S = (S * w_t[:, :, None, :] - s_kk[..., None] * (kk_t * a_t)[:, :, None, :]
             + v_t[..., None] * k_t[:, :, None, :])
        return S, jnp.einsum('bhvk,bhk->bhv', S, r_t)

    seq_first = lambda t: jnp.moveaxis(t.astype(jnp.float32), 1, 0)
    S_fin, o = lax.scan(step, S0.astype(jnp.float32),
                        tuple(map(seq_first, (r, decay, k, v, kk, a))))
    o = jnp.moveaxis(o, 0, 1)
    mean = jnp.mean(o, axis=-1, keepdims=True)
    var = jnp.mean(jnp.square(o - mean), axis=-1, keepdims=True)
    o = ((o - mean) * lax.rsqrt(var + LNX_EPS)).reshape(Bsz, L, D_RWKV) * lnx_w + lnx_b
    bonus = jnp.sum(r * k * r_k, axis=-1, keepdims=True) * v
    y = (o + bonus.reshape(Bsz, L, D_RWKV)) * jax.nn.silu(z)
    return y, S_fin


def ssd_chunked(xdt, dA, Bm, Cm, h0):
    Bsz, L = xdt.shape[:2]
    Q = min(CHUNK, L)
    pad = (-L) % Q
    if pad:
        padf = lambda t: jnp.pad(t, [(0, 0), (0, pad)] + [(0, 0)] * (t.ndim - 2))
        xdt, dA, Bm, Cm = map(padf, (xdt, dA, Bm, Cm))
    nc = (L + pad) // Q
    chunk = lambda t: t.reshape(Bsz, nc, Q, *t.shape[2:])
    xdt, dA, Bm, Cm = map(chunk, (xdt, dA, Bm, Cm))
    A_cs = jnp.cumsum(dA, axis=2)
    seg = A_cs[:, :, :, None] - A_cs[:, :, None, :]
    causal = jnp.tril(jnp.ones((Q, Q), dtype=bool))[:, :, None, None]
    Lmat = jnp.exp(jnp.where(causal, seg, -jnp.inf))
    CB = jnp.einsum('bclgn,bcsgn->bclsg', Cm, Bm)
    y_diag = jnp.einsum('bclsg,bclsgj,bcsgjp->bclgjp', CB, Lmat, xdt)
    decay_to_end = jnp.exp(A_cs[:, :, -1:] - A_cs)
    states = jnp.einsum('bcsgn,bcsgj,bcsgjp->bcgjpn', Bm, decay_to_end, xdt)
    chunk_decay = jnp.exp(A_cs[:, :, -1])

    def carry(h, inp):
        s_c, d_c = inp
        return h * d_c[..., None, None] + s_c, h

    h_fin, h_in = lax.scan(carry, h0, (jnp.moveaxis(states, 1, 0), jnp.moveaxis(chunk_decay, 1, 0)))
    h_in = jnp.moveaxis(h_in, 0, 1)
    y_off = jnp.einsum('bclgn,bcgjpn,bclgj->bclgjp', Cm, h_in, jnp.exp(A_cs))
    y = (y_diag + y_off).reshape(Bsz, nc * Q, *xdt.shape[3:])[:, :L]
    return y, h_fin


def mamba2_mix(z, xbc, dt_raw, conv0, h0, conv_w, conv_b, dt_bias, a_log, d_skip, gnorm_w):
    f32 = jnp.float32
    Bsz, L = xbc.shape[:2]
    ext = jnp.concatenate([conv0.astype(f32), xbc.astype(f32)], axis=1)
    conv = lax.conv_general_dilated(ext, conv_w[:, None, :].astype(f32), (1,), 'VALID',
                                    dimension_numbers=('NWC', 'WIO', 'NWC'),
                                    feature_group_count=CONV_DIM)
    u = jax.nn.silu(conv + conv_b)
    xs, Bm, Cm = jnp.split(u, [D_MAMBA, D_MAMBA + N_GROUPS * D_STATE], axis=-1)
    xs = xs.reshape(Bsz, L, N_GROUPS, HEADS_PER_GROUP, MAMBA_HEADDIM)
    Bm = Bm.reshape(Bsz, L, N_GROUPS, D_STATE)
    Cm = Cm.reshape(Bsz, L, N_GROUPS, D_STATE)
    dt = jax.nn.softplus(dt_raw + dt_bias).reshape(Bsz, L, N_GROUPS, HEADS_PER_GROUP)
    A = -jnp.exp(a_log.astype(f32)).reshape(N_GROUPS, HEADS_PER_GROUP)
    h0g = h0.astype(f32).reshape(Bsz, N_GROUPS, HEADS_PER_GROUP, MAMBA_HEADDIM, D_STATE)
    y, h_fin = ssd_chunked(xs * dt[..., None], dt * A, Bm, Cm, h0g)
    y = y + d_skip.reshape(N_GROUPS, HEADS_PER_GROUP, 1) * xs
    y = y.reshape(Bsz, L, D_MAMBA) * jax.nn.silu(z)
    yg = y.reshape(Bsz, L, N_GROUPS, D_MAMBA // N_GROUPS)
    yg = yg * lax.rsqrt(jnp.mean(yg * yg, axis=-1, keepdims=True) + NORM_EPS)
    y = yg.reshape(Bsz, L, D_MAMBA) * gnorm_w
    return y, ext[:, L:], h_fin.reshape(Bsz, MAMBA_HEADS, MAMBA_HEADDIM, D_STATE)


def hybrid_layer(x, shift_prev, wkv0, conv0, ssm0, norm_w, w_in, mu_shift, w0, w_lora2, a0,
                 a_lora2, k_k, k_a, r_k, lnx_w, lnx_b, conv_w, conv_b, dt_bias, a_log, d_skip,
                 gnorm_w, w_out):
    xn = rmsnorm(x, norm_w)
    proj = xn @ w_in
    prev_sh = shift_prev[:, None, :].astype(jnp.float32) @ w_in[:, :R_SHIFT]
    p_sh, z_r, z_m, xbc, dt_raw = jnp.split(
        proj, [R_SHIFT, D_IN_RWKV, D_IN_RWKV + D_MAMBA, D_IN_RWKV + D_MAMBA + CONV_DIM], axis=-1)
    y_r, wkv_new = rwkv7_mix(p_sh, prev_sh, z_r, wkv0, mu_shift, w0, w_lora2, a0, a_lora2,
                             k_k, k_a, r_k, lnx_w, lnx_b)
    y_m, conv_new, ssm_new = mamba2_mix(z_m, xbc, dt_raw, conv0, ssm0, conv_w, conv_b,
                                        dt_bias, a_log, d_skip, gnorm_w)
    out = jnp.concatenate([y_r, y_m], axis=-1) @ w_out
    return (x + out).astype(x.dtype), xn[:, -1], wkv_new, conv_new, ssm_new


def run_trunk(x, shift, wkv, conv, ssm, layer_params, final_norm_w):
    st_shift, st_wkv, st_conv, st_ssm = [], [], [], []
    for l in range(DEPTH):
        x, s, w, c, h = hybrid_layer(x, shift[l], wkv[l], conv[l], ssm[l],
                                     *[p[l] for p in layer_params])
        st_shift.append(s)
        st_wkv.append(w)
        st_conv.append(c)
        st_ssm.append(h)
    y = rmsnorm(x, final_norm_w).astype(x.dtype)
    return y, jnp.stack(st_shift), jnp.stack(st_wkv), jnp.stack(st_conv), jnp.stack(st_ssm)


def setup_inputs(seed: int = 0) -> dict:
    key = jax.random.key(seed)
    ks = iter(jax.random.split(key, 32))
    f32 = jnp.float32
    nrm = lambda shape, s: s * jax.random.normal(next(ks), shape, f32)
    x_prompt = nrm((BATCH, SEQ, D_MODEL), 1.0)
    x_sample = nrm((DEC_BATCH, DEC_SEQ, D_MODEL), 1.0)
    state_shift = nrm((DEPTH, DEC_BATCH, D_MODEL), 1.0)
    state_wkv = nrm((DEPTH, DEC_BATCH, RWKV_HEADS, RWKV_HEAD, RWKV_HEAD), 0.2)
    state_conv = nrm((DEPTH, DEC_BATCH, CONV_W - 1, CONV_DIM), 1.0)
    state_ssm = nrm((DEPTH, DEC_BATCH, MAMBA_HEADS, MAMBA_HEADDIM, D_STATE), 0.1)
    norm_w = 1.0 + nrm((DEPTH, D_MODEL), 0.02)
    w_in = nrm((DEPTH, D_MODEL, D_IN), D_MODEL ** -0.5)
    mu_shift = jax.random.uniform(next(ks), (DEPTH, R_SHIFT), f32)
    w0 = nrm((DEPTH, D_RWKV), 0.5)
    w_lora2 = nrm((DEPTH, LORA_W, D_RWKV), 0.1)
    a0 = nrm((DEPTH, D_RWKV), 0.1)
    a_lora2 = nrm((DEPTH, LORA_A, D_RWKV), 0.1)
    k_k = 0.85 + nrm((DEPTH, D_RWKV), 0.05)
    k_a = 1.0 + nrm((DEPTH, D_RWKV), 0.05)
    r_k = nrm((DEPTH, RWKV_HEADS, RWKV_HEAD), 0.1)
    lnx_w = 1.0 + nrm((DEPTH, D_RWKV), 0.02)
    lnx_b = nrm((DEPTH, D_RWKV), 0.01)
    conv_w = nrm((DEPTH, CONV_W, CONV_DIM), CONV_W ** -0.5)
    conv_b = nrm((DEPTH, CONV_DIM), 0.01)
    dt0 = jnp.exp(jax.random.uniform(next(ks), (DEPTH, MAMBA_HEADS), f32,
                                     math.log(1e-3), math.log(1e-1)))
    dt_bias = dt0 + jnp.log(-jnp.expm1(-dt0))
    a_log = jnp.log(jax.random.uniform(next(ks), (DEPTH, MAMBA_HEADS), f32, 1.0, 16.0))
    d_skip = 1.0 + nrm((DEPTH, MAMBA_HEADS), 0.1)
    gnorm_w = 1.0 + nrm((DEPTH, D_MAMBA), 0.02)
    w_out = nrm((DEPTH, D_MIX, D_MODEL), D_MIX ** -0.5)
    final_norm_w = 1.0 + nrm((D_MODEL,), 0.02)
    return {"x_prompt": x_prompt, "x_sample": x_sample, "state_shift": state_shift,
            "state_wkv": state_wkv, "state_conv": state_conv, "state_ssm": state_ssm,
            "norm_w": norm_w, "w_in": w_in, "mu_shift": mu_shift, "w0": w0,
            "w_lora2": w_lora2, "a0": a0, "a_lora2": a_lora2, "k_k": k_k, "k_a": k_a,
            "r_k": r_k, "lnx_w": lnx_w, "lnx_b": lnx_b, "conv_w": conv_w, "conv_b": conv_b,
            "dt_bias": dt_bias, "a_log": a_log, "d_skip": d_skip, "gnorm_w": gnorm_w,
            "w_out": w_out, "final_norm_w": final_norm_w}


def reference(x_prompt, x_sample, state_shift, state_wkv, state_conv, state_ssm, norm_w, w_in,
              mu_shift, w0, w_lora2, a0, a_lora2, k_k, k_a, r_k, lnx_w, lnx_b, conv_w, conv_b,
              dt_bias, a_log, d_skip, gnorm_w, w_out, final_norm_w):
    layer_params = (norm_w, w_in, mu_shift, w0, w_lora2, a0, a_lora2, k_k, k_a, r_k, lnx_w,
                    lnx_b, conv_w, conv_b, dt_bias, a_log, d_skip, gnorm_w, w_out)
    f32 = jnp.float32
    bp = x_prompt.shape[0]
    zero_shift = jnp.zeros((DEPTH, bp, D_MODEL), f32)
    zero_wkv = jnp.zeros((DEPTH, bp, RWKV_HEADS, RWKV_HEAD, RWKV_HEAD), f32)
    zero_conv = jnp.zeros((DEPTH, bp, CONV_W - 1, CONV_DIM), f32)
    zero_ssm = jnp.zeros((DEPTH, bp, MAMBA_HEADS, MAMBA_HEADDIM, D_STATE), f32)
    y_prompt, p_shift, p_wkv, p_conv, p_ssm = run_trunk(
        x_prompt, zero_shift, zero_wkv, zero_conv, zero_ssm, layer_params, final_norm_w)
    y_sample, s_shift, s_wkv, s_conv, s_ssm = run_trunk(
        x_sample, state_shift, state_wkv, state_conv, state_ssm, layer_params, final_norm_w)
    return (y_prompt, y_sample, p_shift, p_wkv, p_conv, p_ssm, s_shift, s_wkv, s_conv, s_ssm)
```

```python
import functools

import jax
import jax.numpy as jnp
from jax import lax
from jax.experimental import pallas as pl
from jax.experimental.pallas import tpu as pltpu

F32 = jnp.float32
BF16 = jnp.bfloat16

D_MODEL = 1024
HEAD = 64
N_HEADS = 16
LORA = 64
R_SHIFT = 3 * D_MODEL + 2 * LORA
D_STATE = 128
N_GROUPS = 2
CONV_W = 4
CONV_DIM = D_MODEL + 2 * N_GROUPS * D_STATE
DT_PAD = 128
SEG_SH = (0, R_SHIFT)
SEG_ZR = (R_SHIFT, R_SHIFT + D_MODEL)
SEG_ZM = (SEG_ZR[1], SEG_ZR[1] + D_MODEL)
SEG_XBC = (SEG_ZM[1], SEG_ZM[1] + CONV_DIM)
SEG_DT = (SEG_XBC[1], SEG_XBC[1] + DT_PAD)
D_IN_PAD = SEG_DT[1]
NORM_EPS = 1e-5
LNX_EPS = 64e-5
LANES = 128
PAIR = 2 * HEAD
N_PAIRS = N_HEADS // 2
WKV_CHUNK = 64
SSD_CHUNK = 128
VMEM_LIMIT = 56 * 1024 * 1024


def _cparams(sem):
    return pltpu.CompilerParams(dimension_semantics=sem, vmem_limit_bytes=VMEM_LIMIT)


def _dot(a, b):
    return jnp.dot(a, b, preferred_element_type=F32)


def _dot_nt(a, b):
    return lax.dot_general(a, b, (((1,), (1,)), ((), ())), preferred_element_type=F32)


def _dot_tn(a, b):
    return lax.dot_general(a, b, (((0,), (0,)), ((), ())), preferred_element_type=F32)


def _pieces(x, n):
    out, r = [], x
    for _ in range(n):
        p = r.astype(BF16)
        out.append(p)
        r = r - p.astype(F32)
    return out


def _sigmoid(x):
    return 1.0 / (1.0 + jnp.exp(-x))


def _softplus(x):
    return jnp.maximum(x, 0.0) + jnp.log(1.0 + jnp.exp(-jnp.abs(x)))


def _iota(shape, dim):
    return lax.broadcasted_iota(jnp.int32, shape, dim)


def _proj_kernel(*refs, segs, norm, emit_xn):
    refs = list(refs)
    x_ref = refs.pop(0)
    nw_ref = refs.pop(0) if norm else None
    w_ref = refs.pop(0)
    x = x_ref[...]
    if norm:
        xn = x * lax.rsqrt(jnp.mean(x * x, axis=-1, keepdims=True) + NORM_EPS) * nw_ref[...]
    else:
        xn = x
    xb = xn.astype(BF16)
    for (lo, hi), o_ref in zip(segs, refs):
        o_ref[...] = _dot(xb, w_ref[:, lo:hi])
    if emit_xn:
        refs[len(segs)][...] = xn


def _proj(x, norm_w, w_bf, segs, tm, emit_xn=False):
    m, d = x.shape
    n = w_bf.shape[1]
    norm = norm_w is not None
    ins = [x] + ([norm_w.reshape(1, d)] if norm else []) + [w_bf]
    in_specs = [pl.BlockSpec((tm, d), lambda i: (i, 0))]
    if norm:
        in_specs.append(pl.BlockSpec((1, d), lambda i: (0, 0)))
    in_specs.append(pl.BlockSpec((d, n), lambda i: (0, 0)))
    widths = [hi - lo for lo, hi in segs] + ([d] if emit_xn else [])
    return pl.pallas_call(
        functools.partial(_proj_kernel, segs=tuple(segs), norm=norm, emit_xn=emit_xn),
        grid=(m // tm,),
        in_specs=in_specs,
        out_specs=[pl.BlockSpec((tm, w), lambda i: (i, 0)) for w in widths],
        out_shape=[jax.ShapeDtypeStruct((m, w), F32) for w in widths],
        compiler_params=_cparams(("arbitrary",)),
        name="norm_proj",
    )(*ins)


def _rmsnorm_kernel(x_ref, w_ref, o_ref):
    x = x_ref[...]
    o_ref[...] = x * lax.rsqrt(jnp.mean(x * x, axis=-1, keepdims=True) + NORM_EPS) * w_ref[...]


def _rmsnorm(x, w):
    m, d = x.shape
    return pl.pallas_call(
        _rmsnorm_kernel,
        out_shape=jax.ShapeDtypeStruct((m, d), F32),
        name="rmsnorm_rows",
    )(x, w.reshape(1, d))


def _prep_kernel(*refs, tm, chunk, tiles_per_seq, single):
    (p_ref, prev_ref, mu_ref, w0_ref, a0_ref, kk_ref, ka_ref, w2_ref, a2_ref, bd_ref) = refs[:10]
    refs = list(refs[10:])
    tri_ref = None if single else refs.pop(0)
    o_a, o_b, o_c, o_d, o_v, o_g = refs[:6]
    carry_ref = None if single else refs[6]

    p = p_ref[...]
    if single:
        shifted = prev_ref[...]
    else:
        i = pl.program_id(0)

        @pl.when(i % tiles_per_seq == 0)
        def _():
            carry_ref[...] = jnp.broadcast_to(prev_ref[0], carry_ref.shape)

        rolled = pltpu.roll(p, 1, 0)
        shifted = jnp.where(_iota(p.shape, 0) == 0, carry_ref[0:1, :], rolled)
        carry_ref[...] = jnp.broadcast_to(p[tm - 1:tm, :], carry_ref.shape)
    xm = p + (shifted - p) * mu_ref[...]
    r = xm[:, 0:D_MODEL]
    k = xm[:, D_MODEL:2 * D_MODEL]
    v = xm[:, 2 * D_MODEL:3 * D_MODEL]
    wal = xm[:, 3 * D_MODEL:R_SHIFT]
    dw = _dot(jnp.tanh(wal).astype(BF16), w2_ref[...])
    da = _dot(wal.astype(BF16), a2_ref[...])
    w_pre = -_softplus(-(w0_ref[...] + dw)) - 0.5
    logw = -jnp.exp(w_pre)
    a = _sigmoid(a0_ref[...] + da)

    def head_sum(x):
        hi, lo = _pieces(x, 2)
        cols = []
        for j in range(D_MODEL // 256):
            sl = slice(j * 256, (j + 1) * 256)
            cols.append(_dot(hi[:, sl], bd_ref[...]) + _dot(lo[:, sl], bd_ref[...]))
        return jnp.concatenate(cols, axis=1)

    kk = k * kk_ref[...]
    kap = kk / jnp.maximum(jnp.sqrt(head_sum(kk * kk)), 1e-12)
    k2 = k * (1.0 + (a - 1.0) * ka_ref[...])
    if single:
        o_a[...] = kap
        o_b[...] = kap * a
        o_c[...] = k2
        o_d[...] = r
        o_v[...] = v
        o_g[...] = jnp.exp(logw)
    else:
        cum = sum(_dot(tri_ref[...], piece) for piece in _pieces(logw, 3))
        e_inv = jnp.exp(-cum)
        e_cum = jnp.exp(cum)
        o_a[...] = kap * jnp.exp(cum - logw)
        o_b[...] = kap * a * e_inv
        o_c[...] = k2 * e_inv
        o_d[...] = r * e_cum
        o_v[...] = v
        for j in range(tm // chunk):
            o_g[j] = e_cum[j * chunk + chunk - 1:j * chunk + chunk, :]


def _prep(p_sh, prev, lp, *, seq_len, tm):
    m = p_sh.shape[0]
    single = seq_len == 1
    chunk = 1 if single else WKV_CHUNK
    tiles_per_seq = 1 if single else seq_len // tm
    vec = lambda a, n: a.reshape(1, n)
    bd = (jnp.arange(256)[:, None] // HEAD == jnp.arange(256)[None, :] // HEAD).astype(BF16)
    ins = [p_sh, prev, vec(lp["mu_shift"], R_SHIFT), vec(lp["w0"], D_MODEL), vec(lp["a0"], D_MODEL),
           vec(lp["k_k"], D_MODEL), vec(lp["k_a"], D_MODEL), lp["w2p"], lp["a2p"], bd]
    full = lambda shape: pl.BlockSpec(shape, lambda i: tuple(0 for _ in shape))
    if single:
        prev_spec = pl.BlockSpec((tm, R_SHIFT), lambda i: (i, 0))
    else:
        prev_spec = pl.BlockSpec((1, 1, R_SHIFT), lambda i: (i // tiles_per_seq, 0, 0))
    in_specs = [pl.BlockSpec((tm, R_SHIFT), lambda i: (i, 0)), prev_spec, full((1, R_SHIFT)),
                full((1, D_MODEL)), full((1, D_MODEL)), full((1, D_MODEL)), full((1, D_MODEL)),
                full((2 * LORA, D_MODEL)), full((2 * LORA, D_MODEL)), full((256, 256))]
    scratch = []
    if not single:
        t = jnp.arange(tm)
        tri = ((t[:, None] // chunk == t[None, :] // chunk) & (t[None, :] <= t[:, None])).astype(BF16)
        ins.append(tri)
        in_specs.append(full((tm, tm)))
        scratch.append(pltpu.VMEM((8, R_SHIFT), F32))
    tok = pl.BlockSpec((tm, D_MODEL), lambda i: (i, 0))
    tok_shape = jax.ShapeDtypeStruct((m, D_MODEL), F32)
    if single:
        g_spec, g_shape = tok, tok_shape
    else:
        g_spec = pl.BlockSpec((tm // chunk, 1, D_MODEL), lambda i: (i, 0, 0))
        g_shape = jax.ShapeDtypeStruct((m // chunk, 1, D_MODEL), F32)
    return pl.pallas_call(
        functools.partial(_prep_kernel, tm=tm, chunk=chunk, tiles_per_seq=tiles_per_seq, single=single),
        grid=(m // tm,),
        in_specs=in_specs,
        out_specs=[tok] * 5 + [g_spec],
        out_shape=[tok_shape] * 5 + [g_shape],
        scratch_shapes=scratch,
        compiler_params=_cparams(("arbitrary",)),
        name="rwkv_prep",
    )(*ins)


def _stack(x):
    first = _iota(x.shape, 1) < HEAD
    return jnp.concatenate([jnp.where(first, x, 0.0), jnp.where(first, 0.0, x)], axis=0)


def _wkv_post(o_s, rs, ks, vs, rk_row, lw_row, lb_row, z, rows):
    own = (_iota(o_s.shape, 0) >> (rows.bit_length() - 1)) == (_iota(o_s.shape, 1) >> (HEAD.bit_length() - 1))
    mean = jnp.sum(o_s, axis=-1, keepdims=True) * (1.0 / HEAD)
    dev = jnp.where(own, o_s - mean, 0.0)
    var = jnp.sum(dev * dev, axis=-1, keepdims=True) * (1.0 / HEAD)
    normed = dev * lax.rsqrt(var + LNX_EPS)
    bonus = jnp.sum(rs * ks * rk_row, axis=-1, keepdims=True) * vs
    fold = lambda t: t[0:rows] + t[rows:2 * rows]
    y = fold(normed) * lw_row + lb_row + fold(bonus)
    return y * (z * _sigmoid(z))


def _wkv_chunk_kernel(*refs, chunk, has_state):
    (kt_ref, bh_ref, kh_ref, rt_ref, v_ref, z_ref, gc_ref, rk_ref, lw_ref, lb_ref) = refs[:10]
    refs = refs[10:]
    if has_state:
        s0_ref, y_ref, sout_ref, s_scr = refs
    else:
        y_ref, sout_ref, s_scr = refs
    c = pl.program_id(2)
    n2 = 2 * chunk

    @pl.when(c == 0)
    def _():
        s_scr[...] = s0_ref[0, 0] if has_state else jnp.zeros(s_scr.shape, F32)

    xs = _stack(kt_ref[...])
    bs = _stack(bh_ref[...])
    ks = _stack(kh_ref[...])
    rs = _stack(rt_ref[...])
    vs = _stack(v_ref[...])
    g_row = gc_ref[0]
    xb, bb, kb, rb, vb = (t.astype(BF16) for t in (xs, bs, ks, rs, vs))

    row = _iota((n2, n2), 0)
    col = _iota((n2, n2), 1)
    same = (row >> (chunk.bit_length() - 1)) == (col >> (chunk.bit_length() - 1))
    rt_ = row & (chunk - 1)
    ct_ = col & (chunk - 1)
    strict = same & (ct_ < rt_)
    incl = same & (ct_ <= rt_)

    a_kb = jnp.where(strict, _dot_nt(xb, bb), 0.0)
    a_kk = jnp.where(strict, _dot_nt(xb, kb), 0.0)
    a_rk = jnp.where(incl, _dot_nt(rb, kb), 0.0)
    a_rb = jnp.where(incl, _dot_nt(rb, bb), 0.0)

    eye = (row == col).astype(F32)
    t_inv = eye - jnp.where((rt_ | 1) == (ct_ | 1), a_kb, 0.0)
    s = 2
    while s < chunk:
        off = same & ((rt_ & ~(2 * s - 1)) == (ct_ & ~(2 * s - 1))) & ((rt_ & s) != 0) & ((ct_ & s) == 0)
        b_off = jnp.where(off, a_kb, 0.0).astype(BF16)
        t_b = t_inv.astype(BF16)
        t_inv = t_inv - _dot(t_b, _dot(b_off, t_b).astype(BF16))
        s *= 2
    t_b = t_inv.astype(BF16)

    x1 = _dot(a_kk.astype(BF16), vb)
    wu = _dot(t_b, jnp.concatenate([xs, x1], axis=1).astype(BF16))
    wu_b = wu.astype(BF16)
    pq = jnp.concatenate([rs, _dot(a_rk.astype(BF16), vb)], axis=1) - _dot(a_rb.astype(BF16), wu_b)
    p_m = pq[:, :PAIR]
    q_m = pq[:, PAIR:]
    w_b = wu_b[:, :PAIR]
    u_b = wu_b[:, PAIR:]
    m_low = _dot_tn(w_b, bb) * g_row
    n_t = (_dot_tn(vb, kb) - _dot_tn(u_b, bb)) * g_row

    s0 = s_scr[...]
    s0_b = s0.astype(BF16)
    o_s = _dot_nt(p_m.astype(BF16), s0_b) + q_m
    s_scr[...] = s0 * g_row - _dot(s0_b, m_low.astype(BF16)) + n_t

    y_ref[...] = _wkv_post(o_s, rs, ks, vs, rk_ref[...], lw_ref[...], lb_ref[...], z_ref[...], chunk)

    @pl.when(c == pl.num_programs(2) - 1)
    def _():
        sout_ref[0, 0] = s_scr[...]


def _wkv_chunk(kt, bh, kh, rt, v, z, gc, lp, s0_bd, *, batch, seq_len):
    chunk = WKV_CHUNK
    nc = seq_len // chunk
    m = batch * seq_len
    has_state = s0_bd is not None
    tok = pl.BlockSpec((chunk, PAIR), lambda b, g, c: (b * nc + c, g))
    par = pl.BlockSpec((1, PAIR), lambda b, g, c: (0, g))
    st = pl.BlockSpec((1, 1, PAIR, PAIR), lambda b, g, c: (b, g, 0, 0))
    ins = [kt, bh, kh, rt, v, z, gc, lp["r_k"].reshape(1, D_MODEL), lp["lnx_w"].reshape(1, D_MODEL),
           lp["lnx_b"].reshape(1, D_MODEL)]
    in_specs = [tok] * 6 + [pl.BlockSpec((1, 1, PAIR), lambda b, g, c: (b * nc + c, 0, g)), par, par, par]
    if has_state:
        ins.append(s0_bd)
        in_specs.append(st)
    return pl.pallas_call(
        functools.partial(_wkv_chunk_kernel, chunk=chunk, has_state=has_state),
        grid=(batch, N_PAIRS, nc),
        in_specs=in_specs,
        out_specs=[tok, st],
        out_shape=[jax.ShapeDtypeStruct((m, D_MODEL), F32),
                   jax.ShapeDtypeStruct((batch, N_PAIRS, PAIR, PAIR), F32)],
        scratch_shapes=[pltpu.VMEM((PAIR, PAIR), F32)],
        compiler_params=_cparams(("arbitrary", "arbitrary", "arbitrary")),
        name="rwkv_chunk",
    )(*ins)


def _wkv_step_kernel(kap_ref, b_ref, k_ref, r_ref, v_ref, w_ref, z_ref, rk_ref, lw_ref, lb_ref, s0_ref,
                     y_ref, sout_ref, *, nseq):
    rows = HEAD
    y_tile = jnp.zeros((nseq, PAIR), F32)
    rowid = _iota((nseq, PAIR), 0)
    inv = 1.0 / rows
    for j in range(nseq):
        rep = lambda ref: _stack(jnp.broadcast_to(ref[j:j + 1, :], (rows, PAIR)))
        xs, bs, ks, rs, vs = rep(kap_ref), rep(b_ref), rep(k_ref), rep(r_ref), rep(v_ref)
        w_row = w_ref[j:j + 1, :]
        s0 = s0_ref[j, 0]
        m_low = _dot_tn(xs.astype(BF16), bs.astype(BF16)) * inv
        n_t = _dot_tn(vs.astype(BF16), ks.astype(BF16)) * inv
        s1 = s0 * w_row - _dot(s0.astype(BF16), m_low.astype(BF16)) + n_t
        sout_ref[j, 0] = s1
        o_s = _dot_nt(rs.astype(BF16), s1.astype(BF16))
        z_rep = jnp.broadcast_to(z_ref[j:j + 1, :], (rows, PAIR))
        y = _wkv_post(o_s, rs, ks, vs, rk_ref[...], lw_ref[...], lb_ref[...], z_rep, rows)
        y_tile = jnp.where(rowid == j, y[0:nseq, :], y_tile)
    y_ref[...] = y_tile


def _wkv_step(kap, b, k2, r, v, w, z, lp, s0_bd):
    m = kap.shape[0]
    nseq = 8
    tok = pl.BlockSpec((nseq, PAIR), lambda i, g: (i, g))
    par = pl.BlockSpec((1, PAIR), lambda i, g: (0, g))
    st = pl.BlockSpec((nseq, 1, PAIR, PAIR), lambda i, g: (i, g, 0, 0))
    return pl.pallas_call(
        functools.partial(_wkv_step_kernel, nseq=nseq),
        grid=(m // nseq, N_PAIRS),
        in_specs=[tok] * 7 + [par, par, par, st],
        out_specs=[tok, st],
        out_shape=[jax.ShapeDtypeStruct((m, D_MODEL), F32),
                   jax.ShapeDtypeStruct((m, N_PAIRS, PAIR, PAIR), F32)],
        compiler_params=_cparams(("arbitrary", "arbitrary")),
        name="rwkv_step",
    )(kap, b, k2, r, v, w, z, lp["r_k"].reshape(1, D_MODEL), lp["lnx_w"].reshape(1, D_MODEL),
      lp["lnx_b"].reshape(1, D_MODEL), s0_bd)


def _to_block_diag(s):
    b = s.shape[0]
    sp = s.reshape(b, N_PAIRS, 2, HEAD, HEAD)
    eye = jnp.eye(2, dtype=s.dtype)
    return jnp.einsum("bpivk,ij->bpivjk", sp, eye).reshape(b, N_PAIRS, PAIR, PAIR)


def _from_block_diag(s):
    b = s.shape[0]
    sp = s.reshape(b, N_PAIRS, 2, HEAD, 2, HEAD)
    return jnp.stack([sp[:, :, 0, :, 0, :], sp[:, :, 1, :, 1, :]], axis=2).reshape(b, N_HEADS, HEAD, HEAD)


def _gate_norm(y, xs, zm, dsk_row, gn_row):
    y = (y + dsk_row * xs) * (zm * _sigmoid(zm))
    half = D_MODEL // N_GROUPS
    outs = []
    for g in range(N_GROUPS):
        yg = y[:, g * half:(g + 1) * half]
        outs.append(yg * lax.rsqrt(jnp.mean(yg * yg, axis=-1, keepdims=True) + NORM_EPS))
    return jnp.concatenate(outs, axis=1) * gn_row


def _ssd_chunk_kernel(*refs, q, has_state):
    (xbc_ref, zm_ref, dt_ref, cw_ref, cb_ref, dtb_ref, alog_ref, dsk_ref, gn_ref, e_ref, et_ref) = refs[:11]
    refs = refs[11:]
    if has_state:
        c0_ref, h0_ref, y_ref, cout_ref, hout_ref, ext, h_scr = refs
    else:
        y_ref, cout_ref, hout_ref, ext, h_scr = refs
    c = pl.program_id(1)

    @pl.when(c == 0)
    def _():
        if has_state:
            ext[0:8, :] = c0_ref[0]
            h_scr[...] = h0_ref[0]
        else:
            ext[0:8, :] = jnp.zeros((8, CONV_DIM), F32)
            h_scr[...] = jnp.zeros(h_scr.shape, F32)

    ext[8:8 + q, :] = xbc_ref[...]
    u = cb_ref[...] + cw_ref[3:4, :] * ext[8:8 + q, :]
    for i in range(CONV_W - 1):
        u = u + cw_ref[i:i + 1, :] * ext[5 + i:5 + i + q, :]
    tail = ext[q:q + 8, :]
    ext[0:8, :] = tail
    u = u * _sigmoid(u)
    xs = u[:, :D_MODEL]
    bm = [u[:, D_MODEL + g * D_STATE:D_MODEL + (g + 1) * D_STATE].astype(BF16) for g in range(N_GROUPS)]
    cm = [u[:, D_MODEL + (N_GROUPS + g) * D_STATE:D_MODEL + (N_GROUPS + g + 1) * D_STATE].astype(BF16)
          for g in range(N_GROUPS)]

    dt = _softplus(dt_ref[...] + dtb_ref[...])
    d_a = dt * (-jnp.exp(alog_ref[...]))
    row = _iota((q, q), 0)
    col = _iota((q, q), 1)
    causal = col <= row
    tril = causal.astype(BF16)
    cs = sum(_dot(tril, piece) for piece in _pieces(d_a, 3))
    eye = (row == col).astype(BF16)
    cs_t = sum(_dot_tn(piece, eye) for piece in _pieces(cs, 3))
    last = cs[q - 1:q, :]
    expand = lambda t: sum(_dot(piece, e_ref[...]) for piece in _pieces(t, 2))
    dt_x = expand(dt)
    dte_x = expand(jnp.exp(last - cs))
    ecs_x = expand(jnp.exp(cs))
    xdt = xs * dt_x
    cd = jnp.exp(cs_t[:, q - 1:q])
    cd_b = jnp.broadcast_to(cd, (LANES, D_STATE))
    decay_full = sum(_dot(et_ref[...], piece) for piece in _pieces(cd_b, 2))

    ys = []
    for g in range(N_GROUPS):
        cb_g = _dot_nt(cm[g], bm[g])
        for j in range(N_PAIRS // N_GROUPS):
            pi = g * (N_PAIRS // N_GROUPS) + j
            lanes = slice(pi * PAIR, (pi + 1) * PAIR)
            gs = []
            for hh in (2 * pi, 2 * pi + 1):
                seg = cs[:, hh:hh + 1] - cs_t[hh:hh + 1, :]
                gs.append(cb_g * jnp.exp(jnp.where(causal, seg, -1e30)))
            g_pair = jnp.concatenate(gs, axis=1).astype(BF16)
            xdt_p = xdt[:, lanes]
            y_diag = _dot(g_pair, _stack(xdt_p).astype(BF16))
            h_p = h_scr[pi * PAIR:(pi + 1) * PAIR, :]
            y_off = _dot_nt(cm[g], h_p.astype(BF16)) * ecs_x[:, lanes]
            ys.append(y_diag + y_off)
            states = _dot_tn((xdt_p * dte_x[:, lanes]).astype(BF16), bm[g])
            h_scr[pi * PAIR:(pi + 1) * PAIR, :] = decay_full[pi * PAIR:(pi + 1) * PAIR, :] * h_p + states
    y = jnp.concatenate(ys, axis=1)
    y_ref[...] = _gate_norm(y, xs, zm_ref[...], dsk_ref[...], gn_ref[...])

    @pl.when(c == pl.num_programs(1) - 1)
    def _():
        cout_ref[0] = ext[8 + q - (CONV_W - 1):8 + q, :]
        hout_ref[0] = h_scr[...]


def _ssd_consts(lp):
    heads = jnp.arange(LANES)[:, None]
    cols = jnp.arange(D_MODEL)[None, :] // HEAD
    e_mat = (heads == cols).astype(BF16)
    pad = lambda a: jnp.pad(a.reshape(1, N_HEADS), ((0, 0), (0, DT_PAD - N_HEADS)))
    return dict(cw=lp["conv_w"], cb=lp["conv_b"].reshape(1, CONV_DIM), dtb=pad(lp["dt_bias"]),
                alog=pad(lp["a_log"]), dsk=jnp.repeat(lp["d_skip"], HEAD).reshape(1, D_MODEL),
                gn=lp["gnorm_w"].reshape(1, D_MODEL), e=e_mat, et=e_mat.T)


def _ssd_chunk(xbc, zm, dt, lp, c0_pad, h0, *, batch, seq_len):
    q = SSD_CHUNK
    nq = seq_len // q
    m = batch * seq_len
    has_state = h0 is not None
    k = _ssd_consts(lp)
    full = lambda shape: pl.BlockSpec(shape, lambda b, c: tuple(0 for _ in shape))
    rows = lambda w: pl.BlockSpec((q, w), lambda b, c: (b * nq + c, 0))
    ins = [xbc, zm, dt, k["cw"], k["cb"], k["dtb"], k["alog"], k["dsk"], k["gn"], k["e"], k["et"]]
    in_specs = [rows(CONV_DIM), rows(D_MODEL), rows(DT_PAD), full((CONV_W, CONV_DIM)), full((1, CONV_DIM)),
                full((1, DT_PAD)), full((1, DT_PAD)), full((1, D_MODEL)), full((1, D_MODEL)),
                full((LANES, D_MODEL)), full((D_MODEL, LANES))]
    if has_state:
        ins += [c0_pad, h0]
        in_specs += [pl.BlockSpec((1, 8, CONV_DIM), lambda b, c: (b, 0, 0)),
                     pl.BlockSpec((1, D_MODEL, D_STATE), lambda b, c: (b, 0, 0))]
    return pl.pallas_call(
        functools.partial(_ssd_chunk_kernel, q=q, has_state=has_state),
        grid=(batch, nq),
        in_specs=in_specs,
        out_specs=[rows(D_MODEL), pl.BlockSpec((1, CONV_W - 1, CONV_DIM), lambda b, c: (b, 0, 0)),
                   pl.BlockSpec((1, D_MODEL, D_STATE), lambda b, c: (b, 0, 0))],
        out_shape=[jax.ShapeDtypeStruct((m, D_MODEL), F32),
                   jax.ShapeDtypeStruct((batch, CONV_W - 1, CONV_DIM), F32),
                   jax.ShapeDtypeStruct((batch, D_MODEL, D_STATE), F32)],
        scratch_shapes=[pltpu.VMEM((q + 8, CONV_DIM), F32), pltpu.VMEM((D_MODEL, D_STATE), F32)],
        compiler_params=_cparams(("arbitrary", "arbitrary")),
        name="ssd_chunk",
    )(*ins)


def _ssd_step_kernel(xbc_ref, c0_ref, c1_ref, c2_ref, zm_ref, dt_ref, cw_ref, cb_ref, dtb_ref, alog_ref,
                     dsk_ref, gn_ref, e_ref, h0_ref, y_ref, hout_ref, *, nseq):
    u = (cb_ref[...] + cw_ref[3:4, :] * xbc_ref[...] + cw_ref[2:3, :] * c2_ref[...]
         + cw_ref[1:2, :] * c1_ref[...] + cw_ref[0:1, :] * c0_ref[...])
    u = u * _sigmoid(u)
    xs = u[:, :D_MODEL]
    dt = _softplus(dt_ref[...] + dtb_ref[...])
    ed = jnp.exp(dt * (-jnp.exp(alog_ref[...])))
    expand = lambda t: sum(_dot(piece, e_ref[...]) for piece in _pieces(t, 2))
    xdt = xs * expand(dt)
    ed_x = expand(ed)
    half = D_MODEL // N_GROUPS
    ones = jnp.full((LANES, D_STATE), 1.0 / LANES, BF16)
    rowid = _iota((nseq, half), 0)
    y_groups = [jnp.zeros((nseq, half), F32) for _ in range(N_GROUPS)]
    for j in range(nseq):
        for g in range(N_GROUPS):
            lanes = slice(g * half, (g + 1) * half)
            b_g = u[j:j + 1, D_MODEL + g * D_STATE:D_MODEL + (g + 1) * D_STATE]
            c_g = u[j:j + 1, D_MODEL + (N_GROUPS + g) * D_STATE:D_MODEL + (N_GROUPS + g + 1) * D_STATE]
            x_rep = jnp.broadcast_to(xdt[j:j + 1, lanes], (LANES, half)).astype(BF16)
            b_rep = jnp.broadcast_to(b_g, (LANES, D_STATE)).astype(BF16)
            outer = _dot_tn(x_rep, b_rep) * (1.0 / LANES)
            d_rep = jnp.broadcast_to(ed_x[j:j + 1, lanes], (LANES, half))
            dec = sum(_dot_tn(piece, ones) for piece in _pieces(d_rep, 2))
            h1 = dec * h0_ref[j, lanes, :] + outer
            hout_ref[j, lanes, :] = h1
            c_rep = jnp.broadcast_to(c_g, (8, D_STATE)).astype(BF16)
            y_g = _dot_nt(c_rep, h1.astype(BF16))
            y_groups[g] = jnp.where(rowid == j, y_g[0:nseq, :], y_groups[g])
    y = jnp.concatenate(y_groups, axis=1)
    y_ref[...] = _gate_norm(y, xs, zm_ref[...], dsk_ref[...], gn_ref[...])


def _ssd_step(xbc, conv0, zm, dt, lp, h0):
    m = xbc.shape[0]
    nseq = 8
    k = _ssd_consts(lp)
    full = lambda shape: pl.BlockSpec(shape, lambda i: tuple(0 for _ in shape))
    rows = lambda w: pl.BlockSpec((nseq, w), lambda i: (i, 0))
    st = pl.BlockSpec((nseq, D_MODEL, D_STATE), lambda i: (i, 0, 0))
    y, h1 = pl.pallas_call(
        functools.partial(_ssd_step_kernel, nseq=nseq),
        grid=(m // nseq,),
        in_specs=[rows(CONV_DIM)] * 4 + [rows(D_MODEL), rows(DT_PAD), full((CONV_W, CONV_DIM)),
                                          full((1, CONV_DIM)), full((1, DT_PAD)), full((1, DT_PAD)),
                                          full((1, D_MODEL)), full((1, D_MODEL)), full((LANES, D_MODEL)), st],
        out_specs=[rows(D_MODEL), st],
        out_shape=[jax.ShapeDtypeStruct((m, D_MODEL), F32), jax.ShapeDtypeStruct((m, D_MODEL, D_STATE), F32)],
        compiler_params=_cparams(("arbitrary",)),
        name="ssd_step",
    )(xbc, conv0[:, 0], conv0[:, 1], conv0[:, 2], zm, dt, k["cw"], k["cb"], k["dtb"], k["alog"], k["dsk"],
      k["gn"], k["e"], h0)
    conv_new = jnp.stack([conv0[:, 1], conv0[:, 2], xbc], axis=1)
    return y, conv_new, h1


def _out_kernel(*refs, final):
    if final:
        x_ref, yr_ref, ym_ref, w_ref, fw_ref, o_ref = refs
    else:
        x_ref, yr_ref, ym_ref, w_ref, o_ref = refs
    out = (x_ref[...] + _dot(yr_ref[...].astype(BF16), w_ref[0:D_MODEL, :])
           + _dot(ym_ref[...].astype(BF16), w_ref[D_MODEL:2 * D_MODEL, :]))
    if final:
        out = out * lax.rsqrt(jnp.mean(out * out, axis=-1, keepdims=True) + NORM_EPS) * fw_ref[...]
    o_ref[...] = out


def _out_proj(x, yr, ym, w_bf, final_w, tm):
    m, d = x.shape
    final = final_w is not None
    tok = pl.BlockSpec((tm, d), lambda i: (i, 0))
    ins = [x, yr, ym, w_bf] + ([final_w.reshape(1, d)] if final else [])
    in_specs = [tok, tok, tok, pl.BlockSpec((2 * d, d), lambda i: (0, 0))]
    if final:
        in_specs.append(pl.BlockSpec((1, d), lambda i: (0, 0)))
    return pl.pallas_call(
        functools.partial(_out_kernel, final=final),
        grid=(m // tm,),
        in_specs=in_specs,
        out_specs=tok,
        out_shape=jax.ShapeDtypeStruct((m, d), F32),
        compiler_params=_cparams(("arbitrary",)),
        name="out_proj",
    )(*ins)


ALL_SEGS = (SEG_SH, SEG_ZR, SEG_ZM, SEG_XBC, SEG_DT)


def _layer_prompt(x, lp, final_w, *, batch, seq_len):
    tm = 256
    p_sh, z_r, z_m, xbc, dt = _proj(x, lp["norm_w"], lp["w_in"], ALL_SEGS, tm)
    xn_last = _rmsnorm(x.reshape(batch, seq_len, D_MODEL)[:, -1], lp["norm_w"])
    prev = jnp.zeros((batch, 1, R_SHIFT), F32)
    kt, bh, kh, rt, v, gc = _prep(p_sh, prev, lp, seq_len=seq_len, tm=tm)
    y_r, s_bd = _wkv_chunk(kt, bh, kh, rt, v, z_r, gc, lp, None, batch=batch, seq_len=seq_len)
    y_m, conv_new, ssm_new = _ssd_chunk(xbc, z_m, dt, lp, None, None, batch=batch, seq_len=seq_len)
    x_new = _out_proj(x, y_r, y_m, lp["w_out"], final_w, tm)
    return (x_new, xn_last, _from_block_diag(s_bd), conv_new,
            ssm_new.reshape(batch, N_HEADS, HEAD, D_STATE))


def _layer_sample(x, shift_prev, wkv0, conv0, ssm0, lp, final_w):
    m = x.shape[0]
    tm = m
    p_sh, z_r, z_m, xbc, dt, xn = _proj(x, lp["norm_w"], lp["w_in"], ALL_SEGS, tm, emit_xn=True)
    (prev,) = _proj(shift_prev, None, lp["w_in"], (SEG_SH,), tm)
    kap, b, k2, r, v, w = _prep(p_sh, prev, lp, seq_len=1, tm=tm)
    y_r, s_bd = _wkv_step(kap, b, k2, r, v, w, z_r, lp, _to_block_diag(wkv0))
    y_m, conv_new, ssm_new = _ssd_step(xbc, conv0, z_m, dt, lp, ssm0.reshape(m, D_MODEL, D_STATE))
    x_new = _out_proj(x, y_r, y_m, lp["w_out"], final_w, tm)
    return (x_new, xn, _from_block_diag(s_bd), conv_new, ssm_new.reshape(m, N_HEADS, HEAD, D_STATE))


def _layer_params(params, l):
    lp = {k: v[l] for k, v in params.items()}
    lp["w_in"] = jnp.pad(lp["w_in"], ((0, 0), (0, D_IN_PAD - lp["w_in"].shape[1]))).astype(BF16)
    lp["w_out"] = lp["w_out"].astype(BF16)
    zeros = jnp.zeros((LORA, D_MODEL), F32)
    lp["w2p"] = jnp.concatenate([lp["w_lora2"], zeros], axis=0).astype(BF16)
    lp["a2p"] = jnp.concatenate([zeros, lp["a_lora2"]], axis=0).astype(BF16)
    return lp


def kernel(x_prompt, x_sample, state_shift, state_wkv, state_conv, state_ssm, norm_w, w_in, mu_shift, w0,
           w_lora2, a0, a_lora2, k_k, k_a, r_k, lnx_w, lnx_b, conv_w, conv_b, dt_bias, a_log, d_skip,
           gnorm_w, w_out, final_norm_w):
    params = dict(norm_w=norm_w, w_in=w_in, mu_shift=mu_shift, w0=w0, w_lora2=w_lora2, a0=a0,
                  a_lora2=a_lora2, k_k=k_k, k_a=k_a, r_k=r_k, lnx_w=lnx_w, lnx_b=lnx_b, conv_w=conv_w,
                  conv_b=conv_b, dt_bias=dt_bias, a_log=a_log, d_skip=d_skip, gnorm_w=gnorm_w, w_out=w_out)
    depth = norm_w.shape[0]
    bp, lseq, d = x_prompt.shape
    bs = x_sample.shape[0]
    xp = x_prompt.reshape(bp * lseq, d)
    xs = x_sample.reshape(bs, d)
    p_states, s_states = [], []
    for l in range(depth):
        lp = _layer_params(params, l)
        fw = final_norm_w if l == depth - 1 else None
        xp, *st = _layer_prompt(xp, lp, fw, batch=bp, seq_len=lseq)
        p_states.append(st)
        xs, *st = _layer_sample(xs, state_shift[l], state_wkv[l], state_conv[l], state_ssm[l], lp, fw)
        s_states.append(st)
    stack = lambda states, i: jnp.stack([s[i] for s in states])
    return (xp.reshape(bp, lseq, d), xs.reshape(bs, 1, d),
            stack(p_states, 0), stack(p_states, 1), stack(p_states, 2), stack(p_states, 3),
            stack(s_states, 0), stack(s_states, 1), stack(s_states, 2), stack(s_states, 3))
```

```python
import functools

import jax
import jax.numpy as jnp
from jax import lax
from jax.experimental import pallas as pl
from jax.experimental.pallas import tpu as pltpu

F32 = jnp.float32
BF16 = jnp.bfloat16

D_MODEL = 1024
HEAD = 64
N_HEADS = 16
LORA = 64
R_SHIFT = 3 * D_MODEL + 2 * LORA
D_STATE = 128
N_GROUPS = 2
CONV_W = 4
CONV_DIM = D_MODEL + 2 * N_GROUPS * D_STATE
DT_PAD = 128
SEG_SH = (0, R_SHIFT)
SEG_ZR = (R_SHIFT, R_SHIFT + D_MODEL)
SEG_ZM = (SEG_ZR[1], SEG_ZR[1] + D_MODEL)
SEG_XBC = (SEG_ZM[1], SEG_ZM[1] + CONV_DIM)
SEG_DT = (SEG_XBC[1], SEG_XBC[1] + DT_PAD)
D_IN_PAD = SEG_DT[1]
NORM_EPS = 1e-5
LNX_EPS = 64e-5
LANES = 128
PAIR = 2 * HEAD
N_PAIRS = N_HEADS // 2
WKV_CHUNK = 64
SSD_CHUNK = 128
WKV_PAIRS_PER_STEP = 8
VMEM_LIMIT = 56 * 1024 * 1024


def _cparams(sem):
    return pltpu.CompilerParams(dimension_semantics=sem, vmem_limit_bytes=VMEM_LIMIT)


def _dot(a, b):
    return jnp.dot(a, b, preferred_element_type=F32)


def _dot_nt(a, b):
    return lax.dot_general(a, b, (((1,), (1,)), ((), ())), preferred_element_type=F32)


def _dot_tn(a, b):
    return lax.dot_general(a, b, (((0,), (0,)), ((), ())), preferred_element_type=F32)


def _pieces(x, n):
    out, r = [], x
    for _ in range(n):
        p = r.astype(BF16)
        out.append(p)
        r = r - p.astype(F32)
    return out


def _sigmoid(x):
    return 1.0 / (1.0 + jnp.exp(-x))


def _softplus(x):
    return jnp.maximum(x, 0.0) + jnp.log(1.0 + jnp.exp(-jnp.abs(x)))


def _iota(shape, dim):
    return lax.broadcasted_iota(jnp.int32, shape, dim)


def _proj_kernel(*refs, segs, norm, emit_xn):
    refs = list(refs)
    x_ref = refs.pop(0)
    nw_ref = refs.pop(0) if norm else None
    w_ref = refs.pop(0)
    x = x_ref[...]
    if norm:
        xn = x * lax.rsqrt(jnp.mean(x * x, axis=-1, keepdims=True) + NORM_EPS) * nw_ref[...]
    else:
        xn = x
    xb = xn.astype(BF16)
    for (lo, hi), o_ref in zip(segs, refs):
        o_ref[...] = _dot(xb, w_ref[:, lo:hi])
    if emit_xn:
        refs[len(segs)][...] = xn


def _proj(x, norm_w, w_bf, segs, tm, emit_xn=False):
    m, d = x.shape
    n = w_bf.shape[1]
    norm = norm_w is not None
    ins = [x] + ([norm_w.reshape(1, d)] if norm else []) + [w_bf]
    in_specs = [pl.BlockSpec((tm, d), lambda i: (i, 0))]
    if norm:
        in_specs.append(pl.BlockSpec((1, d), lambda i: (0, 0)))
    in_specs.append(pl.BlockSpec((d, n), lambda i: (0, 0)))
    widths = [hi - lo for lo, hi in segs] + ([d] if emit_xn else [])
    return pl.pallas_call(
        functools.partial(_proj_kernel, segs=tuple(segs), norm=norm, emit_xn=emit_xn),
        grid=(m // tm,),
        in_specs=in_specs,
        out_specs=[pl.BlockSpec((tm, w), lambda i: (i, 0)) for w in widths],
        out_shape=[jax.ShapeDtypeStruct((m, w), F32) for w in widths],
        compiler_params=_cparams(("arbitrary",)),
        name="norm_proj",
    )(*ins)


def _rmsnorm_kernel(x_ref, w_ref, o_ref):
    x = x_ref[...]
    o_ref[...] = x * lax.rsqrt(jnp.mean(x * x, axis=-1, keepdims=True) + NORM_EPS) * w_ref[...]


def _rmsnorm(x, w):
    m, d = x.shape
    return pl.pallas_call(
        _rmsnorm_kernel,
        out_shape=jax.ShapeDtypeStruct((m, d), F32),
        name="rmsnorm_rows",
    )(x, w.reshape(1, d))


def _prep_kernel(*refs, tm, chunk, tiles_per_seq, single):
    (p_ref, prev_ref, mu_ref, w0_ref, a0_ref, kk_ref, ka_ref, w2_ref, a2_ref, bd_ref) = refs[:10]
    refs = list(refs[10:])
    tri_ref = None if single else refs.pop(0)
    o_a, o_b, o_c, o_d, o_v, o_g = refs[:6]
    carry_ref = None if single else refs[6]

    p = p_ref[...]
    if single:
        shifted = prev_ref[...]
    else:
        i = pl.program_id(0)

        @pl.when(i % tiles_per_seq == 0)
        def _():
            carry_ref[...] = jnp.broadcast_to(prev_ref[0], carry_ref.shape)

        rolled = pltpu.roll(p, 1, 0)
        shifted = jnp.where(_iota(p.shape, 0) == 0, carry_ref[0:1, :], rolled)
        carry_ref[...] = jnp.broadcast_to(p[tm - 1:tm, :], carry_ref.shape)
    xm = p + (shifted - p) * mu_ref[...]
    r = xm[:, 0:D_MODEL]
    k = xm[:, D_MODEL:2 * D_MODEL]
    v = xm[:, 2 * D_MODEL:3 * D_MODEL]
    wal = xm[:, 3 * D_MODEL:R_SHIFT]
    dw = _dot(jnp.tanh(wal).astype(BF16), w2_ref[...])
    da = _dot(wal.astype(BF16), a2_ref[...])
    w_pre = -_softplus(-(w0_ref[...] + dw)) - 0.5
    logw = -jnp.exp(w_pre)
    a = _sigmoid(a0_ref[...] + da)

    def head_sum(x):
        hi, lo = _pieces(x, 2)
        cols = []
        for j in range(D_MODEL // 256):
            sl = slice(j * 256, (j + 1) * 256)
            cols.append(_dot(hi[:, sl], bd_ref[...]) + _dot(lo[:, sl], bd_ref[...]))
        return jnp.concatenate(cols, axis=1)

    kk = k * kk_ref[...]
    kap = kk / jnp.maximum(jnp.sqrt(head_sum(kk * kk)), 1e-12)
    k2 = k * (1.0 + (a - 1.0) * ka_ref[...])
    if single:
        o_a[...] = kap
        o_b[...] = kap * a
        o_c[...] = k2
        o_d[...] = r
        o_v[...] = v
        o_g[...] = jnp.exp(logw)
    else:
        cum = sum(_dot(tri_ref[...], piece) for piece in _pieces(logw, 3))
        e_inv = jnp.exp(-cum)
        e_cum = jnp.exp(cum)
        o_a[...] = kap * jnp.exp(cum - logw)
        o_b[...] = kap * a * e_inv
        o_c[...] = k2 * e_inv
        o_d[...] = r * e_cum
        o_v[...] = v
        for j in range(tm // chunk):
            o_g[j] = e_cum[j * chunk + chunk - 1:j * chunk + chunk, :]


def _prep(p_sh, prev, lp, *, seq_len, tm):
    m = p_sh.shape[0]
    single = seq_len == 1
    chunk = 1 if single else WKV_CHUNK
    tiles_per_seq = 1 if single else seq_len // tm
    vec = lambda a, n: a.reshape(1, n)
    bd = (jnp.arange(256)[:, None] // HEAD == jnp.arange(256)[None, :] // HEAD).astype(BF16)
    ins = [p_sh, prev, vec(lp["mu_shift"], R_SHIFT), vec(lp["w0"], D_MODEL), vec(lp["a0"], D_MODEL),
           vec(lp["k_k"], D_MODEL), vec(lp["k_a"], D_MODEL), lp["w2p"], lp["a2p"], bd]
    full = lambda shape: pl.BlockSpec(shape, lambda i: tuple(0 for _ in shape))
    if single:
        prev_spec = pl.BlockSpec((tm, R_SHIFT), lambda i: (i, 0))
    else:
        prev_spec = pl.BlockSpec((1, 1, R_SHIFT), lambda i: (i // tiles_per_seq, 0, 0))
    in_specs = [pl.BlockSpec((tm, R_SHIFT), lambda i: (i, 0)), prev_spec, full((1, R_SHIFT)),
                full((1, D_MODEL)), full((1, D_MODEL)), full((1, D_MODEL)), full((1, D_MODEL)),
                full((2 * LORA, D_MODEL)), full((2 * LORA, D_MODEL)), full((256, 256))]
    scratch = []
    if not single:
        t = jnp.arange(tm)
        tri = ((t[:, None] // chunk == t[None, :] // chunk) & (t[None, :] <= t[:, None])).astype(BF16)
        ins.append(tri)
        in_specs.append(full((tm, tm)))
        scratch.append(pltpu.VMEM((8, R_SHIFT), F32))
    tok = pl.BlockSpec((tm, D_MODEL), lambda i: (i, 0))
    tok_shape = jax.ShapeDtypeStruct((m, D_MODEL), F32)
    if single:
        g_spec, g_shape = tok, tok_shape
    else:
        g_spec = pl.BlockSpec((tm // chunk, 1, D_MODEL), lambda i: (i, 0, 0))
        g_shape = jax.ShapeDtypeStruct((m // chunk, 1, D_MODEL), F32)
    return pl.pallas_call(
        functools.partial(_prep_kernel, tm=tm, chunk=chunk, tiles_per_seq=tiles_per_seq, single=single),
        grid=(m // tm,),
        in_specs=in_specs,
        out_specs=[tok] * 5 + [g_spec],
        out_shape=[tok_shape] * 5 + [g_shape],
        scratch_shapes=scratch,
        compiler_params=_cparams(("arbitrary",)),
        name="rwkv_prep",
    )(*ins)


def _stack(x):
    first = _iota(x.shape, 1) < HEAD
    return jnp.concatenate([jnp.where(first, x, 0.0), jnp.where(first, 0.0, x)], axis=0)


def _wkv_post(o_s, rs, ks, vs, rk_row, lw_row, lb_row, z, rows):
    own = (_iota(o_s.shape, 0) >> (rows.bit_length() - 1)) == (_iota(o_s.shape, 1) >> (HEAD.bit_length() - 1))
    mean = jnp.sum(o_s, axis=-1, keepdims=True) * (1.0 / HEAD)
    dev = jnp.where(own, o_s - mean, 0.0)
    var = jnp.sum(dev * dev, axis=-1, keepdims=True) * (1.0 / HEAD)
    normed = dev * lax.rsqrt(var + LNX_EPS)
    bonus = jnp.sum(rs * ks * rk_row, axis=-1, keepdims=True) * vs
    fold = lambda t: t[0:rows] + t[rows:2 * rows]
    y = fold(normed) * lw_row + lb_row + fold(bonus)
    return y * (z * _sigmoid(z))


def _each(f, *lists):
    return [f(*a) for a in zip(*lists)]


def _wkv_pairs_chunk(kt, bh, kh, rt, v, z, g_row, rk_row, lw_row, lb_row, s0, chunk):
    n2 = 2 * chunk
    bf = lambda t: t.astype(BF16)
    xs, bs, ks, rs, vs = (_each(_stack, t) for t in (kt, bh, kh, rt, v))
    xb, bb, kb, rb, vb = (_each(bf, t) for t in (xs, bs, ks, rs, vs))

    row = _iota((n2, n2), 0)
    col = _iota((n2, n2), 1)
    same = (row >> (chunk.bit_length() - 1)) == (col >> (chunk.bit_length() - 1))
    rt_ = row & (chunk - 1)
    ct_ = col & (chunk - 1)
    strict = same & (ct_ < rt_)
    incl = same & (ct_ <= rt_)
    lower = lambda a: jnp.where(strict, a, 0.0)
    lower_d = lambda a: jnp.where(incl, a, 0.0)

    a_kb = _each(lambda x, y: lower(_dot_nt(x, y)), xb, bb)
    a_kk = _each(lambda x, y: bf(lower(_dot_nt(x, y))), xb, kb)
    a_rk = _each(lambda x, y: bf(lower_d(_dot_nt(x, y))), rb, kb)
    a_rb = _each(lambda x, y: bf(lower_d(_dot_nt(x, y))), rb, bb)

    eye = (row == col).astype(F32)
    pair2 = (rt_ | 1) == (ct_ | 1)
    t_inv = _each(lambda a: eye - jnp.where(pair2, a, 0.0), a_kb)
    s = 2
    while s < chunk:
        off = same & ((rt_ & ~(2 * s - 1)) == (ct_ & ~(2 * s - 1))) & ((rt_ & s) != 0) & ((ct_ & s) == 0)
        b_off = _each(lambda a: bf(jnp.where(off, a, 0.0)), a_kb)
        t_b = _each(bf, t_inv)
        bt = _each(lambda b, t: bf(_dot(b, t)), b_off, t_b)
        t_inv = _each(lambda t, tb, x: t - _dot(tb, x), t_inv, t_b, bt)
        s *= 2
    t_b = _each(bf, t_inv)

    x1 = _each(_dot, a_kk, vb)
    wu_b = _each(lambda t, x, y: bf(_dot(t, bf(jnp.concatenate([x, y], axis=1)))), t_b, xs, x1)
    av = _each(_dot, a_rk, vb)
    pq = _each(lambda r, a, arb, wu: jnp.concatenate([r, a], axis=1) - _dot(arb, wu), rs, av, a_rb, wu_b)
    m_low = _each(lambda wu, b, g: bf(_dot_tn(wu[:, :PAIR], b) * g), wu_b, bb, g_row)
    vk = _each(_dot_tn, vb, kb)
    n_t = _each(lambda n, wu, b, g: (n - _dot_tn(wu[:, PAIR:], b)) * g, vk, wu_b, bb, g_row)

    s0_b = _each(bf, s0)
    o_s = _each(lambda p, sb: _dot_nt(bf(p[:, :PAIR]), sb) + p[:, PAIR:], pq, s0_b)
    s1 = _each(lambda s_, sb, m, n, g: s_ * g - _dot(sb, m) + n, s0, s0_b, m_low, n_t, g_row)
    y = _each(lambda *a: _wkv_post(*a, chunk), o_s, rs, ks, vs, rk_row, lw_row, lb_row, z)
    return y, s1


def _wkv_chunk_kernel(*refs, chunk, npair, has_state):
    (kt_ref, bh_ref, kh_ref, rt_ref, v_ref, z_ref, gc_ref, rk_ref, lw_ref, lb_ref) = refs[:10]
    refs = refs[10:]
    if has_state:
        s0_ref, y_ref, sout_ref, s_scr = refs
    else:
        y_ref, sout_ref, s_scr = refs
    c = pl.program_id(2)

    @pl.when(c == 0)
    def _():
        s_scr[...] = s0_ref[0] if has_state else jnp.zeros(s_scr.shape, F32)

    lanes = [slice(g * PAIR, (g + 1) * PAIR) for g in range(npair)]
    tiles = lambda ref: [ref[:, ln] for ln in lanes]
    y, s1 = _wkv_pairs_chunk(tiles(kt_ref), tiles(bh_ref), tiles(kh_ref), tiles(rt_ref), tiles(v_ref),
                             tiles(z_ref), [gc_ref[0, :, ln] for ln in lanes], tiles(rk_ref), tiles(lw_ref),
                             tiles(lb_ref), [s_scr[g] for g in range(npair)], chunk)
    for g, ln in enumerate(lanes):
        y_ref[:, ln] = y[g]
        s_scr[g] = s1[g]

    @pl.when(c == pl.num_programs(2) - 1)
    def _():
        sout_ref[0] = s_scr[...]


def _wkv_chunk(kt, bh, kh, rt, v, z, gc, lp, s0_bd, *, batch, seq_len):
    chunk = WKV_CHUNK
    npair = WKV_PAIRS_PER_STEP
    width = npair * PAIR
    nc = seq_len // chunk
    m = batch * seq_len
    has_state = s0_bd is not None
    tok = pl.BlockSpec((chunk, width), lambda b, g, c: (b * nc + c, g))
    par = pl.BlockSpec((1, width), lambda b, g, c: (0, g))
    st = pl.BlockSpec((1, npair, PAIR, PAIR), lambda b, g, c: (b, g, 0, 0))
    ins = [kt, bh, kh, rt, v, z, gc, lp["r_k"].reshape(1, D_MODEL), lp["lnx_w"].reshape(1, D_MODEL),
           lp["lnx_b"].reshape(1, D_MODEL)]
    in_specs = [tok] * 6 + [pl.BlockSpec((1, 1, width), lambda b, g, c: (b * nc + c, 0, g)), par, par, par]
    if has_state:
        ins.append(s0_bd)
        in_specs.append(st)
    return pl.pallas_call(
        functools.partial(_wkv_chunk_kernel, chunk=chunk, npair=npair, has_state=has_state),
        grid=(batch, N_PAIRS // npair, nc),
        in_specs=in_specs,
        out_specs=[tok, st],
        out_shape=[jax.ShapeDtypeStruct((m, D_MODEL), F32),
                   jax.ShapeDtypeStruct((batch, N_PAIRS, PAIR, PAIR), F32)],
        scratch_shapes=[pltpu.VMEM((npair, PAIR, PAIR), F32)],
        compiler_params=_cparams(("arbitrary", "arbitrary", "arbitrary")),
        name="rwkv_chunk",
    )(*ins)


def _wkv_step_kernel(kap_ref, b_ref, k_ref, r_ref, v_ref, w_ref, z_ref, rk_ref, lw_ref, lb_ref, s0_ref,
                     y_ref, sout_ref, *, nseq):
    rows = HEAD
    y_tile = jnp.zeros((nseq, PAIR), F32)
    rowid = _iota((nseq, PAIR), 0)
    inv = 1.0 / rows
    for j in range(nseq):
        rep = lambda ref: _stack(jnp.broadcast_to(ref[j:j + 1, :], (rows, PAIR)))
        xs, bs, ks, rs, vs = rep(kap_ref), rep(b_ref), rep(k_ref), rep(r_ref), rep(v_ref)
        w_row = w_ref[j:j + 1, :]
        s0 = s0_ref[j, 0]
        m_low = _dot_tn(xs.astype(BF16), bs.astype(BF16)) * inv
        n_t = _dot_tn(vs.astype(BF16), ks.astype(BF16)) * inv
        s1 = s0 * w_row - _dot(s0.astype(BF16), m_low.astype(BF16)) + n_t
        sout_ref[j, 0] = s1
        o_s = _dot_nt(rs.astype(BF16), s1.astype(BF16))
        z_rep = jnp.broadcast_to(z_ref[j:j + 1, :], (rows, PAIR))
        y = _wkv_post(o_s, rs, ks, vs, rk_ref[...], lw_ref[...], lb_ref[...], z_rep, rows)
        y_tile = jnp.where(rowid == j, y[0:nseq, :], y_tile)
    y_ref[...] = y_tile


def _wkv_step(kap, b, k2, r, v, w, z, lp, s0_bd):
    m = kap.shape[0]
    nseq = 8
    tok = pl.BlockSpec((nseq, PAIR), lambda i, g: (i, g))
    par = pl.BlockSpec((1, PAIR), lambda i, g: (0, g))
    st = pl.BlockSpec((nseq, 1, PAIR, PAIR), lambda i, g: (i, g, 0, 0))
    return pl.pallas_call(
        functools.partial(_wkv_step_kernel, nseq=nseq),
        grid=(m // nseq, N_PAIRS),
        in_specs=[tok] * 7 + [par, par, par, st],
        out_specs=[tok, st],
        out_shape=[jax.ShapeDtypeStruct((m, D_MODEL), F32),
                   jax.ShapeDtypeStruct((m, N_PAIRS, PAIR, PAIR), F32)],
        compiler_params=_cparams(("arbitrary", "arbitrary")),
        name="rwkv_step",
    )(kap, b, k2, r, v, w, z, lp["r_k"].reshape(1, D_MODEL), lp["lnx_w"].reshape(1, D_MODEL),
      lp["lnx_b"].reshape(1, D_MODEL), s0_bd)


def _to_block_diag(s):
    b = s.shape[0]
    sp = s.reshape(b, N_PAIRS, 2, HEAD, HEAD)
    eye = jnp.eye(2, dtype=s.dtype)
    return jnp.einsum("bpivk,ij->bpivjk", sp, eye).reshape(b, N_PAIRS, PAIR, PAIR)


def _from_block_diag(s):
    b = s.shape[0]
    sp = s.reshape(b, N_PAIRS, 2, HEAD, 2, HEAD)
    return jnp.stack([sp[:, :, 0, :, 0, :], sp[:, :, 1, :, 1, :]], axis=2).reshape(b, N_HEADS, HEAD, HEAD)


def _gate_norm(y, xs, zm, dsk_row, gn_row):
    y = (y + dsk_row * xs) * (zm * _sigmoid(zm))
    half = D_MODEL // N_GROUPS
    outs = []
    for g in range(N_GROUPS):
        yg = y[:, g * half:(g + 1) * half]
        outs.append(yg * lax.rsqrt(jnp.mean(yg * yg, axis=-1, keepdims=True) + NORM_EPS))
    return jnp.concatenate(outs, axis=1) * gn_row


def _ssd_chunk_kernel(*refs, q, has_state):
    (xbc_ref, zm_ref, dt_ref, cw_ref, cb_ref, dtb_ref, alog_ref, dsk_ref, gn_ref, e_ref, et_ref) = refs[:11]
    refs = refs[11:]
    if has_state:
        c0_ref, h0_ref, y_ref, cout_ref, hout_ref, ext, h_scr = refs
    else:
        y_ref, cout_ref, hout_ref, ext, h_scr = refs
    c = pl.program_id(1)

    @pl.when(c == 0)
    def _():
        if has_state:
            ext[0:8, :] = c0_ref[0]
            h_scr[...] = h0_ref[0]
        else:
            ext[0:8, :] = jnp.zeros((8, CONV_DIM), F32)
            h_scr[...] = jnp.zeros(h_scr.shape, F32)

    ext[8:8 + q, :] = xbc_ref[...]
    u = cb_ref[...] + cw_ref[3:4, :] * ext[8:8 + q, :]
    for i in range(CONV_W - 1):
        u = u + cw_ref[i:i + 1, :] * ext[5 + i:5 + i + q, :]
    tail = ext[q:q + 8, :]
    ext[0:8, :] = tail
    u = u * _sigmoid(u)
    xs = u[:, :D_MODEL]
    bm = [u[:, D_MODEL + g * D_STATE:D_MODEL + (g + 1) * D_STATE].astype(BF16) for g in range(N_GROUPS)]
    cm = [u[:, D_MODEL + (N_GROUPS + g) * D_STATE:D_MODEL + (N_GROUPS + g + 1) * D_STATE].astype(BF16)
          for g in range(N_GROUPS)]

    dt = _softplus(dt_ref[...] + dtb_ref[...])
    d_a = dt * (-jnp.exp(alog_ref[...]))
    row = _iota((q, q), 0)
    col = _iota((q, q), 1)
    causal = col <= row
    tril = causal.astype(BF16)
    cs = sum(_dot(tril, piece) for piece in _pieces(d_a, 3))
    eye = (row == col).astype(BF16)
    cs_t = sum(_dot_tn(piece, eye) for piece in _pieces(cs, 3))
    last = cs[q - 1:q, :]
    expand = lambda t: sum(_dot(piece, e_ref[...]) for piece in _pieces(t, 2))
    dt_x = expand(dt)
    dte_x = expand(jnp.exp(last - cs))
    ecs_x = expand(jnp.exp(cs))
    xdt = xs * dt_x
    cd = jnp.exp(cs_t[:, q - 1:q])
    cd_b = jnp.broadcast_to(cd, (LANES, D_STATE))
    decay_full = sum(_dot(et_ref[...], piece) for piece in _pieces(cd_b, 2))

    ys = []
    for g in range(N_GROUPS):
        cb_g = _dot_nt(cm[g], bm[g])
        for j in range(N_PAIRS // N_GROUPS):
            pi = g * (N_PAIRS // N_GROUPS) + j
            lanes = slice(pi * PAIR, (pi + 1) * PAIR)
            gs = []
            for hh in (2 * pi, 2 * pi + 1):
                seg = cs[:, hh:hh + 1] - cs_t[hh:hh + 1, :]
                gs.append(cb_g * jnp.exp(jnp.where(causal, seg, -1e30)))
            g_pair = jnp.concatenate(gs, axis=1).astype(BF16)
            xdt_p = xdt[:, lanes]
            y_diag = _dot(g_pair, _stack(xdt_p).astype(BF16))
            h_p = h_scr[pi * PAIR:(pi + 1) * PAIR, :]
            y_off = _dot_nt(cm[g], h_p.astype(BF16)) * ecs_x[:, lanes]
            ys.append(y_diag + y_off)
            states = _dot_tn((xdt_p * dte_x[:, lanes]).astype(BF16), bm[g])
            h_scr[pi * PAIR:(pi + 1) * PAIR, :] = decay_full[pi * PAIR:(pi + 1) * PAIR, :] * h_p + states
    y = jnp.concatenate(ys, axis=1)
    y_ref[...] = _gate_norm(y, xs, zm_ref[...], dsk_ref[...], gn_ref[...])

    @pl.when(c == pl.num_programs(1) - 1)
    def _():
        cout_ref[0] = ext[8 + q - (CONV_W - 1):8 + q, :]
        hout_ref[0] = h_scr[...]


def _ssd_consts(lp):
    heads = jnp.arange(LANES)[:, None]
    cols = jnp.arange(D_MODEL)[None, :] // HEAD
    e_mat = (heads == cols).astype(BF16)
    pad = lambda a: jnp.pad(a.reshape(1, N_HEADS), ((0, 0), (0, DT_PAD - N_HEADS)))
    return dict(cw=lp["conv_w"], cb=lp["conv_b"].reshape(1, CONV_DIM), dtb=pad(lp["dt_bias"]),
                alog=pad(lp["a_log"]), dsk=jnp.repeat(lp["d_skip"], HEAD).reshape(1, D_MODEL),
                gn=lp["gnorm_w"].reshape(1, D_MODEL), e=e_mat, et=e_mat.T)


def _ssd_chunk(xbc, zm, dt, lp, c0_pad, h0, *, batch, seq_len):
    q = SSD_CHUNK
    nq = seq_len // q
    m = batch * seq_len
    has_state = h0 is not None
    k = _ssd_consts(lp)
    full = lambda shape: pl.BlockSpec(shape, lambda b, c: tuple(0 for _ in shape))
    rows = lambda w: pl.BlockSpec((q, w), lambda b, c: (b * nq + c, 0))
    ins = [xbc, zm, dt, k["cw"], k["cb"], k["dtb"], k["alog"], k["dsk"], k["gn"], k["e"], k["et"]]
    in_specs = [rows(CONV_DIM), rows(D_MODEL), rows(DT_PAD), full((CONV_W, CONV_DIM)), full((1, CONV_DIM)),
                full((1, DT_PAD)), full((1, DT_PAD)), full((1, D_MODEL)), full((1, D_MODEL)),
                full((LANES, D_MODEL)), full((D_MODEL, LANES))]
    if has_state:
        ins += [c0_pad, h0]
        in_specs += [pl.BlockSpec((1, 8, CONV_DIM), lambda b, c: (b, 0, 0)),
                     pl.BlockSpec((1, D_MODEL, D_STATE), lambda b, c: (b, 0, 0))]
    return pl.pallas_call(
        functools.partial(_ssd_chunk_kernel, q=q, has_state=has_state),
        grid=(batch, nq),
        in_specs=in_specs,
        out_specs=[rows(D_MODEL), pl.BlockSpec((1, CONV_W - 1, CONV_DIM), lambda b, c: (b, 0, 0)),
                   pl.BlockSpec((1, D_MODEL, D_STATE), lambda b, c: (b, 0, 0))],
        out_shape=[jax.ShapeDtypeStruct((m, D_MODEL), F32),
                   jax.ShapeDtypeStruct((batch, CONV_W - 1, CONV_DIM), F32),
                   jax.ShapeDtypeStruct((batch, D_MODEL, D_STATE), F32)],
        scratch_shapes=[pltpu.VMEM((q + 8, CONV_DIM), F32), pltpu.VMEM((D_MODEL, D_STATE), F32)],
        compiler_params=_cparams(("arbitrary", "arbitrary")),
        name="ssd_chunk",
    )(*ins)


def _ssd_step_kernel(xbc_ref, c0_ref, c1_ref, c2_ref, zm_ref, dt_ref, cw_ref, cb_ref, dtb_ref, alog_ref,
                     dsk_ref, gn_ref, e_ref, h0_ref, y_ref, hout_ref, *, nseq):
    u = (cb_ref[...] + cw_ref[3:4, :] * xbc_ref[...] + cw_ref[2:3, :] * c2_ref[...]
         + cw_ref[1:2, :] * c1_ref[...] + cw_ref[0:1, :] * c0_ref[...])
    u = u * _sigmoid(u)
    xs = u[:, :D_MODEL]
    dt = _softplus(dt_ref[...] + dtb_ref[...])
    ed = jnp.exp(dt * (-jnp.exp(alog_ref[...])))
    expand = lambda t: sum(_dot(piece, e_ref[...]) for piece in _pieces(t, 2))
    xdt = xs * expand(dt)
    ed_x = expand(ed)
    half = D_MODEL // N_GROUPS
    ones = jnp.full((LANES, D_STATE), 1.0 / LANES, BF16)
    rowid = _iota((nseq, half), 0)
    y_groups = [jnp.zeros((nseq, half), F32) for _ in range(N_GROUPS)]
    for j in range(nseq):
        for g in range(N_GROUPS):
            lanes = slice(g * half, (g + 1) * half)
            b_g = u[j:j + 1, D_MODEL + g * D_STATE:D_MODEL + (g + 1) * D_STATE]
            c_g = u[j:j + 1, D_MODEL + (N_GROUPS + g) * D_STATE:D_MODEL + (N_GROUPS + g + 1) * D_STATE]
            x_rep = jnp.broadcast_to(xdt[j:j + 1, lanes], (LANES, half)).astype(BF16)
            b_rep = jnp.broadcast_to(b_g, (LANES, D_STATE)).astype(BF16)
            outer = _dot_tn(x_rep, b_rep) * (1.0 / LANES)
            d_rep = jnp.broadcast_to(ed_x[j:j + 1, lanes], (LANES, half))
            dec = sum(_dot_tn(piece, ones) for piece in _pieces(d_rep, 2))
            h1 = dec * h0_ref[j, lanes, :] + outer
            hout_ref[j, lanes, :] = h1
            c_rep = jnp.broadcast_to(c_g, (8, D_STATE)).astype(BF16)
            y_g = _dot_nt(c_rep, h1.astype(BF16))
            y_groups[g] = jnp.where(rowid == j, y_g[0:nseq, :], y_groups[g])
    y = jnp.concatenate(y_groups, axis=1)
    y_ref[...] = _gate_norm(y, xs, zm_ref[...], dsk_ref[...], gn_ref[...])


def _ssd_step(xbc, conv0, zm, dt, lp, h0):
    m = xbc.shape[0]
    nseq = 8
    k = _ssd_consts(lp)
    full = lambda shape: pl.BlockSpec(shape, lambda i: tuple(0 for _ in shape))
    rows = lambda w: pl.BlockSpec((nseq, w), lambda i: (i, 0))
    st = pl.BlockSpec((nseq, D_MODEL, D_STATE), lambda i: (i, 0, 0))
    y, h1 = pl.pallas_call(
        functools.partial(_ssd_step_kernel, nseq=nseq),
        grid=(m // nseq,),
        in_specs=[rows(CONV_DIM)] * 4 + [rows(D_MODEL), rows(DT_PAD), full((CONV_W, CONV_DIM)),
                                          full((1, CONV_DIM)), full((1, DT_PAD)), full((1, DT_PAD)),
                                          full((1, D_MODEL)), full((1, D_MODEL)), full((LANES, D_MODEL)), st],
        out_specs=[rows(D_MODEL), st],
        out_shape=[jax.ShapeDtypeStruct((m, D_MODEL), F32), jax.ShapeDtypeStruct((m, D_MODEL, D_STATE), F32)],
        compiler_params=_cparams(("arbitrary",)),
        name="ssd_step",
    )(xbc, conv0[:, 0], conv0[:, 1], conv0[:, 2], zm, dt, k["cw"], k["cb"], k["dtb"], k["alog"], k["dsk"],
      k["gn"], k["e"], h0)
    conv_new = jnp.stack([conv0[:, 1], conv0[:, 2], xbc], axis=1)
    return y, conv_new, h1


def _out_kernel(*refs, final):
    if final:
        x_ref, yr_ref, ym_ref, w_ref, fw_ref, o_ref = refs
    else:
        x_ref, yr_ref, ym_ref, w_ref, o_ref = refs
    out = (x_ref[...] + _dot(yr_ref[...].astype(BF16), w_ref[0:D_MODEL, :])
           + _dot(ym_ref[...].astype(BF16), w_ref[D_MODEL:2 * D_MODEL, :]))
    if final:
        out = out * lax.rsqrt(jnp.mean(out * out, axis=-1, keepdims=True) + NORM_EPS) * fw_ref[...]
    o_ref[...] = out


def _out_proj(x, yr, ym, w_bf, final_w, tm):
    m, d = x.shape
    final = final_w is not None
    tok = pl.BlockSpec((tm, d), lambda i: (i, 0))
    ins = [x, yr, ym, w_bf] + ([final_w.reshape(1, d)] if final else [])
    in_specs = [tok, tok, tok, pl.BlockSpec((2 * d, d), lambda i: (0, 0))]
    if final:
        in_specs.append(pl.BlockSpec((1, d), lambda i: (0, 0)))
    return pl.pallas_call(
        functools.partial(_out_kernel, final=final),
        grid=(m // tm,),
        in_specs=in_specs,
        out_specs=tok,
        out_shape=jax.ShapeDtypeStruct((m, d), F32),
        compiler_params=_cparams(("arbitrary",)),
        name="out_proj",
    )(*ins)


ALL_SEGS = (SEG_SH, SEG_ZR, SEG_ZM, SEG_XBC, SEG_DT)


def _layer_prompt(x, lp, final_w, *, batch, seq_len):
    tm = 256
    p_sh, z_r, z_m, xbc, dt = _proj(x, lp["norm_w"], lp["w_in"], ALL_SEGS, tm)
    xn_last = _rmsnorm(x.reshape(batch, seq_len, D_MODEL)[:, -1], lp["norm_w"])
    prev = jnp.zeros((batch, 1, R_SHIFT), F32)
    kt, bh, kh, rt, v, gc = _prep(p_sh, prev, lp, seq_len=seq_len, tm=tm)
    y_r, s_bd = _wkv_chunk(kt, bh, kh, rt, v, z_r, gc, lp, None, batch=batch, seq_len=seq_len)
    y_m, conv_new, ssm_new = _ssd_chunk(xbc, z_m, dt, lp, None, None, batch=batch, seq_len=seq_len)
    x_new = _out_proj(x, y_r, y_m, lp["w_out"], final_w, tm)
    return (x_new, xn_last, _from_block_diag(s_bd), conv_new,
            ssm_new.reshape(batch, N_HEADS, HEAD, D_STATE))


def _layer_sample(x, shift_prev, wkv0, conv0, ssm0, lp, final_w):
    m = x.shape[0]
    tm = m
    p_sh, z_r, z_m, xbc, dt, xn = _proj(x, lp["norm_w"], lp["w_in"], ALL_SEGS, tm, emit_xn=True)
    (prev,) = _proj(shift_prev, None, lp["w_in"], (SEG_SH,), tm)
    kap, b, k2, r, v, w = _prep(p_sh, prev, lp, seq_len=1, tm=tm)
    y_r, s_bd = _wkv_step(kap, b, k2, r, v, w, z_r, lp, _to_block_diag(wkv0))
    y_m, conv_new, ssm_new = _ssd_step(xbc, conv0, z_m, dt, lp, ssm0.reshape(m, D_MODEL, D_STATE))
    x_new = _out_proj(x, y_r, y_m, lp["w_out"], final_w, tm)
    return (x_new, xn, _from_block_diag(s_bd), conv_new, ssm_new.reshape(m, N_HEADS, HEAD, D_STATE))


def _layer_params(params, l):
    lp = {k: v[l] for k, v in params.items()}
    lp["w_in"] = jnp.pad(lp["w_in"], ((0, 0), (0, D_IN_PAD - lp["w_in"].shape[1]))).astype(BF16)
    lp["w_out"] = lp["w_out"].astype(BF16)
    zeros = jnp.zeros((LORA, D_MODEL), F32)
    lp["w2p"] = jnp.concatenate([lp["w_lora2"], zeros], axis=0).astype(BF16)
    lp["a2p"] = jnp.concatenate([zeros, lp["a_lora2"]], axis=0).astype(BF16)
    return lp


def kernel(x_prompt, x_sample, state_shift, state_wkv, state_conv, state_ssm, norm_w, w_in, mu_shift, w0,
           w_lora2, a0, a_lora2, k_k, k_a, r_k, lnx_w, lnx_b, conv_w, conv_b, dt_bias, a_log, d_skip,
           gnorm_w, w_out, final_norm_w):
    params = dict(norm_w=norm_w, w_in=w_in, mu_shift=mu_shift, w0=w0, w_lora2=w_lora2, a0=a0,
                  a_lora2=a_lora2, k_k=k_k, k_a=k_a, r_k=r_k, lnx_w=lnx_w, lnx_b=lnx_b, conv_w=conv_w,
                  conv_b=conv_b, dt_bias=dt_bias, a_log=a_log, d_skip=d_skip, gnorm_w=gnorm_w, w_out=w_out)
    depth = norm_w.shape[0]
    bp, lseq, d = x_prompt.shape
    bs = x_sample.shape[0]
    xp = x_prompt.reshape(bp * lseq, d)
    xs = x_sample.reshape(bs, d)
    p_states, s_states = [], []
    for l in range(depth):
        lp = _layer_params(params, l)
        fw = final_norm_w if l == depth - 1 else None
        xp, *st = _layer_prompt(xp, lp, fw, batch=bp, seq_len=lseq)
        p_states.append(st)
        xs, *st = _layer_sample(xs, state_shift[l], state_wkv[l], state_conv[l], state_ssm[l], lp, fw)
        s_states.append(st)
    stack = lambda states, i: jnp.stack([s[i] for s in states])
    return (xp.reshape(bp, lseq, d), xs.reshape(bs, 1, d),
            stack(p_states, 0), stack(p_states, 1), stack(p_states, 2), stack(p_states, 3),
            stack(s_states, 0), stack(s_states, 1), stack(s_states, 2), stack(s_states, 3))
```

```python
import functools

import jax
import jax.numpy as jnp
from jax import lax
from jax.experimental import pallas as pl
from jax.experimental.pallas import tpu as pltpu

F32 = jnp.float32
BF16 = jnp.bfloat16

D_MODEL = 1024
HEAD = 64
N_HEADS = 16
LORA = 64
R_SHIFT = 3 * D_MODEL + 2 * LORA
D_STATE = 128
N_GROUPS = 2
CONV_W = 4
CONV_DIM = D_MODEL + 2 * N_GROUPS * D_STATE
DT_PAD = 128
SEG_SH = (0, R_SHIFT)
SEG_ZR = (R_SHIFT, R_SHIFT + D_MODEL)
SEG_ZM = (SEG_ZR[1], SEG_ZR[1] + D_MODEL)
SEG_XBC = (SEG_ZM[1], SEG_ZM[1] + CONV_DIM)
SEG_DT = (SEG_XBC[1], SEG_XBC[1] + DT_PAD)
D_IN_PAD = SEG_DT[1]
NORM_EPS = 1e-5
LNX_EPS = 64e-5
LANES = 128
PAIR = 2 * HEAD
N_PAIRS = N_HEADS // 2
WKV_CHUNK = 64
SSD_CHUNK = 128
WKV_PAIRS_PER_STEP = 8
VMEM_LIMIT = 56 * 1024 * 1024


def _cparams(sem):
    return pltpu.CompilerParams(dimension_semantics=sem, vmem_limit_bytes=VMEM_LIMIT)


def _dot(a, b):
    return jnp.dot(a, b, preferred_element_type=F32)


def _dot_nt(a, b):
    return lax.dot_general(a, b, (((1,), (1,)), ((), ())), preferred_element_type=F32)


def _dot_tn(a, b):
    return lax.dot_general(a, b, (((0,), (0,)), ((), ())), preferred_element_type=F32)


def _pieces(x, n):
    out, r = [], x
    for _ in range(n):
        p = r.astype(BF16)
        out.append(p)
        r = r - p.astype(F32)
    return out


def _sigmoid(x):
    return 1.0 / (1.0 + jnp.exp(-x))


def _softplus(x):
    return jnp.maximum(x, 0.0) + jnp.log(1.0 + jnp.exp(-jnp.abs(x)))


def _iota(shape, dim):
    return lax.broadcasted_iota(jnp.int32, shape, dim)


def _proj_kernel(*refs, segs, norm, emit_xn):
    refs = list(refs)
    x_ref = refs.pop(0)
    nw_ref = refs.pop(0) if norm else None
    w_ref = refs.pop(0)
    x = x_ref[...]
    if norm:
        xn = x * lax.rsqrt(jnp.mean(x * x, axis=-1, keepdims=True) + NORM_EPS) * nw_ref[...]
    else:
        xn = x
    xb = xn.astype(BF16)
    for (lo, hi), o_ref in zip(segs, refs):
        o_ref[...] = _dot(xb, w_ref[:, lo:hi])
    if emit_xn:
        refs[len(segs)][...] = xn


def _proj(x, norm_w, w_bf, segs, tm, emit_xn=False):
    m, d = x.shape
    n = w_bf.shape[1]
    norm = norm_w is not None
    ins = [x] + ([norm_w.reshape(1, d)] if norm else []) + [w_bf]
    in_specs = [pl.BlockSpec((tm, d), lambda i: (i, 0))]
    if norm:
        in_specs.append(pl.BlockSpec((1, d), lambda i: (0, 0)))
    in_specs.append(pl.BlockSpec((d, n), lambda i: (0, 0)))
    widths = [hi - lo for lo, hi in segs] + ([d] if emit_xn else [])
    return pl.pallas_call(
        functools.partial(_proj_kernel, segs=tuple(segs), norm=norm, emit_xn=emit_xn),
        grid=(m // tm,),
        in_specs=in_specs,
        out_specs=[pl.BlockSpec((tm, w), lambda i: (i, 0)) for w in widths],
        out_shape=[jax.ShapeDtypeStruct((m, w), F32) for w in widths],
        compiler_params=_cparams(("arbitrary",)),
        name="norm_proj",
    )(*ins)


def _rmsnorm_kernel(x_ref, w_ref, o_ref):
    x = x_ref[...]
    o_ref[...] = x * lax.rsqrt(jnp.mean(x * x, axis=-1, keepdims=True) + NORM_EPS) * w_ref[...]


def _rmsnorm(x, w):
    m, d = x.shape
    return pl.pallas_call(
        _rmsnorm_kernel,
        out_shape=jax.ShapeDtypeStruct((m, d), F32),
        name="rmsnorm_rows",
    )(x, w.reshape(1, d))


def _prep_kernel(*refs, tm, chunk, tiles_per_seq, single):
    (p_ref, prev_ref, mu_ref, w0_ref, a0_ref, kk_ref, ka_ref, w2_ref, a2_ref, bd_ref) = refs[:10]
    refs = list(refs[10:])
    tri_ref = None if single else refs.pop(0)
    o_a, o_b, o_c, o_d, o_v, o_g = refs[:6]
    carry_ref = None if single else refs[6]

    p = p_ref[...]
    if single:
        shifted = prev_ref[...]
    else:
        i = pl.program_id(0)

        @pl.when(i % tiles_per_seq == 0)
        def _():
            carry_ref[...] = jnp.broadcast_to(prev_ref[0], carry_ref.shape)

        rolled = pltpu.roll(p, 1, 0)
        shifted = jnp.where(_iota(p.shape, 0) == 0, carry_ref[0:1, :], rolled)
        carry_ref[...] = jnp.broadcast_to(p[tm - 1:tm, :], carry_ref.shape)
    xm = p + (shifted - p) * mu_ref[...]
    r = xm[:, 0:D_MODEL]
    k = xm[:, D_MODEL:2 * D_MODEL]
    v = xm[:, 2 * D_MODEL:3 * D_MODEL]
    wal = xm[:, 3 * D_MODEL:R_SHIFT]
    dw = _dot(jnp.tanh(wal).astype(BF16), w2_ref[...])
    da = _dot(wal.astype(BF16), a2_ref[...])
    w_pre = -_softplus(-(w0_ref[...] + dw)) - 0.5
    logw = -jnp.exp(w_pre)
    a = _sigmoid(a0_ref[...] + da)

    def head_sum(x):
        hi, lo = _pieces(x, 2)
        cols = []
        for j in range(D_MODEL // 256):
            sl = slice(j * 256, (j + 1) * 256)
            cols.append(_dot(hi[:, sl], bd_ref[...]) + _dot(lo[:, sl], bd_ref[...]))
        return jnp.concatenate(cols, axis=1)

    kk = k * kk_ref[...]
    kap = kk / jnp.maximum(jnp.sqrt(head_sum(kk * kk)), 1e-12)
    k2 = k * (1.0 + (a - 1.0) * ka_ref[...])
    if single:
        o_a[...] = kap
        o_b[...] = kap * a
        o_c[...] = k2
        o_d[...] = r
        o_v[...] = v
        o_g[...] = jnp.exp(logw)
    else:
        cum = sum(_dot(tri_ref[...], piece) for piece in _pieces(logw, 3))
        e_inv = jnp.exp(-cum)
        e_cum = jnp.exp(cum)
        o_a[...] = kap * jnp.exp(cum - logw)
        o_b[...] = kap * a * e_inv
        o_c[...] = k2 * e_inv
        o_d[...] = r * e_cum
        o_v[...] = v
        for j in range(tm // chunk):
            o_g[j] = e_cum[j * chunk + chunk - 1:j * chunk + chunk, :]


def _prep(p_sh, prev, lp, *, seq_len, tm):
    m = p_sh.shape[0]
    single = seq_len == 1
    chunk = 1 if single else WKV_CHUNK
    tiles_per_seq = 1 if single else seq_len // tm
    vec = lambda a, n: a.reshape(1, n)
    bd = (jnp.arange(256)[:, None] // HEAD == jnp.arange(256)[None, :] // HEAD).astype(BF16)
    ins = [p_sh, prev, vec(lp["mu_shift"], R_SHIFT), vec(lp["w0"], D_MODEL), vec(lp["a0"], D_MODEL),
           vec(lp["k_k"], D_MODEL), vec(lp["k_a"], D_MODEL), lp["w2p"], lp["a2p"], bd]
    full = lambda shape: pl.BlockSpec(shape, lambda i: tuple(0 for _ in shape))
    if single:
        prev_spec = pl.BlockSpec((tm, R_SHIFT), lambda i: (i, 0))
    else:
        prev_spec = pl.BlockSpec((1, 1, R_SHIFT), lambda i: (i // tiles_per_seq, 0, 0))
    in_specs = [pl.BlockSpec((tm, R_SHIFT), lambda i: (i, 0)), prev_spec, full((1, R_SHIFT)),
                full((1, D_MODEL)), full((1, D_MODEL)), full((1, D_MODEL)), full((1, D_MODEL)),
                full((2 * LORA, D_MODEL)), full((2 * LORA, D_MODEL)), full((256, 256))]
    scratch = []
    if not single:
        t = jnp.arange(tm)
        tri = ((t[:, None] // chunk == t[None, :] // chunk) & (t[None, :] <= t[:, None])).astype(BF16)
        ins.append(tri)
        in_specs.append(full((tm, tm)))
        scratch.append(pltpu.VMEM((8, R_SHIFT), F32))
    tok = pl.BlockSpec((tm, D_MODEL), lambda i: (i, 0))
    tok_shape = jax.ShapeDtypeStruct((m, D_MODEL), F32)
    if single:
        g_spec, g_shape = tok, tok_shape
    else:
        g_spec = pl.BlockSpec((tm // chunk, 1, D_MODEL), lambda i: (i, 0, 0))
        g_shape = jax.ShapeDtypeStruct((m // chunk, 1, D_MODEL), F32)
    return pl.pallas_call(
        functools.partial(_prep_kernel, tm=tm, chunk=chunk, tiles_per_seq=tiles_per_seq, single=single),
        grid=(m // tm,),
        in_specs=in_specs,
        out_specs=[tok] * 5 + [g_spec],
        out_shape=[tok_shape] * 5 + [g_shape],
        scratch_shapes=scratch,
        compiler_params=_cparams(("arbitrary",)),
        name="rwkv_prep",
    )(*ins)


def _stack(x):
    first = _iota(x.shape, 1) < HEAD
    return jnp.concatenate([jnp.where(first, x, 0.0), jnp.where(first, 0.0, x)], axis=0)


def _wkv_post(o_s, rs, ks, vs, rk_row, lw_row, lb_row, z, rows):
    own = (_iota(o_s.shape, 0) >> (rows.bit_length() - 1)) == (_iota(o_s.shape, 1) >> (HEAD.bit_length() - 1))
    mean = jnp.sum(o_s, axis=-1, keepdims=True) * (1.0 / HEAD)
    dev = jnp.where(own, o_s - mean, 0.0)
    var = jnp.sum(dev * dev, axis=-1, keepdims=True) * (1.0 / HEAD)
    normed = dev * lax.rsqrt(var + LNX_EPS)
    bonus = jnp.sum(rs * ks * rk_row, axis=-1, keepdims=True) * vs
    fold = lambda t: t[0:rows] + t[rows:2 * rows]
    y = fold(normed) * lw_row + lb_row + fold(bonus)
    return y * (z * _sigmoid(z))


def _each(f, *lists):
    return [f(*a) for a in zip(*lists)]


def _wkv_pairs_chunk(kt, bh, kh, rt, v, z, g_row, rk_row, lw_row, lb_row, s0, chunk):
    n2 = 2 * chunk
    bf = lambda t: t.astype(BF16)
    xs, bs, ks, rs, vs = (_each(_stack, t) for t in (kt, bh, kh, rt, v))
    xb, bb, kb, rb, vb = (_each(bf, t) for t in (xs, bs, ks, rs, vs))

    row = _iota((n2, n2), 0)
    col = _iota((n2, n2), 1)
    same = (row >> (chunk.bit_length() - 1)) == (col >> (chunk.bit_length() - 1))
    rt_ = row & (chunk - 1)
    ct_ = col & (chunk - 1)
    strict = same & (ct_ < rt_)
    incl = same & (ct_ <= rt_)
    lower = lambda a: jnp.where(strict, a, 0.0)
    lower_d = lambda a: jnp.where(incl, a, 0.0)

    gram = _each(lambda x, r, k, b: _dot_nt(jnp.concatenate([x, r], axis=0), jnp.concatenate([k, b], axis=0)),
                 xb, rb, kb, bb)
    a_kb = _each(lambda g: lower(g[:n2, n2:]), gram)
    a_kr = _each(lambda g: bf(jnp.concatenate([lower(g[:n2, :n2]), lower_d(g[n2:, :n2])], axis=0)), gram)
    a_rb = _each(lambda g: bf(lower_d(g[n2:, n2:])), gram)

    eye = (row == col).astype(F32)
    pair2 = (rt_ | 1) == (ct_ | 1)
    t_inv = _each(lambda a: eye - jnp.where(pair2, a, 0.0), a_kb)
    s = 2
    while s < chunk:
        off = same & ((rt_ & ~(2 * s - 1)) == (ct_ & ~(2 * s - 1))) & ((rt_ & s) != 0) & ((ct_ & s) == 0)
        b_off = _each(lambda a: bf(jnp.where(off, a, 0.0)), a_kb)
        t_b = _each(bf, t_inv)
        bt = _each(lambda b, t: bf(_dot(b, t)), b_off, t_b)
        t_inv = _each(lambda t, tb, x: t - _dot(tb, x), t_inv, t_b, bt)
        s *= 2
    t_b = _each(bf, t_inv)

    xa = _each(_dot, a_kr, vb)
    wu_b = _each(lambda t, x, y: bf(_dot(t, bf(jnp.concatenate([x, y[:n2]], axis=1)))), t_b, xs, xa)
    pq = _each(lambda r, a, arb, wu: jnp.concatenate([r, a[n2:]], axis=1) - _dot(arb, wu), rs, xa, a_rb, wu_b)
    m_low = _each(lambda wu, b, g: bf(_dot_tn(wu[:, :PAIR], b) * g), wu_b, bb, g_row)
    n_t = _each(lambda v_, wu, k, b, g: _dot_tn(jnp.concatenate([v_, -wu[:, PAIR:]], axis=0),
                                                jnp.concatenate([k, b], axis=0)) * g,
                vb, wu_b, kb, bb, g_row)

    s0_b = _each(bf, s0)
    o_s = _each(lambda p, sb: _dot_nt(bf(p[:, :PAIR]), sb) + p[:, PAIR:], pq, s0_b)
    s1 = _each(lambda s_, sb, m, n, g: s_ * g - _dot(sb, m) + n, s0, s0_b, m_low, n_t, g_row)
    y = _each(lambda *a: _wkv_post(*a, chunk), o_s, rs, ks, vs, rk_row, lw_row, lb_row, z)
    return y, s1


def _wkv_chunk_kernel(*refs, chunk, npair, has_state):
    (kt_ref, bh_ref, kh_ref, rt_ref, v_ref, z_ref, gc_ref, rk_ref, lw_ref, lb_ref) = refs[:10]
    refs = refs[10:]
    if has_state:
        s0_ref, y_ref, sout_ref, s_scr = refs
    else:
        y_ref, sout_ref, s_scr = refs
    c = pl.program_id(2)

    @pl.when(c == 0)
    def _():
        s_scr[...] = s0_ref[0] if has_state else jnp.zeros(s_scr.shape, F32)

    lanes = [slice(g * PAIR, (g + 1) * PAIR) for g in range(npair)]
    tiles = lambda ref: [ref[:, ln] for ln in lanes]
    y, s1 = _wkv_pairs_chunk(tiles(kt_ref), tiles(bh_ref), tiles(kh_ref), tiles(rt_ref), tiles(v_ref),
                             tiles(z_ref), [gc_ref[0, :, ln] for ln in lanes], tiles(rk_ref), tiles(lw_ref),
                             tiles(lb_ref), [s_scr[g] for g in range(npair)], chunk)
    for g, ln in enumerate(lanes):
        y_ref[:, ln] = y[g]
        s_scr[g] = s1[g]

    @pl.when(c == pl.num_programs(2) - 1)
    def _():
        sout_ref[0] = s_scr[...]


def _wkv_chunk(kt, bh, kh, rt, v, z, gc, lp, s0_bd, *, batch, seq_len):
    chunk = WKV_CHUNK
    npair = WKV_PAIRS_PER_STEP
    width = npair * PAIR
    nc = seq_len // chunk
    m = batch * seq_len
    has_state = s0_bd is not None
    tok = pl.BlockSpec((chunk, width), lambda b, g, c: (b * nc + c, g))
    par = pl.BlockSpec((1, width), lambda b, g, c: (0, g))
    st = pl.BlockSpec((1, npair, PAIR, PAIR), lambda b, g, c: (b, g, 0, 0))
    ins = [kt, bh, kh, rt, v, z, gc, lp["r_k"].reshape(1, D_MODEL), lp["lnx_w"].reshape(1, D_MODEL),
           lp["lnx_b"].reshape(1, D_MODEL)]
    in_specs = [tok] * 6 + [pl.BlockSpec((1, 1, width), lambda b, g, c: (b * nc + c, 0, g)), par, par, par]
    if has_state:
        ins.append(s0_bd)
        in_specs.append(st)
    return pl.pallas_call(
        functools.partial(_wkv_chunk_kernel, chunk=chunk, npair=npair, has_state=has_state),
        grid=(batch, N_PAIRS // npair, nc),
        in_specs=in_specs,
        out_specs=[tok, st],
        out_shape=[jax.ShapeDtypeStruct((m, D_MODEL), F32),
                   jax.ShapeDtypeStruct((batch, N_PAIRS, PAIR, PAIR), F32)],
        scratch_shapes=[pltpu.VMEM((npair, PAIR, PAIR), F32)],
        compiler_params=_cparams(("arbitrary", "arbitrary", "arbitrary")),
        name="rwkv_chunk",
    )(*ins)


def _head_sum_nat(x):
    first = _iota(x.shape, 1) < HEAD
    s_first = jnp.sum(jnp.where(first, x, 0.0), axis=-1, keepdims=True)
    s_all = jnp.sum(x, axis=-1, keepdims=True)
    return jnp.where(first, s_first, s_all - s_first)


def _cols_of(tile):
    pad = jnp.zeros((LANES - tile.shape[0], LANES), F32)
    return jnp.concatenate([tile, pad], axis=0).T


def _wkv_step_kernel(*refs, nseq, npair, aliased):
    (kap_ref, b_ref, k_ref, r_ref, v_ref, w_ref, z_ref, rk_ref, lw_ref, lb_ref, s0_ref) = refs[:11]
    y_ref, sout_ref = refs[12:14] if aliased else refs[11:13]
    lane = _iota((HEAD, LANES), 1)
    rowid = _iota((LANES, HEAD), 0)
    rep = lambda r_, n: jnp.broadcast_to(r_, (n, HEAD)).astype(BF16)
    items = [(j, hh) for j in range(nseq) for hh in range(2)]
    for q in range(npair):
        pl_ = slice(q * PAIR, (q + 1) * PAIR)
        v_cols_b = _cols_of(v_ref[:, pl_]).astype(BF16)

        def row(ref, j, hh):
            full = ref[j:j + 1, pl_]
            return full[:, :HEAD] if hh == 0 else pltpu.roll(full, HEAD, 1)[:, :HEAD]

        s0s = [s0_ref[j, 2 * q + hh] for j, hh in items]
        s_kk = [_dot_nt(s0.astype(BF16), rep(row(kap_ref, j, hh), HEAD)) for s0, (j, hh) in zip(s0s, items)]
        vk = [_dot(v_cols_b[hh * HEAD:(hh + 1) * HEAD, :],
                   jnp.where(rowid == j, jnp.broadcast_to(row(k_ref, j, hh), (LANES, HEAD)), 0.0).astype(BF16))
              for j, hh in items]
        s1s = [s0 * row(w_ref, j, hh) - sk * row(b_ref, j, hh) + o_
               for s0, sk, o_, (j, hh) in zip(s0s, s_kk, vk, items)]
        for s1, (j, hh) in zip(s1s, items):
            sout_ref[j, 2 * q + hh] = s1
        o_b = [_dot_nt(s1.astype(BF16), rep(row(r_ref, j, hh), LANES)) for s1, (j, hh) in zip(s1s, items)]
        o_t = [jnp.zeros((HEAD, LANES), F32) for _ in range(2)]
        for o_, (j, hh) in zip(o_b, items):
            o_t[hh] = jnp.where(lane == j, o_, o_t[hh])
        o = jnp.concatenate(o_t, axis=0).T[0:nseq, :]
        dev = o - _head_sum_nat(o) * (1.0 / HEAD)
        var = _head_sum_nat(dev * dev) * (1.0 / HEAD)
        bonus = _head_sum_nat(r_ref[:, pl_] * k_ref[:, pl_] * rk_ref[:, pl_]) * v_ref[:, pl_]
        z = z_ref[:, pl_]
        y = dev * lax.rsqrt(var + LNX_EPS) * lw_ref[:, pl_] + lb_ref[:, pl_] + bonus
        y_ref[:, pl_] = y * (z * _sigmoid(z))


def _wkv_step(kap, b, k2, r, v, w, z, lp, state_all, layer, prev_out):
    m = kap.shape[0]
    nseq, npair = 8, 2
    width = npair * PAIR
    tok = pl.BlockSpec((nseq, width), lambda i, g: (i, g))
    par = pl.BlockSpec((1, width), lambda i, g: (0, g))
    st = pl.BlockSpec((None, nseq, 2 * npair, HEAD, HEAD), lambda i, g: (layer, i, g, 0, 0))
    ins = [kap, b, k2, r, v, w, z, lp["r_k"].reshape(1, D_MODEL), lp["lnx_w"].reshape(1, D_MODEL),
           lp["lnx_b"].reshape(1, D_MODEL), state_all]
    in_specs = [tok] * 7 + [par, par, par, st]
    aliases = {}
    if prev_out is not None:
        ins.append(prev_out)
        in_specs.append(pl.BlockSpec(memory_space=pl.ANY))
        aliases = {len(ins) - 1: 1}
    return pl.pallas_call(
        functools.partial(_wkv_step_kernel, nseq=nseq, npair=npair, aliased=prev_out is not None),
        grid=(m // nseq, N_PAIRS // npair),
        in_specs=in_specs,
        out_specs=[tok, st],
        out_shape=[jax.ShapeDtypeStruct((m, D_MODEL), F32), jax.ShapeDtypeStruct(state_all.shape, F32)],
        input_output_aliases=aliases,
        compiler_params=_cparams(("arbitrary", "arbitrary")),
        name="rwkv_step",
    )(*ins)


def _from_block_diag(s):
    b = s.shape[0]
    sp = s.reshape(b, N_PAIRS, 2, HEAD, 2, HEAD)
    return jnp.stack([sp[:, :, 0, :, 0, :], sp[:, :, 1, :, 1, :]], axis=2).reshape(b, N_HEADS, HEAD, HEAD)


def _gate_norm(y, xs, zm, dsk_row, gn_row):
    y = (y + dsk_row * xs) * (zm * _sigmoid(zm))
    half = D_MODEL // N_GROUPS
    outs = []
    for g in range(N_GROUPS):
        yg = y[:, g * half:(g + 1) * half]
        outs.append(yg * lax.rsqrt(jnp.mean(yg * yg, axis=-1, keepdims=True) + NORM_EPS))
    return jnp.concatenate(outs, axis=1) * gn_row


def _ssd_chunk_kernel(*refs, q, has_state):
    (xbc_ref, zm_ref, dt_ref, cw_ref, cb_ref, dtb_ref, alog_ref, dsk_ref, gn_ref, e_ref, et_ref) = refs[:11]
    refs = refs[11:]
    if has_state:
        c0_ref, h0_ref, y_ref, cout_ref, hout_ref, ext, h_scr = refs
    else:
        y_ref, cout_ref, hout_ref, ext, h_scr = refs
    c = pl.program_id(1)

    @pl.when(c == 0)
    def _():
        if has_state:
            ext[0:8, :] = c0_ref[0]
            h_scr[...] = h0_ref[0]
        else:
            ext[0:8, :] = jnp.zeros((8, CONV_DIM), F32)
            h_scr[...] = jnp.zeros(h_scr.shape, F32)

    ext[8:8 + q, :] = xbc_ref[...]
    u = cb_ref[...] + cw_ref[3:4, :] * ext[8:8 + q, :]
    for i in range(CONV_W - 1):
        u = u + cw_ref[i:i + 1, :] * ext[5 + i:5 + i + q, :]
    tail = ext[q:q + 8, :]
    ext[0:8, :] = tail
    u = u * _sigmoid(u)
    xs = u[:, :D_MODEL]
    bm = [u[:, D_MODEL + g * D_STATE:D_MODEL + (g + 1) * D_STATE].astype(BF16) for g in range(N_GROUPS)]
    cm = [u[:, D_MODEL + (N_GROUPS + g) * D_STATE:D_MODEL + (N_GROUPS + g + 1) * D_STATE].astype(BF16)
          for g in range(N_GROUPS)]

    dt = _softplus(dt_ref[...] + dtb_ref[...])
    d_a = dt * (-jnp.exp(alog_ref[...]))
    row = _iota((q, q), 0)
    col = _iota((q, q), 1)
    causal = col <= row
    tril = causal.astype(BF16)
    cs = sum(_dot(tril, piece) for piece in _pieces(d_a, 3))
    eye = (row == col).astype(BF16)
    cs_t = sum(_dot_tn(piece, eye) for piece in _pieces(cs, 3))
    last = cs[q - 1:q, :]
    expand = lambda t: sum(_dot(piece, e_ref[...]) for piece in _pieces(t, 2))
    dt_x = expand(dt)
    dte_x = expand(jnp.exp(last - cs))
    ecs_x = expand(jnp.exp(cs))
    xdt = xs * dt_x
    cd = jnp.exp(cs_t[:, q - 1:q])
    cd_b = jnp.broadcast_to(cd, (LANES, D_STATE))
    decay_full = sum(_dot(et_ref[...], piece) for piece in _pieces(cd_b, 2))

    ys = []
    for g in range(N_GROUPS):
        cb_g = _dot_nt(cm[g], bm[g])
        for j in range(N_PAIRS // N_GROUPS):
            pi = g * (N_PAIRS // N_GROUPS) + j
            lanes = slice(pi * PAIR, (pi + 1) * PAIR)
            gs = []
            for hh in (2 * pi, 2 * pi + 1):
                seg = cs[:, hh:hh + 1] - cs_t[hh:hh + 1, :]
                gs.append(cb_g * jnp.exp(jnp.where(causal, seg, -1e30)))
            g_pair = jnp.concatenate(gs, axis=1).astype(BF16)
            xdt_p = xdt[:, lanes]
            y_diag = _dot(g_pair, _stack(xdt_p).astype(BF16))
            h_p = h_scr[pi * PAIR:(pi + 1) * PAIR, :]
            y_off = _dot_nt(cm[g], h_p.astype(BF16)) * ecs_x[:, lanes]
            ys.append(y_diag + y_off)
            states = _dot_tn((xdt_p * dte_x[:, lanes]).astype(BF16), bm[g])
            h_scr[pi * PAIR:(pi + 1) * PAIR, :] = decay_full[pi * PAIR:(pi + 1) * PAIR, :] * h_p + states
    y = jnp.concatenate(ys, axis=1)
    y_ref[...] = _gate_norm(y, xs, zm_ref[...], dsk_ref[...], gn_ref[...])

    @pl.when(c == pl.num_programs(1) - 1)
    def _():
        cout_ref[0] = ext[8 + q - (CONV_W - 1):8 + q, :]
        hout_ref[0] = h_scr[...]


def _ssd_consts(lp):
    heads = jnp.arange(LANES)[:, None]
    cols = jnp.arange(D_MODEL)[None, :] // HEAD
    e_mat = (heads == cols).astype(BF16)
    pad = lambda a: jnp.pad(a.reshape(1, N_HEADS), ((0, 0), (0, DT_PAD - N_HEADS)))
    return dict(cw=lp["conv_w"], cb=lp["conv_b"].reshape(1, CONV_DIM), dtb=pad(lp["dt_bias"]),
                alog=pad(lp["a_log"]), dsk=jnp.repeat(lp["d_skip"], HEAD).reshape(1, D_MODEL),
                gn=lp["gnorm_w"].reshape(1, D_MODEL), e=e_mat, et=e_mat.T)


def _ssd_chunk(xbc, zm, dt, lp, c0_pad, h0, *, batch, seq_len):
    q = SSD_CHUNK
    nq = seq_len // q
    m = batch * seq_len
    has_state = h0 is not None
    k = _ssd_consts(lp)
    full = lambda shape: pl.BlockSpec(shape, lambda b, c: tuple(0 for _ in shape))
    rows = lambda w: pl.BlockSpec((q, w), lambda b, c: (b * nq + c, 0))
    ins = [xbc, zm, dt, k["cw"], k["cb"], k["dtb"], k["alog"], k["dsk"], k["gn"], k["e"], k["et"]]
    in_specs = [rows(CONV_DIM), rows(D_MODEL), rows(DT_PAD), full((CONV_W, CONV_DIM)), full((1, CONV_DIM)),
                full((1, DT_PAD)), full((1, DT_PAD)), full((1, D_MODEL)), full((1, D_MODEL)),
                full((LANES, D_MODEL)), full((D_MODEL, LANES))]
    if has_state:
        ins += [c0_pad, h0]
        in_specs += [pl.BlockSpec((1, 8, CONV_DIM), lambda b, c: (b, 0, 0)),
                     pl.BlockSpec((1, D_MODEL, D_STATE), lambda b, c: (b, 0, 0))]
    return pl.pallas_call(
        functools.partial(_ssd_chunk_kernel, q=q, has_state=has_state),
        grid=(batch, nq),
        in_specs=in_specs,
        out_specs=[rows(D_MODEL), pl.BlockSpec((1, CONV_W - 1, CONV_DIM), lambda b, c: (b, 0, 0)),
                   pl.BlockSpec((1, D_MODEL, D_STATE), lambda b, c: (b, 0, 0))],
        out_shape=[jax.ShapeDtypeStruct((m, D_MODEL), F32),
                   jax.ShapeDtypeStruct((batch, CONV_W - 1, CONV_DIM), F32),
                   jax.ShapeDtypeStruct((batch, D_MODEL, D_STATE), F32)],
        scratch_shapes=[pltpu.VMEM((q + 8, CONV_DIM), F32), pltpu.VMEM((D_MODEL, D_STATE), F32)],
        compiler_params=_cparams(("arbitrary", "arbitrary")),
        name="ssd_chunk",
    )(*ins)


def _ssd_step_kernel(*refs, nseq, aliased):
    (xbc_ref, c_ref, zm_ref, dt_ref, cw_ref, cb_ref, dtb_ref, alog_ref, dsk_ref, gn_ref, e_ref,
     h0_ref) = refs[:12]
    y_ref, cout_ref, hout_ref = refs[14:17] if aliased else refs[12:15]
    rowid = _iota((nseq, CONV_DIM), 0)
    u = cb_ref[...] + cw_ref[3:4, :] * xbc_ref[...]
    for i in range(CONV_W - 1):
        prev_rows = jnp.zeros((nseq, CONV_DIM), F32)
        for j in range(nseq):
            prev_rows = jnp.where(rowid == j, c_ref[j, i:i + 1, :], prev_rows)
        u = u + cw_ref[i:i + 1, :] * prev_rows
    for j in range(nseq):
        cout_ref[j, 0:CONV_W - 2, :] = c_ref[j, 1:CONV_W - 1, :]
        cout_ref[j, CONV_W - 2:CONV_W - 1, :] = xbc_ref[j:j + 1, :]
    u = u * _sigmoid(u)
    xs = u[:, :D_MODEL]
    dt = _softplus(dt_ref[...] + dtb_ref[...])
    ed = jnp.exp(dt * (-jnp.exp(alog_ref[...])))
    expand = lambda t: sum(_dot(piece, e_ref[...]) for piece in _pieces(t, 2))
    xdt = xs * expand(dt)
    ed_x = expand(ed)
    lane = _iota((PAIR, LANES), 1)
    ys = []
    for p in range(N_PAIRS):
        g = p // (N_PAIRS // N_GROUPS)
        pl_ = slice(p * PAIR, (p + 1) * PAIR)
        b_rows = u[:, D_MODEL + g * D_STATE:D_MODEL + (g + 1) * D_STATE]
        c_rows = u[:, D_MODEL + (N_GROUPS + g) * D_STATE:D_MODEL + (N_GROUPS + g + 1) * D_STATE]
        x_cols = _cols_of(xdt[:, pl_])
        d_cols = _cols_of(ed_x[:, pl_])
        h1s = [d_cols[:, j:j + 1] * h0_ref[j, pl_, :] + x_cols[:, j:j + 1] * b_rows[j:j + 1, :]
               for j in range(nseq)]
        for j in range(nseq):
            hout_ref[j, pl_, :] = h1s[j]
        y_cols = [jnp.sum(h1s[j] * c_rows[j:j + 1, :], axis=-1, keepdims=True) for j in range(nseq)]
        y_t = jnp.zeros((PAIR, LANES), F32)
        for j in range(nseq):
            y_t = jnp.where(lane == j, y_cols[j], y_t)
        ys.append(y_t.T[0:nseq, :])
    y = jnp.concatenate(ys, axis=1)
    y_ref[...] = _gate_norm(y, xs, zm_ref[...], dsk_ref[...], gn_ref[...])


def _ssd_step(xbc, zm, dt, lp, conv_all, ssm_all, layer, prev_conv, prev_ssm):
    m = xbc.shape[0]
    nseq = 8
    k = _ssd_consts(lp)
    full = lambda shape: pl.BlockSpec(shape, lambda i: tuple(0 for _ in shape))
    rows = lambda w: pl.BlockSpec((nseq, w), lambda i: (i, 0))
    cst = pl.BlockSpec((None, nseq, CONV_W - 1, CONV_DIM), lambda i: (layer, i, 0, 0))
    hst = pl.BlockSpec((None, nseq, D_MODEL, D_STATE), lambda i: (layer, i, 0, 0))
    ins = [xbc, conv_all, zm, dt, k["cw"], k["cb"], k["dtb"], k["alog"], k["dsk"], k["gn"], k["e"], ssm_all]
    in_specs = [rows(CONV_DIM), cst, rows(D_MODEL), rows(DT_PAD), full((CONV_W, CONV_DIM)),
                full((1, CONV_DIM)), full((1, DT_PAD)), full((1, DT_PAD)), full((1, D_MODEL)),
                full((1, D_MODEL)), full((LANES, D_MODEL)), hst]
    aliases = {}
    aliased = prev_conv is not None
    if aliased:
        ins += [prev_conv, prev_ssm]
        in_specs += [pl.BlockSpec(memory_space=pl.ANY)] * 2
        aliases = {len(ins) - 2: 1, len(ins) - 1: 2}
    return pl.pallas_call(
        functools.partial(_ssd_step_kernel, nseq=nseq, aliased=aliased),
        grid=(m // nseq,),
        in_specs=in_specs,
        out_specs=[rows(D_MODEL), cst, hst],
        out_shape=[jax.ShapeDtypeStruct((m, D_MODEL), F32), jax.ShapeDtypeStruct(conv_all.shape, F32),
                   jax.ShapeDtypeStruct(ssm_all.shape, F32)],
        input_output_aliases=aliases,
        compiler_params=_cparams(("arbitrary",)),
        name="ssd_step",
    )(*ins)


def _out_kernel(*refs, final):
    if final:
        x_ref, yr_ref, ym_ref, w_ref, fw_ref, o_ref = refs
    else:
        x_ref, yr_ref, ym_ref, w_ref, o_ref = refs
    out = (x_ref[...] + _dot(yr_ref[...].astype(BF16), w_ref[0:D_MODEL, :])
           + _dot(ym_ref[...].astype(BF16), w_ref[D_MODEL:2 * D_MODEL, :]))
    if final:
        out = out * lax.rsqrt(jnp.mean(out * out, axis=-1, keepdims=True) + NORM_EPS) * fw_ref[...]
    o_ref[...] = out


def _out_proj(x, yr, ym, w_bf, final_w, tm):
    m, d = x.shape
    final = final_w is not None
    tok = pl.BlockSpec((tm, d), lambda i: (i, 0))
    ins = [x, yr, ym, w_bf] + ([final_w.reshape(1, d)] if final else [])
    in_specs = [tok, tok, tok, pl.BlockSpec((2 * d, d), lambda i: (0, 0))]
    if final:
        in_specs.append(pl.BlockSpec((1, d), lambda i: (0, 0)))
    return pl.pallas_call(
        functools.partial(_out_kernel, final=final),
        grid=(m // tm,),
        in_specs=in_specs,
        out_specs=tok,
        out_shape=jax.ShapeDtypeStruct((m, d), F32),
        compiler_params=_cparams(("arbitrary",)),
        name="out_proj",
    )(*ins)


ALL_SEGS = (SEG_SH, SEG_ZR, SEG_ZM, SEG_XBC, SEG_DT)


def _layer_prompt(x, lp, final_w, *, batch, seq_len):
    tm = 256
    p_sh, z_r, z_m, xbc, dt = _proj(x, lp["norm_w"], lp["w_in"], ALL_SEGS, tm)
    xn_last = _rmsnorm(x.reshape(batch, seq_len, D_MODEL)[:, -1], lp["norm_w"])
    prev = jnp.zeros((batch, 1, R_SHIFT), F32)
    kt, bh, kh, rt, v, gc = _prep(p_sh, prev, lp, seq_len=seq_len, tm=tm)
    y_r, s_bd = _wkv_chunk(kt, bh, kh, rt, v, z_r, gc, lp, None, batch=batch, seq_len=seq_len)
    y_m, conv_new, ssm_new = _ssd_chunk(xbc, z_m, dt, lp, None, None, batch=batch, seq_len=seq_len)
    x_new = _out_proj(x, y_r, y_m, lp["w_out"], final_w, tm)
    return (x_new, xn_last, _from_block_diag(s_bd), conv_new,
            ssm_new.reshape(batch, N_HEADS, HEAD, D_STATE))


def _layer_sample(x, states, layer, prev_outs, lp, final_w):
    m = x.shape[0]
    tm = m
    shift_all, wkv_all, conv_all, ssm_all = states
    p_sh, z_r, z_m, xbc, dt, xn = _proj(x, lp["norm_w"], lp["w_in"], ALL_SEGS, tm, emit_xn=True)
    (prev,) = _proj(shift_all[layer], None, lp["w_in"], (SEG_SH,), tm)
    kap, b, k2, r, v, w = _prep(p_sh, prev, lp, seq_len=1, tm=tm)
    y_r, wkv_out = _wkv_step(kap, b, k2, r, v, w, z_r, lp, wkv_all, layer, prev_outs[0])
    y_m, conv_out, ssm_out = _ssd_step(xbc, z_m, dt, lp, conv_all, ssm_all, layer, prev_outs[1], prev_outs[2])
    x_new = _out_proj(x, y_r, y_m, lp["w_out"], final_w, tm)
    return x_new, xn, (wkv_out, conv_out, ssm_out)


def _layer_params(params, l):
    lp = {k: v[l] for k, v in params.items()}
    lp["w_in"] = jnp.pad(lp["w_in"], ((0, 0), (0, D_IN_PAD - lp["w_in"].shape[1]))).astype(BF16)
    lp["w_out"] = lp["w_out"].astype(BF16)
    zeros = jnp.zeros((LORA, D_MODEL), F32)
    lp["w2p"] = jnp.concatenate([lp["w_lora2"], zeros], axis=0).astype(BF16)
    lp["a2p"] = jnp.concatenate([zeros, lp["a_lora2"]], axis=0).astype(BF16)
    return lp


def kernel(x_prompt, x_sample, state_shift, state_wkv, state_conv, state_ssm, norm_w, w_in, mu_shift, w0,
           w_lora2, a0, a_lora2, k_k, k_a, r_k, lnx_w, lnx_b, conv_w, conv_b, dt_bias, a_log, d_skip,
           gnorm_w, w_out, final_norm_w):
    params = dict(norm_w=norm_w, w_in=w_in, mu_shift=mu_shift, w0=w0, w_lora2=w_lora2, a0=a0,
                  a_lora2=a_lora2, k_k=k_k, k_a=k_a, r_k=r_k, lnx_w=lnx_w, lnx_b=lnx_b, conv_w=conv_w,
                  conv_b=conv_b, dt_bias=dt_bias, a_log=a_log, d_skip=d_skip, gnorm_w=gnorm_w, w_out=w_out)
    depth = norm_w.shape[0]
    bp, lseq, d = x_prompt.shape
    bs = x_sample.shape[0]
    xp = x_prompt.reshape(bp * lseq, d)
    xs = x_sample.reshape(bs, d)
    sample_states = (state_shift, state_wkv, state_conv,
                     state_ssm.reshape(depth, bs, N_HEADS * HEAD, D_STATE))
    p_states, s_shift = [], []
    s_outs = (None, None, None)
    for l in range(depth):
        lp = _layer_params(params, l)
        fw = final_norm_w if l == depth - 1 else None
        xp, *st = _layer_prompt(xp, lp, fw, batch=bp, seq_len=lseq)
        p_states.append(st)
        xs, xn, s_outs = _layer_sample(xs, sample_states, l, s_outs, lp, fw)
        s_shift.append(xn)
    stack = lambda states, i: jnp.stack([s[i] for s in states])
    return (xp.reshape(bp, lseq, d), xs.reshape(bs, 1, d),
            stack(p_states, 0), stack(p_states, 1), stack(p_states, 2), stack(p_states, 3),
            jnp.stack(s_shift), s_outs[0], s_outs[1], s_outs[2].reshape(state_ssm.shape))
```

```python
import functools

import jax
import jax.numpy as jnp
from jax import lax
from jax.experimental import pallas as pl
from jax.experimental.pallas import tpu as pltpu

F32 = jnp.float32
BF16 = jnp.bfloat16

D_MODEL = 1024
HEAD = 64
N_HEADS = 16
LORA = 64
R_SHIFT = 3 * D_MODEL + 2 * LORA
D_STATE = 128
N_GROUPS = 2
CONV_W = 4
CONV_DIM = D_MODEL + 2 * N_GROUPS * D_STATE
DT_PAD = 128
SEG_SH = (0, R_SHIFT)
SEG_ZR = (R_SHIFT, R_SHIFT + D_MODEL)
SEG_ZM = (SEG_ZR[1], SEG_ZR[1] + D_MODEL)
SEG_XBC = (SEG_ZM[1], SEG_ZM[1] + CONV_DIM)
SEG_DT = (SEG_XBC[1], SEG_XBC[1] + DT_PAD)
D_IN_PAD = SEG_DT[1]
NORM_EPS = 1e-5
LNX_EPS = 64e-5
DECAY_SCALE = 0.6065306597126334
LANES = 128
PAIR = 2 * HEAD
N_PAIRS = N_HEADS // 2
WKV_CHUNK = 64
SSD_CHUNK = 128
WKV_PAIRS_PER_STEP = 8
VMEM_LIMIT = 56 * 1024 * 1024


def _cparams(sem):
    return pltpu.CompilerParams(dimension_semantics=sem, vmem_limit_bytes=VMEM_LIMIT)


def _dot(a, b):
    return jnp.dot(a, b, preferred_element_type=F32)


def _dot_nt(a, b):
    return lax.dot_general(a, b, (((1,), (1,)), ((), ())), preferred_element_type=F32)


def _dot_tn(a, b):
    return lax.dot_general(a, b, (((0,), (0,)), ((), ())), preferred_element_type=F32)


def _pieces(x, n):
    out, r = [], x
    for _ in range(n):
        p = r.astype(BF16)
        out.append(p)
        r = r - p.astype(F32)
    return out


def _sigmoid(x):
    return 1.0 / (1.0 + jnp.exp(-x))


def _softplus(x):
    return jnp.maximum(x, 0.0) + jnp.log(1.0 + jnp.exp(-jnp.abs(x)))


def _iota(shape, dim):
    return lax.broadcasted_iota(jnp.int32, shape, dim)


def _proj_kernel(*refs, segs, norm, emit_xn):
    refs = list(refs)
    x_ref = refs.pop(0)
    nw_ref = refs.pop(0) if norm else None
    w_ref = refs.pop(0)
    x = x_ref[...]
    if norm:
        xn = x * lax.rsqrt(jnp.mean(x * x, axis=-1, keepdims=True) + NORM_EPS) * nw_ref[...]
    else:
        xn = x
    xb = xn.astype(BF16)
    for (lo, hi), o_ref in zip(segs, refs):
        o_ref[...] = _dot(xb, w_ref[:, lo:hi])
    if emit_xn:
        refs[len(segs)][...] = xn


def _proj(x, norm_w, w_bf, segs, tm, emit_xn=False):
    m, d = x.shape
    n = w_bf.shape[1]
    norm = norm_w is not None
    ins = [x] + ([norm_w.reshape(1, d)] if norm else []) + [w_bf]
    in_specs = [pl.BlockSpec((tm, d), lambda i: (i, 0))]
    if norm:
        in_specs.append(pl.BlockSpec((1, d), lambda i: (0, 0)))
    in_specs.append(pl.BlockSpec((d, n), lambda i: (0, 0)))
    widths = [hi - lo for lo, hi in segs] + ([d] if emit_xn else [])
    return pl.pallas_call(
        functools.partial(_proj_kernel, segs=tuple(segs), norm=norm, emit_xn=emit_xn),
        grid=(m // tm,),
        in_specs=in_specs,
        out_specs=[pl.BlockSpec((tm, w), lambda i: (i, 0)) for w in widths],
        out_shape=[jax.ShapeDtypeStruct((m, w), F32) for w in widths],
        compiler_params=_cparams(("arbitrary",)),
        name="norm_proj",
    )(*ins)


def _cast_w_kernel(wt_ref, o_ref, *, n_valid):
    rows = _iota(wt_ref.shape, 0) + pl.program_id(0) * LANES
    w_t = jnp.where(rows < n_valid, wt_ref[...], 0.0)
    o_ref[...] = w_t.T.astype(BF16)


def _cast_w_in(w_t):
    n, d = w_t.shape
    return pl.pallas_call(
        functools.partial(_cast_w_kernel, n_valid=n),
        grid=(D_IN_PAD // LANES,),
        in_specs=[pl.BlockSpec((LANES, d), lambda j: (j, 0))],
        out_specs=pl.BlockSpec((d, LANES), lambda j: (0, j)),
        out_shape=jax.ShapeDtypeStruct((d, D_IN_PAD), BF16),
        compiler_params=_cparams(("arbitrary",)),
        name="cast_w_in",
    )(w_t)


def _rmsnorm_kernel(x_ref, w_ref, o_ref):
    x = x_ref[...]
    o_ref[...] = x * lax.rsqrt(jnp.mean(x * x, axis=-1, keepdims=True) + NORM_EPS) * w_ref[...]


def _rmsnorm(x, w):
    m, d = x.shape
    return pl.pallas_call(
        _rmsnorm_kernel,
        out_shape=jax.ShapeDtypeStruct((m, d), F32),
        name="rmsnorm_rows",
    )(x, w.reshape(1, d))


def _proj_prep_kernel(*refs, tm, chunk, tiles_per_seq, single, emit_xn):
    (x_ref, nw_ref, w_ref, prev_ref, mu_ref, w0_ref, a0_ref, kk_ref, ka_ref, w2_ref, a2_ref, bd_ref) = refs[:12]
    refs = list(refs[12:])
    tri_ref = None if single else refs.pop(0)
    o_a, o_b, o_c, o_d, o_v, o_g, zr_ref, zm_ref, xbc_ref, dt_ref = refs[:10]
    refs = refs[10:]
    xn_ref = refs.pop(0) if emit_xn else None
    carry_ref = None if single else refs.pop(0)

    x = x_ref[...]
    xn = x * lax.rsqrt(jnp.mean(x * x, axis=-1, keepdims=True) + NORM_EPS) * nw_ref[...]
    if emit_xn:
        xn_ref[...] = xn
    xb = xn.astype(BF16)
    seg = lambda lo, hi: _dot(xb, w_ref[:, lo:hi])
    if not single:
        first_tile = pl.program_id(0) % tiles_per_seq == 0

        @pl.when(first_tile)
        def _():
            carry_ref[...] = jnp.broadcast_to(prev_ref[0], carry_ref.shape)

    def mixed(lo, hi):
        p = seg(lo, hi)
        if single:
            shifted = prev_ref[:, lo:hi]
        else:
            shifted = jnp.where(_iota(p.shape, 0) == 0, carry_ref[0:1, lo:hi], pltpu.roll(p, 1, 0))
            carry_ref[:, lo:hi] = jnp.broadcast_to(p[tm - 1:tm, :], (carry_ref.shape[0], hi - lo))
        return p + (shifted - p) * mu_ref[:, lo:hi]

    def head_sum(t):
        hi, lo = _pieces(t, 2)
        cols = []
        for j in range(D_MODEL // 256):
            sl = slice(j * 256, (j + 1) * 256)
            cols.append(_dot(hi[:, sl], bd_ref[...]) + _dot(lo[:, sl], bd_ref[...]))
        return jnp.concatenate(cols, axis=1)

    wal = mixed(3 * D_MODEL, R_SHIFT)
    k = mixed(D_MODEL, 2 * D_MODEL)
    zr_ref[...] = seg(*SEG_ZR)
    dw = _dot(jnp.tanh(wal).astype(BF16), w2_ref[...])
    da = _dot(wal.astype(BF16), a2_ref[...])
    zm_ref[...] = seg(*SEG_ZM)
    logw = -DECAY_SCALE * _sigmoid(w0_ref[...] + dw)
    a = _sigmoid(a0_ref[...] + da)
    kk = k * kk_ref[...]
    if not single:
        cum = sum(_dot(tri_ref[...], piece) for piece in _pieces(logw, 3))
    kap = kk / jnp.maximum(jnp.sqrt(head_sum(kk * kk)), 1e-12)
    r = mixed(0, D_MODEL)
    v = mixed(2 * D_MODEL, 3 * D_MODEL)
    xbc_ref[...] = seg(*SEG_XBC)
    dt_ref[...] = seg(*SEG_DT)
    k2 = k * (1.0 + (a - 1.0) * ka_ref[...])
    o_v[...] = v
    if single:
        o_a[...] = kap
        o_b[...] = kap * a
        o_c[...] = k2
        o_d[...] = r
        o_g[...] = jnp.exp(logw)
    else:
        e_inv = jnp.exp(-cum)
        e_cum = jnp.exp(cum)
        o_a[...] = kap * jnp.exp(cum - logw)
        o_b[...] = kap * a * e_inv
        o_c[...] = k2 * e_inv
        o_d[...] = r * e_cum
        for j in range(tm // chunk):
            o_g[j] = e_cum[j * chunk + chunk - 1:j * chunk + chunk, :]


def _proj_prep(x, prev, lp, *, seq_len, tm, emit_xn=False):
    m, d = x.shape
    single = seq_len == 1
    chunk = 1 if single else WKV_CHUNK
    tiles_per_seq = 1 if single else seq_len // tm
    vec = lambda a, n: a.reshape(1, n)
    bd = (jnp.arange(256)[:, None] // HEAD == jnp.arange(256)[None, :] // HEAD).astype(BF16)
    ins = [x, vec(lp["norm_w"], d), lp["w_in"], prev, vec(lp["mu_shift"], R_SHIFT), vec(lp["w0"], D_MODEL),
           vec(lp["a0"], D_MODEL), vec(lp["k_k"], D_MODEL), vec(lp["k_a"], D_MODEL), lp["w2p"], lp["a2p"], bd]
    full = lambda shape: pl.BlockSpec(shape, lambda i: tuple(0 for _ in shape))
    if single:
        prev_spec = pl.BlockSpec((tm, R_SHIFT), lambda i: (i, 0))
    else:
        prev_spec = pl.BlockSpec((1, 1, R_SHIFT), lambda i: (i // tiles_per_seq, 0, 0))
    in_specs = [pl.BlockSpec((tm, d), lambda i: (i, 0)), full((1, d)), full((d, D_IN_PAD)), prev_spec,
                full((1, R_SHIFT)), full((1, D_MODEL)), full((1, D_MODEL)), full((1, D_MODEL)),
                full((1, D_MODEL)), full((2 * LORA, D_MODEL)), full((2 * LORA, D_MODEL)), full((256, 256))]
    scratch = []
    if not single:
        t = jnp.arange(tm)
        tri = ((t[:, None] // chunk == t[None, :] // chunk) & (t[None, :] <= t[:, None])).astype(BF16)
        ins.append(tri)
        in_specs.append(full((tm, tm)))
        scratch.append(pltpu.VMEM((8, R_SHIFT), F32))
    rows = lambda w: pl.BlockSpec((tm, w), lambda i: (i, 0))
    shp = lambda w: jax.ShapeDtypeStruct((m, w), F32)
    if single:
        g_spec, g_shape = rows(D_MODEL), shp(D_MODEL)
    else:
        g_spec = pl.BlockSpec((tm // chunk, 1, D_MODEL), lambda i: (i, 0, 0))
        g_shape = jax.ShapeDtypeStruct((m // chunk, 1, D_MODEL), F32)
    widths = [D_MODEL, D_MODEL, CONV_DIM, DT_PAD] + ([d] if emit_xn else [])
    return pl.pallas_call(
        functools.partial(_proj_prep_kernel, tm=tm, chunk=chunk, tiles_per_seq=tiles_per_seq, single=single,
                          emit_xn=emit_xn),
        grid=(m // tm,),
        in_specs=in_specs,
        out_specs=[rows(D_MODEL)] * 5 + [g_spec] + [rows(w) for w in widths],
        out_shape=[shp(D_MODEL)] * 5 + [g_shape] + [shp(w) for w in widths],
        scratch_shapes=scratch,
        compiler_params=_cparams(("arbitrary",)),
        name="proj_prep",
    )(*ins)


def _stack(x):
    first = _iota(x.shape, 1) < HEAD
    return jnp.concatenate([jnp.where(first, x, 0.0), jnp.where(first, 0.0, x)], axis=0)


def _wkv_post(o_s, rs, ks, vs, rk_row, lw_row, lb_row, z, rows):
    own = (_iota(o_s.shape, 0) >> (rows.bit_length() - 1)) == (_iota(o_s.shape, 1) >> (HEAD.bit_length() - 1))
    mean = jnp.sum(o_s, axis=-1, keepdims=True) * (1.0 / HEAD)
    dev = jnp.where(own, o_s - mean, 0.0)
    var = jnp.sum(dev * dev, axis=-1, keepdims=True) * (1.0 / HEAD)
    normed = dev * lax.rsqrt(var + LNX_EPS)
    bonus = jnp.sum(rs * ks * rk_row, axis=-1, keepdims=True) * vs
    fold = lambda t: t[0:rows] + t[rows:2 * rows]
    y = fold(normed) * lw_row + lb_row + fold(bonus)
    return y * (z * _sigmoid(z))


def _each(f, *lists):
    return [f(*a) for a in zip(*lists)]


def _wkv_pairs_chunk(kt, bh, kh, rt, v, z, g_row, rk_row, lw_row, lb_row, s0, chunk):
    n2 = 2 * chunk
    bf = lambda t: t.astype(BF16)
    xs, bs, ks, rs, vs = (_each(_stack, t) for t in (kt, bh, kh, rt, v))
    xb, bb, kb, rb, vb = (_each(bf, t) for t in (xs, bs, ks, rs, vs))

    row = _iota((n2, n2), 0)
    col = _iota((n2, n2), 1)
    same = (row >> (chunk.bit_length() - 1)) == (col >> (chunk.bit_length() - 1))
    rt_ = row & (chunk - 1)
    ct_ = col & (chunk - 1)
    strict = same & (ct_ < rt_)
    incl = same & (ct_ <= rt_)
    lower = lambda a: jnp.where(strict, a, 0.0)
    lower_d = lambda a: jnp.where(incl, a, 0.0)

    gram = _each(lambda x, r, k, b: _dot_nt(jnp.concatenate([x, r], axis=0), jnp.concatenate([k, b], axis=0)),
                 xb, rb, kb, bb)
    a_kb = _each(lambda g: lower(g[:n2, n2:]), gram)
    a_kr = _each(lambda g: bf(jnp.concatenate([lower(g[:n2, :n2]), lower_d(g[n2:, :n2])], axis=0)), gram)
    a_rb = _each(lambda g: bf(lower_d(g[n2:, n2:])), gram)

    eye = (row == col).astype(F32)
    pair2 = (rt_ | 1) == (ct_ | 1)
    t_inv = _each(lambda a: eye - jnp.where(pair2, a, 0.0), a_kb)
    s = 2
    while s < chunk:
        off = same & ((rt_ & ~(2 * s - 1)) == (ct_ & ~(2 * s - 1))) & ((rt_ & s) != 0) & ((ct_ & s) == 0)
        b_off = _each(lambda a: bf(jnp.where(off, a, 0.0)), a_kb)
        t_b = _each(bf, t_inv)
        bt = _each(lambda b, t: bf(_dot(b, t)), b_off, t_b)
        t_inv = _each(lambda t, tb, x: t - _dot(tb, x), t_inv, t_b, bt)
        s *= 2
    t_b = _each(bf, t_inv)

    xa = _each(_dot, a_kr, vb)
    wu_b = _each(lambda t, x, y: bf(_dot(t, bf(jnp.concatenate([x, y[:n2]], axis=1)))), t_b, xs, xa)
    pq = _each(lambda r, a, arb, wu: jnp.concatenate([r, a[n2:]], axis=1) - _dot(arb, wu), rs, xa, a_rb, wu_b)
    m_low = _each(lambda wu, b, g: bf(_dot_tn(wu[:, :PAIR], b) * g), wu_b, bb, g_row)
    n_t = _each(lambda v_, wu, k, b, g: _dot_tn(jnp.concatenate([v_, -wu[:, PAIR:]], axis=0),
                                                jnp.concatenate([k, b], axis=0)) * g,
                vb, wu_b, kb, bb, g_row)

    s0_b = _each(bf, s0)
    o_s = _each(lambda p, sb: _dot_nt(bf(p[:, :PAIR]), sb) + p[:, PAIR:], pq, s0_b)
    s1 = _each(lambda s_, sb, m, n, g: s_ * g - _dot(sb, m) + n, s0, s0_b, m_low, n_t, g_row)
    y = _each(lambda *a: _wkv_post(*a, chunk), o_s, rs, ks, vs, rk_row, lw_row, lb_row, z)
    return y, s1


def _wkv_chunk_kernel(*refs, chunk, npair, has_state):
    (kt_ref, bh_ref, kh_ref, rt_ref, v_ref, z_ref, gc_ref, rk_ref, lw_ref, lb_ref) = refs[:10]
    refs = refs[10:]
    if has_state:
        s0_ref, y_ref, sout_ref, s_scr = refs
    else:
        y_ref, sout_ref, s_scr = refs
    c = pl.program_id(2)

    @pl.when(c == 0)
    def _():
        s_scr[...] = s0_ref[0] if has_state else jnp.zeros(s_scr.shape, F32)

    lanes = [slice(g * PAIR, (g + 1) * PAIR) for g in range(npair)]
    tiles = lambda ref: [ref[:, ln] for ln in lanes]
    y, s1 = _wkv_pairs_chunk(tiles(kt_ref), tiles(bh_ref), tiles(kh_ref), tiles(rt_ref), tiles(v_ref),
                             tiles(z_ref), [gc_ref[0, :, ln] for ln in lanes], tiles(rk_ref), tiles(lw_ref),
                             tiles(lb_ref), [s_scr[g] for g in range(npair)], chunk)
    for g, ln in enumerate(lanes):
        y_ref[:, ln] = y[g]
        s_scr[g] = s1[g]

    @pl.when(c == pl.num_programs(2) - 1)
    def _():
        sout_ref[0] = s_scr[...]


def _wkv_chunk(kt, bh, kh, rt, v, z, gc, lp, s0_bd, *, batch, seq_len):
    chunk = WKV_CHUNK
    npair = WKV_PAIRS_PER_STEP
    width = npair * PAIR
    nc = seq_len // chunk
    m = batch * seq_len
    has_state = s0_bd is not None
    tok = pl.BlockSpec((chunk, width), lambda b, g, c: (b * nc + c, g))
    par = pl.BlockSpec((1, width), lambda b, g, c: (0, g))
    st = pl.BlockSpec((1, npair, PAIR, PAIR), lambda b, g, c: (b, g, 0, 0))
    ins = [kt, bh, kh, rt, v, z, gc, lp["r_k"].reshape(1, D_MODEL), lp["lnx_w"].reshape(1, D_MODEL),
           lp["lnx_b"].reshape(1, D_MODEL)]
    in_specs = [tok] * 6 + [pl.BlockSpec((1, 1, width), lambda b, g, c: (b * nc + c, 0, g)), par, par, par]
    if has_state:
        ins.append(s0_bd)
        in_specs.append(st)
    return pl.pallas_call(
        functools.partial(_wkv_chunk_kernel, chunk=chunk, npair=npair, has_state=has_state),
        grid=(batch, N_PAIRS // npair, nc),
        in_specs=in_specs,
        out_specs=[tok, st],
        out_shape=[jax.ShapeDtypeStruct((m, D_MODEL), F32),
                   jax.ShapeDtypeStruct((batch, N_PAIRS, PAIR, PAIR), F32)],
        scratch_shapes=[pltpu.VMEM((npair, PAIR, PAIR), F32)],
        compiler_params=_cparams(("arbitrary", "arbitrary", "arbitrary")),
        name="rwkv_chunk",
    )(*ins)


def _head_sum_nat(x):
    first = _iota(x.shape, 1) < HEAD
    s_first = jnp.sum(jnp.where(first, x, 0.0), axis=-1, keepdims=True)
    s_all = jnp.sum(x, axis=-1, keepdims=True)
    return jnp.where(first, s_first, s_all - s_first)


def _cols_of(tile):
    pad = jnp.zeros((LANES - tile.shape[0], LANES), F32)
    return jnp.concatenate([tile, pad], axis=0).T


def _wkv_step_kernel(*refs, aliased):
    (kap_ref, b_ref, k_ref, r_ref, v_ref, w_ref, z_ref, rk_ref, lw_ref, lb_ref, s0_ref) = refs[:11]
    y_ref, sout_ref, vt_scr, ot_scr = refs[12:] if aliased else refs[11:]
    tr = lambda ref: ref[...].T
    kap_t, b_t, k_t, r_t, w_t = tr(kap_ref), tr(b_ref), tr(k_ref), tr(r_ref), tr(w_ref)
    vt_scr[...] = tr(v_ref)

    def body(vi, carry):
        for hh in range(2):
            rows = slice(hh * HEAD, (hh + 1) * HEAD)
            s = s0_ref[hh, vi]
            s_kk = jnp.sum(s * kap_t[rows], axis=0, keepdims=True)
            v_row = vt_scr[pl.ds(hh * HEAD + vi, 1), :]
            s1 = s * w_t[rows] - s_kk * b_t[rows] + v_row * k_t[rows]
            sout_ref[hh, vi] = s1
            ot_scr[pl.ds(hh * HEAD + vi, 1), :] = jnp.sum(s1 * r_t[rows], axis=0, keepdims=True)
        return carry

    lax.fori_loop(0, HEAD, body, 0)
    o = ot_scr[...].T
    dev = o - _head_sum_nat(o) * (1.0 / HEAD)
    var = _head_sum_nat(dev * dev) * (1.0 / HEAD)
    bonus = _head_sum_nat(r_ref[...] * k_ref[...] * rk_ref[...]) * v_ref[...]
    z = z_ref[...]
    y = dev * lax.rsqrt(var + LNX_EPS) * lw_ref[...] + lb_ref[...] + bonus
    y_ref[...] = y * (z * _sigmoid(z))


def _wkv_step(kap, b, k2, r, v, w, z, lp, state_all_t, layer, prev_out):
    m = kap.shape[0]
    tok = pl.BlockSpec((m, PAIR), lambda g: (0, g))
    par = pl.BlockSpec((1, PAIR), lambda g: (0, g))
    st = pl.BlockSpec((None, 2, HEAD, HEAD, m), lambda g: (layer, g, 0, 0, 0))
    ins = [kap, b, k2, r, v, w, z, lp["r_k"].reshape(1, D_MODEL), lp["lnx_w"].reshape(1, D_MODEL),
           lp["lnx_b"].reshape(1, D_MODEL), state_all_t]
    in_specs = [tok] * 7 + [par, par, par, st]
    aliases = {}
    if prev_out is not None:
        ins.append(prev_out)
        in_specs.append(pl.BlockSpec(memory_space=pl.ANY))
        aliases = {len(ins) - 1: 1}
    return pl.pallas_call(
        functools.partial(_wkv_step_kernel, aliased=prev_out is not None),
        grid=(N_PAIRS,),
        in_specs=in_specs,
        out_specs=[tok, st],
        out_shape=[jax.ShapeDtypeStruct((m, D_MODEL), F32), jax.ShapeDtypeStruct(state_all_t.shape, F32)],
        scratch_shapes=[pltpu.VMEM((PAIR, m), F32), pltpu.VMEM((PAIR, m), F32)],
        input_output_aliases=aliases,
        compiler_params=_cparams(("arbitrary",)),
        name="rwkv_step",
    )(*ins)


def _from_block_diag(s):
    b = s.shape[0]
    sp = s.reshape(b, N_PAIRS, 2, HEAD, 2, HEAD)
    return jnp.stack([sp[:, :, 0, :, 0, :], sp[:, :, 1, :, 1, :]], axis=2).reshape(b, N_HEADS, HEAD, HEAD)


def _gate_norm(y, xs, zm, dsk_row, gn_row):
    y = (y + dsk_row * xs) * (zm * _sigmoid(zm))
    half = D_MODEL // N_GROUPS
    outs = []
    for g in range(N_GROUPS):
        yg = y[:, g * half:(g + 1) * half]
        outs.append(yg * lax.rsqrt(jnp.mean(yg * yg, axis=-1, keepdims=True) + NORM_EPS))
    return jnp.concatenate(outs, axis=1) * gn_row


def _ssd_chunk_kernel(*refs, q, has_state):
    (xbc_ref, zm_ref, dt_ref, cw_ref, cb_ref, dtb_ref, alog_ref, dsk_ref, gn_ref, e_ref, et_ref) = refs[:11]
    refs = refs[11:]
    if has_state:
        c0_ref, h0_ref, y_ref, cout_ref, hout_ref, ext, h_scr = refs
    else:
        y_ref, cout_ref, hout_ref, ext, h_scr = refs
    c = pl.program_id(1)

    @pl.when(c == 0)
    def _():
        if has_state:
            ext[0:8, :] = c0_ref[0]
            h_scr[...] = h0_ref[0]
        else:
            ext[0:8, :] = jnp.zeros((8, CONV_DIM), F32)
            h_scr[...] = jnp.zeros(h_scr.shape, F32)

    ext[8:8 + q, :] = xbc_ref[...]
    u = cb_ref[...] + cw_ref[3:4, :] * ext[8:8 + q, :]
    for i in range(CONV_W - 1):
        u = u + cw_ref[i:i + 1, :] * ext[5 + i:5 + i + q, :]
    tail = ext[q:q + 8, :]
    ext[0:8, :] = tail
    u = u * _sigmoid(u)
    xs = u[:, :D_MODEL]
    bm = [u[:, D_MODEL + g * D_STATE:D_MODEL + (g + 1) * D_STATE].astype(BF16) for g in range(N_GROUPS)]
    cm = [u[:, D_MODEL + (N_GROUPS + g) * D_STATE:D_MODEL + (N_GROUPS + g + 1) * D_STATE].astype(BF16)
          for g in range(N_GROUPS)]

    dt = _softplus(dt_ref[...] + dtb_ref[...])
    d_a = dt * (-jnp.exp(alog_ref[...]))
    row = _iota((q, q), 0)
    col = _iota((q, q), 1)
    causal = col <= row
    tril = causal.astype(BF16)
    cs = sum(_dot(tril, piece) for piece in _pieces(d_a, 3))
    eye = (row == col).astype(BF16)
    cs_t = sum(_dot_tn(piece, eye) for piece in _pieces(cs, 3))
    last = cs[q - 1:q, :]
    expand = lambda t: sum(_dot(piece, e_ref[...]) for piece in _pieces(t, 2))
    dt_x = expand(dt)
    dte_x = expand(jnp.exp(last - cs))
    ecs_x = expand(jnp.exp(cs))
    xdt = xs * dt_x
    cd = jnp.exp(cs_t[:, q - 1:q])
    cd_b = jnp.broadcast_to(cd, (LANES, D_STATE))
    decay_full = sum(_dot(et_ref[...], piece) for piece in _pieces(cd_b, 2))

    ys = []
    for g in range(N_GROUPS):
        cb_g = _dot_nt(cm[g], bm[g])
        for j in range(N_PAIRS // N_GROUPS):
            pi = g * (N_PAIRS // N_GROUPS) + j
            lanes = slice(pi * PAIR, (pi + 1) * PAIR)
            gs = []
            for hh in (2 * pi, 2 * pi + 1):
                seg = cs[:, hh:hh + 1] - cs_t[hh:hh + 1, :]
                gs.append(cb_g * jnp.exp(jnp.where(causal, seg, -1e30)))
            g_pair = jnp.concatenate(gs, axis=1).astype(BF16)
            xdt_p = xdt[:, lanes]
            y_diag = _dot(g_pair, _stack(xdt_p).astype(BF16))
            h_p = h_scr[pi * PAIR:(pi + 1) * PAIR, :]
            y_off = _dot_nt(cm[g], h_p.astype(BF16)) * ecs_x[:, lanes]
            ys.append(y_diag + y_off)
            states = _dot_tn((xdt_p * dte_x[:, lanes]).astype(BF16), bm[g])
            h_scr[pi * PAIR:(pi + 1) * PAIR, :] = decay_full[pi * PAIR:(pi + 1) * PAIR, :] * h_p + states
    y = jnp.concatenate(ys, axis=1)
    y_ref[...] = _gate_norm(y, xs, zm_ref[...], dsk_ref[...], gn_ref[...])

    @pl.when(c == pl.num_programs(1) - 1)
    def _():
        cout_ref[0] = ext[8 + q - (CONV_W - 1):8 + q, :]
        hout_ref[0] = h_scr[...]


def _ssd_consts(lp):
    heads = jnp.arange(LANES)[:, None]
    cols = jnp.arange(D_MODEL)[None, :] // HEAD
    e_mat = (heads == cols).astype(BF16)
    pad = lambda a: jnp.pad(a.reshape(1, N_HEADS), ((0, 0), (0, DT_PAD - N_HEADS)))
    return dict(cw=lp["conv_w"], cb=lp["conv_b"].reshape(1, CONV_DIM), dtb=pad(lp["dt_bias"]),
                alog=pad(lp["a_log"]), dsk=jnp.repeat(lp["d_skip"], HEAD).reshape(1, D_MODEL),
                gn=lp["gnorm_w"].reshape(1, D_MODEL), e=e_mat, et=e_mat.T)


def _ssd_chunk(xbc, zm, dt, lp, c0_pad, h0, *, batch, seq_len):
    q = SSD_CHUNK
    nq = seq_len // q
    m = batch * seq_len
    has_state = h0 is not None
    k = _ssd_consts(lp)
    full = lambda shape: pl.BlockSpec(shape, lambda b, c: tuple(0 for _ in shape))
    rows = lambda w: pl.BlockSpec((q, w), lambda b, c: (b * nq + c, 0))
    ins = [xbc, zm, dt, k["cw"], k["cb"], k["dtb"], k["alog"], k["dsk"], k["gn"], k["e"], k["et"]]
    in_specs = [rows(CONV_DIM), rows(D_MODEL), rows(DT_PAD), full((CONV_W, CONV_DIM)), full((1, CONV_DIM)),
                full((1, DT_PAD)), full((1, DT_PAD)), full((1, D_MODEL)), full((1, D_MODEL)),
                full((LANES, D_MODEL)), full((D_MODEL, LANES))]
    if has_state:
        ins += [c0_pad, h0]
        in_specs += [pl.BlockSpec((1, 8, CONV_DIM), lambda b, c: (b, 0, 0)),
                     pl.BlockSpec((1, D_MODEL, D_STATE), lambda b, c: (b, 0, 0))]
    return pl.pallas_call(
        functools.partial(_ssd_chunk_kernel, q=q, has_state=has_state),
        grid=(batch, nq),
        in_specs=in_specs,
        out_specs=[rows(D_MODEL), pl.BlockSpec((1, CONV_W - 1, CONV_DIM), lambda b, c: (b, 0, 0)),
                   pl.BlockSpec((1, D_MODEL, D_STATE), lambda b, c: (b, 0, 0))],
        out_shape=[jax.ShapeDtypeStruct((m, D_MODEL), F32),
                   jax.ShapeDtypeStruct((batch, CONV_W - 1, CONV_DIM), F32),
                   jax.ShapeDtypeStruct((batch, D_MODEL, D_STATE), F32)],
        scratch_shapes=[pltpu.VMEM((q + 8, CONV_DIM), F32), pltpu.VMEM((D_MODEL, D_STATE), F32)],
        compiler_params=_cparams(("arbitrary", "arbitrary")),
        name="ssd_chunk",
    )(*ins)


def _ssd_step_kernel(*refs, nseq, aliased):
    (xbc_ref, c_ref, zm_ref, dt_ref, cw_ref, cb_ref, dtb_ref, alog_ref, dsk_ref, gn_ref, e_ref,
     h0_ref) = refs[:12]
    y_ref, cout_ref, hout_ref = refs[14:17] if aliased else refs[12:15]
    rowid = _iota((nseq, CONV_DIM), 0)
    u = cb_ref[...] + cw_ref[3:4, :] * xbc_ref[...]
    for i in range(CONV_W - 1):
        prev_rows = jnp.zeros((nseq, CONV_DIM), F32)
        for j in range(nseq):
            prev_rows = jnp.where(rowid == j, c_ref[j, i:i + 1, :], prev_rows)
        u = u + cw_ref[i:i + 1, :] * prev_rows
    for j in range(nseq):
        cout_ref[j, 0:CONV_W - 2, :] = c_ref[j, 1:CONV_W - 1, :]
        cout_ref[j, CONV_W - 2:CONV_W - 1, :] = xbc_ref[j:j + 1, :]
    u = u * _sigmoid(u)
    xs = u[:, :D_MODEL]
    dt = _softplus(dt_ref[...] + dtb_ref[...])
    ed = jnp.exp(dt * (-jnp.exp(alog_ref[...])))
    expand = lambda t: sum(_dot(piece, e_ref[...]) for piece in _pieces(t, 2))
    xdt = xs * expand(dt)
    ed_x = expand(ed)
    lane = _iota((PAIR, LANES), 1)
    ys = []
    for p in range(N_PAIRS):
        g = p // (N_PAIRS // N_GROUPS)
        pl_ = slice(p * PAIR, (p + 1) * PAIR)
        b_rows = u[:, D_MODEL + g * D_STATE:D_MODEL + (g + 1) * D_STATE]
        c_rows = u[:, D_MODEL + (N_GROUPS + g) * D_STATE:D_MODEL + (N_GROUPS + g + 1) * D_STATE]
        x_cols = _cols_of(xdt[:, pl_])
        d_cols = _cols_of(ed_x[:, pl_])
        h1s = [d_cols[:, j:j + 1] * h0_ref[j, pl_, :] + x_cols[:, j:j + 1] * b_rows[j:j + 1, :]
               for j in range(nseq)]
        for j in range(nseq):
            hout_ref[j, pl_, :] = h1s[j]
        y_cols = [jnp.sum(h1s[j] * c_rows[j:j + 1, :], axis=-1, keepdims=True) for j in range(nseq)]
        y_t = jnp.zeros((PAIR, LANES), F32)
        for j in range(nseq):
            y_t = jnp.where(lane == j, y_cols[j], y_t)
        ys.append(y_t.T[0:nseq, :])
    y = jnp.concatenate(ys, axis=1)
    y_ref[...] = _gate_norm(y, xs, zm_ref[...], dsk_ref[...], gn_ref[...])


def _ssd_step(xbc, zm, dt, lp, conv_all, ssm_all, layer, prev_conv, prev_ssm):
    m = xbc.shape[0]
    nseq = 8
    k = _ssd_consts(lp)
    full = lambda shape: pl.BlockSpec(shape, lambda i: tuple(0 for _ in shape))
    rows = lambda w: pl.BlockSpec((nseq, w), lambda i: (i, 0))
    cst = pl.BlockSpec((None, nseq, CONV_W - 1, CONV_DIM), lambda i: (layer, i, 0, 0))
    hst = pl.BlockSpec((None, nseq, D_MODEL, D_STATE), lambda i: (layer, i, 0, 0))
    ins = [xbc, conv_all, zm, dt, k["cw"], k["cb"], k["dtb"], k["alog"], k["dsk"], k["gn"], k["e"], ssm_all]
    in_specs = [rows(CONV_DIM), cst, rows(D_MODEL), rows(DT_PAD), full((CONV_W, CONV_DIM)),
                full((1, CONV_DIM)), full((1, DT_PAD)), full((1, DT_PAD)), full((1, D_MODEL)),
                full((1, D_MODEL)), full((LANES, D_MODEL)), hst]
    aliases = {}
    aliased = prev_conv is not None
    if aliased:
        ins += [prev_conv, prev_ssm]
        in_specs += [pl.BlockSpec(memory_space=pl.ANY)] * 2
        aliases = {len(ins) - 2: 1, len(ins) - 1: 2}
    return pl.pallas_call(
        functools.partial(_ssd_step_kernel, nseq=nseq, aliased=aliased),
        grid=(m // nseq,),
        in_specs=in_specs,
        out_specs=[rows(D_MODEL), cst, hst],
        out_shape=[jax.ShapeDtypeStruct((m, D_MODEL), F32), jax.ShapeDtypeStruct(conv_all.shape, F32),
                   jax.ShapeDtypeStruct(ssm_all.shape, F32)],
        input_output_aliases=aliases,
        compiler_params=_cparams(("arbitrary",)),
        name="ssd_step",
    )(*ins)


def _out_kernel(*refs, final):
    if final:
        x_ref, yr_ref, ym_ref, w_ref, fw_ref, o_ref = refs
    else:
        x_ref, yr_ref, ym_ref, w_ref, o_ref = refs
    out = (x_ref[...] + _dot(yr_ref[...].astype(BF16), w_ref[0:D_MODEL, :])
           + _dot(ym_ref[...].astype(BF16), w_ref[D_MODEL:2 * D_MODEL, :]))
    if final:
        out = out * lax.rsqrt(jnp.mean(out * out, axis=-1, keepdims=True) + NORM_EPS) * fw_ref[...]
    o_ref[...] = out


def _out_proj(x, yr, ym, w_bf, final_w, tm):
    m, d = x.shape
    final = final_w is not None
    tok = pl.BlockSpec((tm, d), lambda i: (i, 0))
    ins = [x, yr, ym, w_bf] + ([final_w.reshape(1, d)] if final else [])
    in_specs = [tok, tok, tok, pl.BlockSpec((2 * d, d), lambda i: (0, 0))]
    if final:
        in_specs.append(pl.BlockSpec((1, d), lambda i: (0, 0)))
    return pl.pallas_call(
        functools.partial(_out_kernel, final=final),
        grid=(m // tm,),
        in_specs=in_specs,
        out_specs=tok,
        out_shape=jax.ShapeDtypeStruct((m, d), F32),
        compiler_params=_cparams(("arbitrary",)),
        name="out_proj",
    )(*ins)


def _layer_prompt(x, lp, final_w, *, batch, seq_len):
    tm = 256
    prev = jnp.zeros((batch, 1, R_SHIFT), F32)
    kt, bh, kh, rt, v, gc, z_r, z_m, xbc, dt = _proj_prep(x, prev, lp, seq_len=seq_len, tm=tm)
    xn_last = _rmsnorm(x.reshape(batch, seq_len, D_MODEL)[:, -1], lp["norm_w"])
    y_r, s_bd = _wkv_chunk(kt, bh, kh, rt, v, z_r, gc, lp, None, batch=batch, seq_len=seq_len)
    y_m, conv_new, ssm_new = _ssd_chunk(xbc, z_m, dt, lp, None, None, batch=batch, seq_len=seq_len)
    x_new = _out_proj(x, y_r, y_m, lp["w_out"], final_w, tm)
    return (x_new, xn_last, _from_block_diag(s_bd), conv_new,
            ssm_new.reshape(batch, N_HEADS, HEAD, D_STATE))


def _layer_sample(x, states, layer, prev_outs, lp, final_w):
    m = x.shape[0]
    tm = m
    shift_all, wkv_all, conv_all, ssm_all = states
    (prev,) = _proj(shift_all[layer], None, lp["w_in"], (SEG_SH,), tm)
    kap, b, k2, r, v, w, z_r, z_m, xbc, dt, xn = _proj_prep(x, prev, lp, seq_len=1, tm=tm, emit_xn=True)
    y_r, wkv_out = _wkv_step(kap, b, k2, r, v, w, z_r, lp, wkv_all, layer, prev_outs[0])
    y_m, conv_out, ssm_out = _ssd_step(xbc, z_m, dt, lp, conv_all, ssm_all, layer, prev_outs[1], prev_outs[2])
    x_new = _out_proj(x, y_r, y_m, lp["w_out"], final_w, tm)
    return x_new, xn, (wkv_out, conv_out, ssm_out)


def _layer_params(params, l):
    lp = {k: v[l] for k, v in params.items()}
    lp["w_in"] = _cast_w_in(lp["w_in"].T)
    lp["w_out"] = lp["w_out"].astype(BF16)
    zeros = jnp.zeros((LORA, D_MODEL), F32)
    lp["w2p"] = jnp.concatenate([lp["w_lora2"], zeros], axis=0).astype(BF16)
    lp["a2p"] = jnp.concatenate([zeros, lp["a_lora2"]], axis=0).astype(BF16)
    return lp


def kernel(x_prompt, x_sample, state_shift, state_wkv, state_conv, state_ssm, norm_w, w_in, mu_shift, w0,
           w_lora2, a0, a_lora2, k_k, k_a, r_k, lnx_w, lnx_b, conv_w, conv_b, dt_bias, a_log, d_skip,
           gnorm_w, w_out, final_norm_w):
    params = dict(norm_w=norm_w, w_in=w_in, mu_shift=mu_shift, w0=w0, w_lora2=w_lora2, a0=a0,
                  a_lora2=a_lora2, k_k=k_k, k_a=k_a, r_k=r_k, lnx_w=lnx_w, lnx_b=lnx_b, conv_w=conv_w,
                  conv_b=conv_b, dt_bias=dt_bias, a_log=a_log, d_skip=d_skip, gnorm_w=gnorm_w, w_out=w_out)
    depth = norm_w.shape[0]
    bp, lseq, d = x_prompt.shape
    bs = x_sample.shape[0]
    xp = x_prompt.reshape(bp * lseq, d)
    xs = x_sample.reshape(bs, d)
    sample_states = (state_shift, jnp.transpose(state_wkv, (0, 2, 3, 4, 1)), state_conv,
                     state_ssm.reshape(depth, bs, N_HEADS * HEAD, D_STATE))
    p_states, s_shift = [], []
    s_outs = (None, None, None)
    for l in range(depth):
        lp = _layer_params(params, l)
        fw = final_norm_w if l == depth - 1 else None
        xp, *st = _layer_prompt(xp, lp, fw, batch=bp, seq_len=lseq)
        p_states.append(st)
        xs, xn, s_outs = _layer_sample(xs, sample_states, l, s_outs, lp, fw)
        s_shift.append(xn)
    stack = lambda states, i: jnp.stack([s[i] for s in states])
    return (xp.reshape(bp, lseq, d), xs.reshape(bs, 1, d),
            stack(p_states, 0), stack(p_states, 1), stack(p_states, 2), stack(p_states, 3),
            jnp.stack(s_shift), jnp.transpose(s_outs[0], (0, 4, 1, 2, 3)), s_outs[1],
            s_outs[2].reshape(state_ssm.shape))
```

```python
import functools

import jax
import jax.numpy as jnp
from jax import lax
from jax.experimental import pallas as pl
from jax.experimental.pallas import tpu as pltpu

F32 = jnp.float32
BF16 = jnp.bfloat16

D_MODEL = 1024
HEAD = 64
N_HEADS = 16
LORA = 64
R_SHIFT = 3 * D_MODEL + 2 * LORA
D_STATE = 128
N_GROUPS = 2
CONV_W = 4
CONV_DIM = D_MODEL + 2 * N_GROUPS * D_STATE
DT_PAD = 128
SEG_SH = (0, R_SHIFT)
SEG_ZR = (R_SHIFT, R_SHIFT + D_MODEL)
SEG_ZM = (SEG_ZR[1], SEG_ZR[1] + D_MODEL)
SEG_XBC = (SEG_ZM[1], SEG_ZM[1] + CONV_DIM)
SEG_DT = (SEG_XBC[1], SEG_XBC[1] + DT_PAD)
D_IN_PAD = SEG_DT[1]
NORM_EPS = 1e-5
LNX_EPS = 64e-5
DECAY_SCALE = 0.6065306597126334
LANES = 128
PAIR = 2 * HEAD
N_PAIRS = N_HEADS // 2
WKV_CHUNK = 64
SSD_CHUNK = 128
WKV_SEQS_PER_STEP = 2
VMEM_LIMIT = 56 * 1024 * 1024


def _cparams(sem):
    return pltpu.CompilerParams(dimension_semantics=sem, vmem_limit_bytes=VMEM_LIMIT)


def _dot(a, b):
    return jnp.dot(a, b, preferred_element_type=F32)


def _dot_nt(a, b):
    return lax.dot_general(a, b, (((1,), (1,)), ((), ())), preferred_element_type=F32)


def _dot_tn(a, b):
    return lax.dot_general(a, b, (((0,), (0,)), ((), ())), preferred_element_type=F32)


def _pieces(x, n):
    out, r = [], x
    for _ in range(n):
        p = r.astype(BF16)
        out.append(p)
        r = r - p.astype(F32)
    return out


def _sigmoid(x):
    return 1.0 / (1.0 + jnp.exp(-x))


def _softplus(x):
    return jnp.maximum(x, 0.0) + jnp.log(1.0 + jnp.exp(-jnp.abs(x)))


def _iota(shape, dim):
    return lax.broadcasted_iota(jnp.int32, shape, dim)


def _proj_kernel(*refs, segs, norm, emit_xn):
    refs = list(refs)
    x_ref = refs.pop(0)
    nw_ref = refs.pop(0) if norm else None
    w_ref = refs.pop(0)
    x = x_ref[...]
    if norm:
        xn = x * lax.rsqrt(jnp.mean(x * x, axis=-1, keepdims=True) + NORM_EPS) * nw_ref[...]
    else:
        xn = x
    xb = xn.astype(BF16)
    for (lo, hi), o_ref in zip(segs, refs):
        o_ref[...] = _dot(xb, w_ref[:, lo:hi])
    if emit_xn:
        refs[len(segs)][...] = xn


def _proj(x, norm_w, w_bf, segs, tm, emit_xn=False):
    m, d = x.shape
    n = w_bf.shape[1]
    norm = norm_w is not None
    ins = [x] + ([norm_w.reshape(1, d)] if norm else []) + [w_bf]
    in_specs = [pl.BlockSpec((tm, d), lambda i: (i, 0))]
    if norm:
        in_specs.append(pl.BlockSpec((1, d), lambda i: (0, 0)))
    in_specs.append(pl.BlockSpec((d, n), lambda i: (0, 0)))
    widths = [hi - lo for lo, hi in segs] + ([d] if emit_xn else [])
    return pl.pallas_call(
        functools.partial(_proj_kernel, segs=tuple(segs), norm=norm, emit_xn=emit_xn),
        grid=(m // tm,),
        in_specs=in_specs,
        out_specs=[pl.BlockSpec((tm, w), lambda i: (i, 0)) for w in widths],
        out_shape=[jax.ShapeDtypeStruct((m, w), F32) for w in widths],
        compiler_params=_cparams(("arbitrary",)),
        name="norm_proj",
    )(*ins)


def _cast_w_kernel(wt_ref, o_ref, *, n_valid):
    rows = _iota(wt_ref.shape, 0) + pl.program_id(0) * wt_ref.shape[0]
    w_t = jnp.where(rows < n_valid, wt_ref[...], 0.0)
    o_ref[...] = w_t.T.astype(BF16)


def _cast_w_in(w_t_all, layer):
    _, n, d = w_t_all.shape
    tn = 256
    return pl.pallas_call(
        functools.partial(_cast_w_kernel, n_valid=n),
        grid=(D_IN_PAD // tn,),
        in_specs=[pl.BlockSpec((None, tn, d), lambda j: (layer, j, 0))],
        out_specs=pl.BlockSpec((d, tn), lambda j: (0, j)),
        out_shape=jax.ShapeDtypeStruct((d, D_IN_PAD), BF16),
        compiler_params=_cparams(("arbitrary",)),
        name="cast_w_in",
    )(w_t_all)


def _rmsnorm_kernel(x_ref, w_ref, o_ref):
    x = x_ref[...]
    o_ref[...] = x * lax.rsqrt(jnp.mean(x * x, axis=-1, keepdims=True) + NORM_EPS) * w_ref[...]


def _rmsnorm(x, w):
    m, d = x.shape
    return pl.pallas_call(
        _rmsnorm_kernel,
        out_shape=jax.ShapeDtypeStruct((m, d), F32),
        name="rmsnorm_rows",
    )(x, w.reshape(1, d))


def _proj_prep_kernel(*refs, tm, chunk, tiles_per_seq, single, emit_xn):
    (x_ref, nw_ref, w_ref, prev_ref, mu_ref, w0_ref, a0_ref, kk_ref, ka_ref, w2_ref, a2_ref, bd_ref) = refs[:12]
    refs = list(refs[12:])
    tri_ref = None if single else refs.pop(0)
    o_a, o_b, o_c, o_d, o_v, o_g, zr_ref, zm_ref, xbc_ref, dt_ref = refs[:10]
    refs = refs[10:]
    xn_ref = refs.pop(0) if emit_xn else None
    carry_ref = None if single else refs.pop(0)

    x = x_ref[...]
    xn = x * lax.rsqrt(jnp.mean(x * x, axis=-1, keepdims=True) + NORM_EPS) * nw_ref[...]
    if emit_xn:
        xn_ref[...] = xn
    xb = xn.astype(BF16)
    seg = lambda lo, hi: _dot(xb, w_ref[:, lo:hi])
    if not single:
        first_tile = pl.program_id(0) % tiles_per_seq == 0

        @pl.when(first_tile)
        def _():
            carry_ref[...] = jnp.broadcast_to(prev_ref[0], carry_ref.shape)

    def mixed(lo, hi):
        p = seg(lo, hi)
        if single:
            shifted = prev_ref[:, lo:hi]
        else:
            shifted = jnp.where(_iota(p.shape, 0) == 0, carry_ref[0:1, lo:hi], pltpu.roll(p, 1, 0))
            carry_ref[:, lo:hi] = jnp.broadcast_to(p[tm - 1:tm, :], (carry_ref.shape[0], hi - lo))
        return p + (shifted - p) * mu_ref[:, lo:hi]

    def head_sum(t):
        hi, lo = _pieces(t, 2)
        cols = []
        for j in range(D_MODEL // 256):
            sl = slice(j * 256, (j + 1) * 256)
            cols.append(_dot(hi[:, sl], bd_ref[...]) + _dot(lo[:, sl], bd_ref[...]))
        return jnp.concatenate(cols, axis=1)

    wal = mixed(3 * D_MODEL, R_SHIFT)
    k = mixed(D_MODEL, 2 * D_MODEL)
    zr_ref[...] = seg(*SEG_ZR)
    dw = _dot(jnp.tanh(wal).astype(BF16), w2_ref[...])
    da = _dot(wal.astype(BF16), a2_ref[...])
    zm_ref[...] = seg(*SEG_ZM)
    logw = -DECAY_SCALE * _sigmoid(w0_ref[...] + dw)
    a = _sigmoid(a0_ref[...] + da)
    kk = k * kk_ref[...]
    if not single:
        cum = sum(_dot(tri_ref[...], piece) for piece in _pieces(logw, 2))
    kap = kk / jnp.maximum(jnp.sqrt(head_sum(kk * kk)), 1e-12)
    r = mixed(0, D_MODEL)
    v = mixed(2 * D_MODEL, 3 * D_MODEL)
    xbc_ref[...] = seg(*SEG_XBC)
    dt_ref[...] = seg(*SEG_DT)
    k2 = k * (1.0 + (a - 1.0) * ka_ref[...])
    o_v[...] = v
    if single:
        o_a[...] = kap
        o_b[...] = kap * a
        o_c[...] = k2
        o_d[...] = r
        o_g[...] = jnp.exp(logw)
    else:
        e_inv = jnp.exp(-cum)
        e_cum = jnp.exp(cum)
        o_a[...] = kap * jnp.exp(cum - logw)
        o_b[...] = kap * a * e_inv
        o_c[...] = k2 * e_inv
        o_d[...] = r * e_cum
        for j in range(tm // chunk):
            o_g[j] = e_cum[j * chunk + chunk - 1:j * chunk + chunk, :]


def _proj_prep(x, prev, lp, *, seq_len, tm, emit_xn=False):
    m, d = x.shape
    single = seq_len == 1
    chunk = 1 if single else WKV_CHUNK
    tiles_per_seq = 1 if single else seq_len // tm
    vec = lambda a, n: a.reshape(1, n)
    bd = (jnp.arange(256)[:, None] // HEAD == jnp.arange(256)[None, :] // HEAD).astype(BF16)
    ins = [x, vec(lp["norm_w"], d), lp["w_in"], prev, vec(lp["mu_shift"], R_SHIFT), vec(lp["w0"], D_MODEL),
           vec(lp["a0"], D_MODEL), vec(lp["k_k"], D_MODEL), vec(lp["k_a"], D_MODEL), lp["w2p"], lp["a2p"], bd]
    full = lambda shape: pl.BlockSpec(shape, lambda i: tuple(0 for _ in shape))
    if single:
        prev_spec = pl.BlockSpec((tm, R_SHIFT), lambda i: (i, 0))
    else:
        prev_spec = pl.BlockSpec((1, 1, R_SHIFT), lambda i: (i // tiles_per_seq, 0, 0))
    in_specs = [pl.BlockSpec((tm, d), lambda i: (i, 0)), full((1, d)), full((d, D_IN_PAD)), prev_spec,
                full((1, R_SHIFT)), full((1, D_MODEL)), full((1, D_MODEL)), full((1, D_MODEL)),
                full((1, D_MODEL)), full((2 * LORA, D_MODEL)), full((2 * LORA, D_MODEL)), full((256, 256))]
    scratch = []
    if not single:
        t = jnp.arange(tm)
        tri = ((t[:, None] // chunk == t[None, :] // chunk) & (t[None, :] <= t[:, None])).astype(BF16)
        ins.append(tri)
        in_specs.append(full((tm, tm)))
        scratch.append(pltpu.VMEM((8, R_SHIFT), F32))
    rows = lambda w: pl.BlockSpec((tm, w), lambda i: (i, 0))
    shp = lambda w: jax.ShapeDtypeStruct((m, w), F32)
    if single:
        g_spec, g_shape = rows(D_MODEL), shp(D_MODEL)
    else:
        g_spec = pl.BlockSpec((tm // chunk, 1, D_MODEL), lambda i: (i, 0, 0))
        g_shape = jax.ShapeDtypeStruct((m // chunk, 1, D_MODEL), F32)
    widths = [D_MODEL, D_MODEL, CONV_DIM, DT_PAD] + ([d] if emit_xn else [])
    return pl.pallas_call(
        functools.partial(_proj_prep_kernel, tm=tm, chunk=chunk, tiles_per_seq=tiles_per_seq, single=single,
                          emit_xn=emit_xn),
        grid=(m // tm,),
        in_specs=in_specs,
        out_specs=[rows(D_MODEL)] * 5 + [g_spec] + [rows(w) for w in widths],
        out_shape=[shp(D_MODEL)] * 5 + [g_shape] + [shp(w) for w in widths],
        scratch_shapes=scratch,
        compiler_params=_cparams(("arbitrary",)),
        name="proj_prep",
    )(*ins)


def _stack(x):
    first = _iota(x.shape, 1) < HEAD
    return jnp.concatenate([jnp.where(first, x, 0.0), jnp.where(first, 0.0, x)], axis=0)


def _wkv_post(o_s, rs, ks, vs, rk_row, lw_row, lb_row, z, rows):
    own = (_iota(o_s.shape, 0) >> (rows.bit_length() - 1)) == (_iota(o_s.shape, 1) >> (HEAD.bit_length() - 1))
    mean = jnp.sum(o_s, axis=-1, keepdims=True) * (1.0 / HEAD)
    dev = jnp.where(own, o_s - mean, 0.0)
    var = jnp.sum(dev * dev, axis=-1, keepdims=True) * (1.0 / HEAD)
    normed = dev * lax.rsqrt(var + LNX_EPS)
    bonus = jnp.sum(rs * ks * rk_row, axis=-1, keepdims=True) * vs
    fold = lambda t: t[0:rows] + t[rows:2 * rows]
    y = fold(normed) * lw_row + lb_row + fold(bonus)
    return y * (z * _sigmoid(z))


def _each(f, *lists):
    return [f(*a) for a in zip(*lists)]


def _wkv_pairs_chunk(kt, bh, kh, rt, v, z, g_row, rk_row, lw_row, lb_row, s0, chunk):
    n2 = 2 * chunk
    bf = lambda t: t.astype(BF16)
    xs, bs, ks, rs, vs = (_each(_stack, t) for t in (kt, bh, kh, rt, v))
    xb, bb, kb, rb, vb = (_each(bf, t) for t in (xs, bs, ks, rs, vs))

    row = _iota((n2, n2), 0)
    col = _iota((n2, n2), 1)
    same = (row >> (chunk.bit_length() - 1)) == (col >> (chunk.bit_length() - 1))
    rt_ = row & (chunk - 1)
    ct_ = col & (chunk - 1)
    strict = same & (ct_ < rt_)
    incl = same & (ct_ <= rt_)
    lower = lambda a: jnp.where(strict, a, 0.0)
    lower_d = lambda a: jnp.where(incl, a, 0.0)

    gram = _each(lambda x, r, k, b: _dot_nt(jnp.concatenate([x, r], axis=0), jnp.concatenate([k, b], axis=0)),
                 xb, rb, kb, bb)
    a_kb = _each(lambda g: lower(g[:n2, n2:]), gram)
    a_kr = _each(lambda g: bf(jnp.concatenate([lower(g[:n2, :n2]), lower_d(g[n2:, :n2])], axis=0)), gram)
    a_rb = _each(lambda g: bf(lower_d(g[n2:, n2:])), gram)

    eye = (row == col).astype(F32)
    pair2 = (rt_ | 1) == (ct_ | 1)
    t_inv = _each(lambda a: eye - jnp.where(pair2, a, 0.0), a_kb)
    s = 2
    while s < chunk:
        off = same & ((rt_ & ~(2 * s - 1)) == (ct_ & ~(2 * s - 1))) & ((rt_ & s) != 0) & ((ct_ & s) == 0)
        b_off = _each(lambda a: bf(jnp.where(off, a, 0.0)), a_kb)
        t_b = _each(bf, t_inv)
        bt = _each(lambda b, t: bf(_dot(b, t)), b_off, t_b)
        t_inv = _each(lambda t, tb, x: t - _dot(tb, x), t_inv, t_b, bt)
        s *= 2
    t_b = _each(bf, t_inv)

    xa = _each(_dot, a_kr, vb)
    wu_b = _each(lambda t, x, y: bf(_dot(t, bf(jnp.concatenate([x, y[:n2]], axis=1)))), t_b, xs, xa)
    pq = _each(lambda r, a, arb, wu: jnp.concatenate([r, a[n2:]], axis=1) - _dot(arb, wu), rs, xa, a_rb, wu_b)
    m_low = _each(lambda wu, b, g: bf(_dot_tn(wu[:, :PAIR], b) * g), wu_b, bb, g_row)
    n_t = _each(lambda v_, wu, k, b, g: _dot_tn(jnp.concatenate([v_, -wu[:, PAIR:]], axis=0),
                                                jnp.concatenate([k, b], axis=0)) * g,
                vb, wu_b, kb, bb, g_row)

    s0_b = _each(bf, s0)
    o_s = _each(lambda p, sb: _dot_nt(bf(p[:, :PAIR]), sb) + p[:, PAIR:], pq, s0_b)
    s1 = _each(lambda s_, sb, m, n, g: s_ * g - _dot(sb, m) + n, s0, s0_b, m_low, n_t, g_row)
    y = _each(lambda *a: _wkv_post(*a, chunk), o_s, rs, ks, vs, rk_row, lw_row, lb_row, z)
    return y, s1


def _wkv_chunk_kernel(kt_ref, bh_ref, kh_ref, rt_ref, v_ref, z_ref, gc_ref, rk_ref, lw_ref, lb_ref,
                      y_ref, sout_ref, s_scr, *, chunk, nseq):
    c = pl.program_id(1)

    @pl.when(c == 0)
    def _():
        s_scr[...] = jnp.zeros(s_scr.shape, F32)

    items = [(i, slice(g * PAIR, (g + 1) * PAIR)) for i in range(nseq) for g in range(N_PAIRS)]
    tiles = lambda ref: [ref[i, :, ln] for i, ln in items]
    params = lambda ref: [ref[:, ln] for _, ln in items]
    y, s1 = _wkv_pairs_chunk(tiles(kt_ref), tiles(bh_ref), tiles(kh_ref), tiles(rt_ref), tiles(v_ref),
                             tiles(z_ref), [gc_ref[i, 0, :, ln] for i, ln in items], params(rk_ref),
                             params(lw_ref), params(lb_ref), [s_scr[n] for n in range(len(items))], chunk)
    for n, (i, ln) in enumerate(items):
        y_ref[i, :, ln] = y[n]
        s_scr[n] = s1[n]

    @pl.when(c == pl.num_programs(1) - 1)
    def _():
        sout_ref[...] = s_scr[...].reshape(sout_ref.shape)


def _wkv_chunk(kt, bh, kh, rt, v, z, gc, lp, *, batch, seq_len):
    chunk = WKV_CHUNK
    nseq = WKV_SEQS_PER_STEP
    nc = seq_len // chunk
    seq = lambda a: a.reshape(batch, seq_len, D_MODEL)
    tok = pl.BlockSpec((nseq, chunk, D_MODEL), lambda b, c: (b, c, 0))
    par = pl.BlockSpec((1, D_MODEL), lambda b, c: (0, 0))
    st = pl.BlockSpec((nseq, N_PAIRS, PAIR, PAIR), lambda b, c: (b, 0, 0, 0))
    y, s_bd = pl.pallas_call(
        functools.partial(_wkv_chunk_kernel, chunk=chunk, nseq=nseq),
        grid=(batch // nseq, nc),
        in_specs=[tok] * 6 + [pl.BlockSpec((nseq, 1, 1, D_MODEL), lambda b, c: (b, c, 0, 0)), par, par, par],
        out_specs=[tok, st],
        out_shape=[jax.ShapeDtypeStruct((batch, seq_len, D_MODEL), F32),
                   jax.ShapeDtypeStruct((batch, N_PAIRS, PAIR, PAIR), F32)],
        scratch_shapes=[pltpu.VMEM((nseq * N_PAIRS, PAIR, PAIR), F32)],
        compiler_params=_cparams(("arbitrary", "arbitrary")),
        name="rwkv_chunk",
    )(seq(kt), seq(bh), seq(kh), seq(rt), seq(v), seq(z), gc.reshape(batch, nc, 1, D_MODEL),
      lp["r_k"].reshape(1, D_MODEL), lp["lnx_w"].reshape(1, D_MODEL), lp["lnx_b"].reshape(1, D_MODEL))
    return y.reshape(batch * seq_len, D_MODEL), s_bd


def _head_sum_nat(x):
    first = _iota(x.shape, 1) < HEAD
    s_first = jnp.sum(jnp.where(first, x, 0.0), axis=-1, keepdims=True)
    s_all = jnp.sum(x, axis=-1, keepdims=True)
    return jnp.where(first, s_first, s_all - s_first)


def _cols_of(tile):
    pad = jnp.zeros((LANES - tile.shape[0], LANES), F32)
    return jnp.concatenate([tile, pad], axis=0).T


def _wkv_step_kernel(*refs, aliased):
    (kap_ref, b_ref, k_ref, r_ref, v_ref, w_ref, z_ref, rk_ref, lw_ref, lb_ref, s0_ref) = refs[:11]
    y_ref, sout_ref, vt_scr, ot_scr = refs[12:] if aliased else refs[11:]
    tr = lambda ref: ref[...].T
    kap_t, b_t, k_t, r_t, w_t = tr(kap_ref), tr(b_ref), tr(k_ref), tr(r_ref), tr(w_ref)
    vt_scr[...] = tr(v_ref)

    def body(vi, carry):
        for hh in range(2):
            rows = slice(hh * HEAD, (hh + 1) * HEAD)
            s = s0_ref[hh, vi]
            s_kk = jnp.sum(s * kap_t[rows], axis=0, keepdims=True)
            v_row = vt_scr[pl.ds(hh * HEAD + vi, 1), :]
            s1 = s * w_t[rows] - s_kk * b_t[rows] + v_row * k_t[rows]
            sout_ref[hh, vi] = s1
            ot_scr[pl.ds(hh * HEAD + vi, 1), :] = jnp.sum(s1 * r_t[rows], axis=0, keepdims=True)
        return carry

    lax.fori_loop(0, HEAD, body, 0)
    o = ot_scr[...].T
    dev = o - _head_sum_nat(o) * (1.0 / HEAD)
    var = _head_sum_nat(dev * dev) * (1.0 / HEAD)
    bonus = _head_sum_nat(r_ref[...] * k_ref[...] * rk_ref[...]) * v_ref[...]
    z = z_ref[...]
    y = dev * lax.rsqrt(var + LNX_EPS) * lw_ref[...] + lb_ref[...] + bonus
    y_ref[...] = y * (z * _sigmoid(z))


def _wkv_step(kap, b, k2, r, v, w, z, lp, state_all_t, layer, prev_out):
    m = kap.shape[0]
    tok = pl.BlockSpec((m, PAIR), lambda g: (0, g))
    par = pl.BlockSpec((1, PAIR), lambda g: (0, g))
    st = pl.BlockSpec((None, 2, HEAD, HEAD, m), lambda g: (layer, g, 0, 0, 0))
    ins = [kap, b, k2, r, v, w, z, lp["r_k"].reshape(1, D_MODEL), lp["lnx_w"].reshape(1, D_MODEL),
           lp["lnx_b"].reshape(1, D_MODEL), state_all_t]
    in_specs = [tok] * 7 + [par, par, par, st]
    aliases = {}
    if prev_out is not None:
        ins.append(prev_out)
        in_specs.append(pl.BlockSpec(memory_space=pl.ANY))
        aliases = {len(ins) - 1: 1}
    return pl.pallas_call(
        functools.partial(_wkv_step_kernel, aliased=prev_out is not None),
        grid=(N_PAIRS,),
        in_specs=in_specs,
        out_specs=[tok, st],
        out_shape=[jax.ShapeDtypeStruct((m, D_MODEL), F32), jax.ShapeDtypeStruct(state_all_t.shape, F32)],
        scratch_shapes=[pltpu.VMEM((PAIR, m), F32), pltpu.VMEM((PAIR, m), F32)],
        input_output_aliases=aliases,
        compiler_params=_cparams(("arbitrary",)),
        name="rwkv_step",
    )(*ins)


def _from_block_diag(s):
    b = s.shape[0]
    sp = s.reshape(b, N_PAIRS, 2, HEAD, 2, HEAD)
    return jnp.stack([sp[:, :, 0, :, 0, :], sp[:, :, 1, :, 1, :]], axis=2).reshape(b, N_HEADS, HEAD, HEAD)


def _gate_norm(y, xs, zm, dsk_row, gn_row):
    y = (y + dsk_row * xs) * (zm * _sigmoid(zm))
    half = D_MODEL // N_GROUPS
    outs = []
    for g in range(N_GROUPS):
        yg = y[:, g * half:(g + 1) * half]
        outs.append(yg * lax.rsqrt(jnp.mean(yg * yg, axis=-1, keepdims=True) + NORM_EPS))
    return jnp.concatenate(outs, axis=1) * gn_row


def _ssd_chunk_kernel(*refs, q, has_state):
    (xbc_ref, zm_ref, dt_ref, cw_ref, cb_ref, dtb_ref, alog_ref, dsk_ref, gn_ref, e_ref, et_ref) = refs[:11]
    refs = refs[11:]
    if has_state:
        c0_ref, h0_ref, y_ref, cout_ref, hout_ref, ext, h_scr = refs
    else:
        y_ref, cout_ref, hout_ref, ext, h_scr = refs
    c = pl.program_id(1)

    @pl.when(c == 0)
    def _():
        if has_state:
            ext[0:8, :] = c0_ref[0]
            h_scr[...] = h0_ref[0]
        else:
            ext[0:8, :] = jnp.zeros((8, CONV_DIM), F32)
            h_scr[...] = jnp.zeros(h_scr.shape, F32)

    ext[8:8 + q, :] = xbc_ref[...]
    u = cb_ref[...] + cw_ref[3:4, :] * ext[8:8 + q, :]
    for i in range(CONV_W - 1):
        u = u + cw_ref[i:i + 1, :] * ext[5 + i:5 + i + q, :]
    tail = ext[q:q + 8, :]
    ext[0:8, :] = tail
    u = u * _sigmoid(u)
    xs = u[:, :D_MODEL]
    bm = [u[:, D_MODEL + g * D_STATE:D_MODEL + (g + 1) * D_STATE].astype(BF16) for g in range(N_GROUPS)]
    cm = [u[:, D_MODEL + (N_GROUPS + g) * D_STATE:D_MODEL + (N_GROUPS + g + 1) * D_STATE].astype(BF16)
          for g in range(N_GROUPS)]

    dt = _softplus(dt_ref[...] + dtb_ref[...])
    d_a = dt * (-jnp.exp(alog_ref[...]))
    row = _iota((q, q), 0)
    col = _iota((q, q), 1)
    causal = col <= row
    tril = causal.astype(BF16)
    cs = sum(_dot(tril, piece) for piece in _pieces(d_a, 3))
    eye = (row == col).astype(BF16)
    cs_t = sum(_dot_tn(piece, eye) for piece in _pieces(cs, 3))
    last = cs[q - 1:q, :]
    expand = lambda t: sum(_dot(piece, e_ref[...]) for piece in _pieces(t, 2))
    dt_x = expand(dt)
    dte_x = expand(jnp.exp(last - cs))
    ecs_x = expand(jnp.exp(cs))
    xdt = xs * dt_x
    cd = jnp.exp(cs_t[:, q - 1:q])
    cd_b = jnp.broadcast_to(cd, (LANES, D_STATE))
    decay_full = sum(_dot(et_ref[...], piece) for piece in _pieces(cd_b, 2))

    pairs = list(range(N_PAIRS))
    grp = lambda pi: pi // (N_PAIRS // N_GROUPS)
    rows = lambda pi: slice(pi * PAIR, (pi + 1) * PAIR)
    cb = [_dot_nt(cm[g], bm[g]) for g in range(N_GROUPS)]

    def decay_weighted(pi):
        gs = []
        for hh in (2 * pi, 2 * pi + 1):
            seg = cs[:, hh:hh + 1] - cs_t[hh:hh + 1, :]
            gs.append(cb[grp(pi)] * jnp.exp(jnp.where(causal, seg, -1e30)))
        return jnp.concatenate(gs, axis=1).astype(BF16)

    g_pair = [decay_weighted(pi) for pi in pairs]
    h_in = [h_scr[rows(pi), :] for pi in pairs]
    y_diag = [_dot(g_pair[pi], _stack(xdt[:, rows(pi)]).astype(BF16)) for pi in pairs]
    y_off = [_dot_nt(cm[grp(pi)], h_in[pi].astype(BF16)) for pi in pairs]
    states = [_dot_tn((xdt[:, rows(pi)] * dte_x[:, rows(pi)]).astype(BF16), bm[grp(pi)]) for pi in pairs]
    for pi in pairs:
        h_scr[rows(pi), :] = decay_full[rows(pi), :] * h_in[pi] + states[pi]
    ys = [y_diag[pi] + y_off[pi] * ecs_x[:, rows(pi)] for pi in pairs]
    y = jnp.concatenate(ys, axis=1)
    y_ref[...] = _gate_norm(y, xs, zm_ref[...], dsk_ref[...], gn_ref[...])

    @pl.when(c == pl.num_programs(1) - 1)
    def _():
        cout_ref[0] = ext[8 + q - (CONV_W - 1):8 + q, :]
        hout_ref[0] = h_scr[...]


def _ssd_consts(lp):
    heads = jnp.arange(LANES)[:, None]
    cols = jnp.arange(D_MODEL)[None, :] // HEAD
    e_mat = (heads == cols).astype(BF16)
    pad = lambda a: jnp.pad(a.reshape(1, N_HEADS), ((0, 0), (0, DT_PAD - N_HEADS)))
    return dict(cw=lp["conv_w"], cb=lp["conv_b"].reshape(1, CONV_DIM), dtb=pad(lp["dt_bias"]),
                alog=pad(lp["a_log"]), dsk=jnp.repeat(lp["d_skip"], HEAD).reshape(1, D_MODEL),
                gn=lp["gnorm_w"].reshape(1, D_MODEL), e=e_mat, et=e_mat.T)


def _ssd_chunk(xbc, zm, dt, lp, c0_pad, h0, *, batch, seq_len):
    q = SSD_CHUNK
    nq = seq_len // q
    m = batch * seq_len
    has_state = h0 is not None
    k = _ssd_consts(lp)
    full = lambda shape: pl.BlockSpec(shape, lambda b, c: tuple(0 for _ in shape))
    rows = lambda w: pl.BlockSpec((q, w), lambda b, c: (b * nq + c, 0))
    ins = [xbc, zm, dt, k["cw"], k["cb"], k["dtb"], k["alog"], k["dsk"], k["gn"], k["e"], k["et"]]
    in_specs = [rows(CONV_DIM), rows(D_MODEL), rows(DT_PAD), full((CONV_W, CONV_DIM)), full((1, CONV_DIM)),
                full((1, DT_PAD)), full((1, DT_PAD)), full((1, D_MODEL)), full((1, D_MODEL)),
                full((LANES, D_MODEL)), full((D_MODEL, LANES))]
    if has_state:
        ins += [c0_pad, h0]
        in_specs += [pl.BlockSpec((1, 8, CONV_DIM), lambda b, c: (b, 0, 0)),
                     pl.BlockSpec((1, D_MODEL, D_STATE), lambda b, c: (b, 0, 0))]
    return pl.pallas_call(
        functools.partial(_ssd_chunk_kernel, q=q, has_state=has_state),
        grid=(batch, nq),
        in_specs=in_specs,
        out_specs=[rows(D_MODEL), pl.BlockSpec((1, CONV_W - 1, CONV_DIM), lambda b, c: (b, 0, 0)),
                   pl.BlockSpec((1, D_MODEL, D_STATE), lambda b, c: (b, 0, 0))],
        out_shape=[jax.ShapeDtypeStruct((m, D_MODEL), F32),
                   jax.ShapeDtypeStruct((batch, CONV_W - 1, CONV_DIM), F32),
                   jax.ShapeDtypeStruct((batch, D_MODEL, D_STATE), F32)],
        scratch_shapes=[pltpu.VMEM((q + 8, CONV_DIM), F32), pltpu.VMEM((D_MODEL, D_STATE), F32)],
        compiler_params=_cparams(("arbitrary", "arbitrary")),
        name="ssd_chunk",
    )(*ins)


def _ssd_step_kernel(*refs, nseq, aliased):
    (xbc_ref, c_ref, zm_ref, dt_ref, cw_ref, cb_ref, dtb_ref, alog_ref, dsk_ref, gn_ref, e_ref,
     h0_ref) = refs[:12]
    y_ref, cout_ref, hout_ref = refs[14:17] if aliased else refs[12:15]
    rowid = _iota((nseq, CONV_DIM), 0)
    u = cb_ref[...] + cw_ref[3:4, :] * xbc_ref[...]
    for i in range(CONV_W - 1):
        prev_rows = jnp.zeros((nseq, CONV_DIM), F32)
        for j in range(nseq):
            prev_rows = jnp.where(rowid == j, c_ref[j, i:i + 1, :], prev_rows)
        u = u + cw_ref[i:i + 1, :] * prev_rows
    for j in range(nseq):
        cout_ref[j, 0:CONV_W - 2, :] = c_ref[j, 1:CONV_W - 1, :]
        cout_ref[j, CONV_W - 2:CONV_W - 1, :] = xbc_ref[j:j + 1, :]
    u = u * _sigmoid(u)
    xs = u[:, :D_MODEL]
    dt = _softplus(dt_ref[...] + dtb_ref[...])
    ed = jnp.exp(dt * (-jnp.exp(alog_ref[...])))
    expand = lambda t: sum(_dot(piece, e_ref[...]) for piece in _pieces(t, 2))
    xdt = xs * expand(dt)
    lane = _iota((PAIR, LANES), 1)
    ys = []
    for p in range(N_PAIRS):
        g = p // (N_PAIRS // N_GROUPS)
        pl_ = slice(p * PAIR, (p + 1) * PAIR)
        b_rows = u[:, D_MODEL + g * D_STATE:D_MODEL + (g + 1) * D_STATE]
        c_rows = u[:, D_MODEL + (N_GROUPS + g) * D_STATE:D_MODEL + (N_GROUPS + g + 1) * D_STATE]
        x_cols = _cols_of(xdt[:, pl_])
        upper = _iota((PAIR, D_STATE), 0) < HEAD
        decay = lambda j: jnp.where(upper, ed[j:j + 1, 2 * p:2 * p + 1], ed[j:j + 1, 2 * p + 1:2 * p + 2])
        h1s = [decay(j) * h0_ref[j, pl_, :] + x_cols[:, j:j + 1] * b_rows[j:j + 1, :]
               for j in range(nseq)]
        for j in range(nseq):
            hout_ref[j, pl_, :] = h1s[j]
        y_cols = [jnp.sum(h1s[j] * c_rows[j:j + 1, :], axis=-1, keepdims=True) for j in range(nseq)]
        y_t = jnp.zeros((PAIR, LANES), F32)
        for j in range(nseq):
            y_t = jnp.where(lane == j, y_cols[j], y_t)
        ys.append(y_t.T[0:nseq, :])
    y = jnp.concatenate(ys, axis=1)
    y_ref[...] = _gate_norm(y, xs, zm_ref[...], dsk_ref[...], gn_ref[...])


def _ssd_step(xbc, zm, dt, lp, conv_all, ssm_all, layer, prev_conv, prev_ssm):
    m = xbc.shape[0]
    nseq = 8
    k = _ssd_consts(lp)
    full = lambda shape: pl.BlockSpec(shape, lambda i: tuple(0 for _ in shape))
    rows = lambda w: pl.BlockSpec((nseq, w), lambda i: (i, 0))
    cst = pl.BlockSpec((None, nseq, CONV_W - 1, CONV_DIM), lambda i: (layer, i, 0, 0))
    hst = pl.BlockSpec((None, nseq, D_MODEL, D_STATE), lambda i: (layer, i, 0, 0))
    ins = [xbc, conv_all, zm, dt, k["cw"], k["cb"], k["dtb"], k["alog"], k["dsk"], k["gn"], k["e"], ssm_all]
    in_specs = [rows(CONV_DIM), cst, rows(D_MODEL), rows(DT_PAD), full((CONV_W, CONV_DIM)),
                full((1, CONV_DIM)), full((1, DT_PAD)), full((1, DT_PAD)), full((1, D_MODEL)),
                full((1, D_MODEL)), full((LANES, D_MODEL)), hst]
    aliases = {}
    aliased = prev_conv is not None
    if aliased:
        ins += [prev_conv, prev_ssm]
        in_specs += [pl.BlockSpec(memory_space=pl.ANY)] * 2
        aliases = {len(ins) - 2: 1, len(ins) - 1: 2}
    return pl.pallas_call(
        functools.partial(_ssd_step_kernel, nseq=nseq, aliased=aliased),
        grid=(m // nseq,),
        in_specs=in_specs,
        out_specs=[rows(D_MODEL), cst, hst],
        out_shape=[jax.ShapeDtypeStruct((m, D_MODEL), F32), jax.ShapeDtypeStruct(conv_all.shape, F32),
                   jax.ShapeDtypeStruct(ssm_all.shape, F32)],
        input_output_aliases=aliases,
        compiler_params=_cparams(("arbitrary",)),
        name="ssd_step",
    )(*ins)


def _out_kernel(*refs, final):
    if final:
        x_ref, yr_ref, ym_ref, w_ref, fw_ref, o_ref = refs
    else:
        x_ref, yr_ref, ym_ref, w_ref, o_ref = refs
    out = (x_ref[...] + _dot(yr_ref[...].astype(BF16), w_ref[0:D_MODEL, :])
           + _dot(ym_ref[...].astype(BF16), w_ref[D_MODEL:2 * D_MODEL, :]))
    if final:
        out = out * lax.rsqrt(jnp.mean(out * out, axis=-1, keepdims=True) + NORM_EPS) * fw_ref[...]
    o_ref[...] = out


def _out_proj(x, yr, ym, w_bf, final_w, tm):
    m, d = x.shape
    final = final_w is not None
    tok = pl.BlockSpec((tm, d), lambda i: (i, 0))
    ins = [x, yr, ym, w_bf] + ([final_w.reshape(1, d)] if final else [])
    in_specs = [tok, tok, tok, pl.BlockSpec((2 * d, d), lambda i: (0, 0))]
    if final:
        in_specs.append(pl.BlockSpec((1, d), lambda i: (0, 0)))
    return pl.pallas_call(
        functools.partial(_out_kernel, final=final),
        grid=(m // tm,),
        in_specs=in_specs,
        out_specs=tok,
        out_shape=jax.ShapeDtypeStruct((m, d), F32),
        compiler_params=_cparams(("arbitrary",)),
        name="out_proj",
    )(*ins)


def _layer_prompt(x, lp, final_w, *, batch, seq_len):
    tm = 256
    prev = jnp.zeros((batch, 1, R_SHIFT), F32)
    kt, bh, kh, rt, v, gc, z_r, z_m, xbc, dt = _proj_prep(x, prev, lp, seq_len=seq_len, tm=tm)
    xn_last = _rmsnorm(x.reshape(batch, seq_len, D_MODEL)[:, -1], lp["norm_w"])
    y_r, s_bd = _wkv_chunk(kt, bh, kh, rt, v, z_r, gc, lp, batch=batch, seq_len=seq_len)
    y_m, conv_new, ssm_new = _ssd_chunk(xbc, z_m, dt, lp, None, None, batch=batch, seq_len=seq_len)
    x_new = _out_proj(x, y_r, y_m, lp["w_out"], final_w, 512)
    return (x_new, xn_last, _from_block_diag(s_bd), conv_new,
            ssm_new.reshape(batch, N_HEADS, HEAD, D_STATE))


def _layer_sample(x, states, layer, prev_outs, lp, final_w):
    m = x.shape[0]
    tm = m
    shift_all, wkv_all, conv_all, ssm_all = states
    (prev,) = _proj(shift_all[layer], None, lp["w_in"], (SEG_SH,), tm)
    kap, b, k2, r, v, w, z_r, z_m, xbc, dt, xn = _proj_prep(x, prev, lp, seq_len=1, tm=tm, emit_xn=True)
    y_r, wkv_out = _wkv_step(kap, b, k2, r, v, w, z_r, lp, wkv_all, layer, prev_outs[0])
    y_m, conv_out, ssm_out = _ssd_step(xbc, z_m, dt, lp, conv_all, ssm_all, layer, prev_outs[1], prev_outs[2])
    x_new = _out_proj(x, y_r, y_m, lp["w_out"], final_w, tm)
    return x_new, xn, (wkv_out, conv_out, ssm_out)


def _layer_params(params, l):
    lp = {k: v[l] for k, v in params.items() if k != "w_in"}
    lp["w_in"] = _cast_w_in(jnp.swapaxes(params["w_in"], 1, 2), l)
    lp["w_out"] = lp["w_out"].astype(BF16)
    zeros = jnp.zeros((LORA, D_MODEL), F32)
    lp["w2p"] = jnp.concatenate([lp["w_lora2"], zeros], axis=0).astype(BF16)
    lp["a2p"] = jnp.concatenate([zeros, lp["a_lora2"]], axis=0).astype(BF16)
    return lp


def kernel(x_prompt, x_sample, state_shift, state_wkv, state_conv, state_ssm, norm_w, w_in, mu_shift, w0,
           w_lora2, a0, a_lora2, k_k, k_a, r_k, lnx_w, lnx_b, conv_w, conv_b, dt_bias, a_log, d_skip,
           gnorm_w, w_out, final_norm_w):
    params = dict(norm_w=norm_w, w_in=w_in, mu_shift=mu_shift, w0=w0, w_lora2=w_lora2, a0=a0,
                  a_lora2=a_lora2, k_k=k_k, k_a=k_a, r_k=r_k, lnx_w=lnx_w, lnx_b=lnx_b, conv_w=conv_w,
                  conv_b=conv_b, dt_bias=dt_bias, a_log=a_log, d_skip=d_skip, gnorm_w=gnorm_w, w_out=w_out)
    depth = norm_w.shape[0]
    bp, lseq, d = x_prompt.shape
    bs = x_sample.shape[0]
    xp = x_prompt.reshape(bp * lseq, d)
    xs = x_sample.reshape(bs, d)
    sample_states = (state_shift, jnp.transpose(state_wkv, (0, 2, 3, 4, 1)), state_conv,
                     state_ssm.reshape(depth, bs, N_HEADS * HEAD, D_STATE))
    p_states, s_shift = [], []
    s_outs = (None, None, None)
    for l in range(depth):
        lp = _layer_params(params, l)
        fw = final_norm_w if l == depth - 1 else None
        xp, *st = _layer_prompt(xp, lp, fw, batch=bp, seq_len=lseq)
        p_states.append(st)
        xs, xn, s_outs = _layer_sample(xs, sample_states, l, s_outs, lp, fw)
        s_shift.append(xn)
    stack = lambda states, i: jnp.stack([s[i] for s in states])
    return (xp.reshape(bp, lseq, d), xs.reshape(bs, 1, d),
            stack(p_states, 0), stack(p_states, 1), stack(p_states, 2), stack(p_states, 3),
            jnp.stack(s_shift), jnp.transpose(s_outs[0], (0, 4, 1, 2, 3)), s_outs[1],
            s_outs[2].reshape(state_ssm.shape))
```

```python
import functools

import jax
import jax.numpy as jnp
from jax import lax
from jax.experimental import pallas as pl
from jax.experimental.pallas import tpu as pltpu

F32 = jnp.float32
BF16 = jnp.bfloat16

D_MODEL = 1024
HEAD = 64
N_HEADS = 16
LORA = 64
R_SHIFT = 3 * D_MODEL + 2 * LORA
D_STATE = 128
N_GROUPS = 2
CONV_W = 4
CONV_DIM = D_MODEL + 2 * N_GROUPS * D_STATE
DT_PAD = 128
SEG_SH = (0, R_SHIFT)
SEG_ZR = (R_SHIFT, R_SHIFT + D_MODEL)
SEG_ZM = (SEG_ZR[1], SEG_ZR[1] + D_MODEL)
SEG_XBC = (SEG_ZM[1], SEG_ZM[1] + CONV_DIM)
SEG_DT = (SEG_XBC[1], SEG_XBC[1] + DT_PAD)
D_IN_PAD = SEG_DT[1]
NORM_EPS = 1e-5
LNX_EPS = 64e-5
DECAY_SCALE = 0.6065306597126334
LANES = 128
PAIR = 2 * HEAD
N_PAIRS = N_HEADS // 2
WKV_CHUNK = 64
SSD_CHUNK = 128
WKV_SEQS_PER_STEP = 2
VMEM_LIMIT = 56 * 1024 * 1024


def _cparams(sem):
    return pltpu.CompilerParams(dimension_semantics=sem, vmem_limit_bytes=VMEM_LIMIT)


def _dot(a, b):
    return jnp.dot(a, b, preferred_element_type=F32)


def _dot_nt(a, b):
    return lax.dot_general(a, b, (((1,), (1,)), ((), ())), preferred_element_type=F32)


def _dot_tn(a, b):
    return lax.dot_general(a, b, (((0,), (0,)), ((), ())), preferred_element_type=F32)


def _pieces(x, n):
    out, r = [], x
    for _ in range(n):
        p = r.astype(BF16)
        out.append(p)
        r = r - p.astype(F32)
    return out


def _sigmoid(x):
    return 1.0 / (1.0 + jnp.exp(-x))


def _softplus(x):
    return jnp.maximum(x, 0.0) + jnp.log(1.0 + jnp.exp(-jnp.abs(x)))


def _iota(shape, dim):
    return lax.broadcasted_iota(jnp.int32, shape, dim)


def _proj_kernel(*refs, segs, norm, emit_xn):
    refs = list(refs)
    x_ref = refs.pop(0)
    nw_ref = refs.pop(0) if norm else None
    w_ref = refs.pop(0)
    x = x_ref[...]
    if norm:
        xn = x * lax.rsqrt(jnp.mean(x * x, axis=-1, keepdims=True) + NORM_EPS) * nw_ref[...]
    else:
        xn = x
    xb = xn.astype(BF16)
    for (lo, hi), o_ref in zip(segs, refs):
        o_ref[...] = _dot(xb, w_ref[:, lo:hi])
    if emit_xn:
        refs[len(segs)][...] = xn


def _proj(x, norm_w, w_bf, segs, tm, emit_xn=False):
    m, d = x.shape
    n = w_bf.shape[1]
    norm = norm_w is not None
    ins = [x] + ([norm_w.reshape(1, d)] if norm else []) + [w_bf]
    in_specs = [pl.BlockSpec((tm, d), lambda i: (i, 0))]
    if norm:
        in_specs.append(pl.BlockSpec((1, d), lambda i: (0, 0)))
    in_specs.append(pl.BlockSpec((d, n), lambda i: (0, 0)))
    widths = [hi - lo for lo, hi in segs] + ([d] if emit_xn else [])
    return pl.pallas_call(
        functools.partial(_proj_kernel, segs=tuple(segs), norm=norm, emit_xn=emit_xn),
        grid=(m // tm,),
        in_specs=in_specs,
        out_specs=[pl.BlockSpec((tm, w), lambda i: (i, 0)) for w in widths],
        out_shape=[jax.ShapeDtypeStruct((m, w), F32) for w in widths],
        compiler_params=_cparams(("arbitrary",)),
        name="norm_proj",
    )(*ins)


def _cast_w_kernel(wt_ref, o_ref, *, n_valid):
    rows = _iota(wt_ref.shape, 0) + pl.program_id(0) * wt_ref.shape[0]
    w_t = jnp.where(rows < n_valid, wt_ref[...], 0.0)
    o_ref[...] = w_t.T.astype(BF16)


def _cast_w_in(w_t_all, layer):
    _, n, d = w_t_all.shape
    tn = 256
    return pl.pallas_call(
        functools.partial(_cast_w_kernel, n_valid=n),
        grid=(D_IN_PAD // tn,),
        in_specs=[pl.BlockSpec((None, tn, d), lambda j: (layer, j, 0))],
        out_specs=pl.BlockSpec((d, tn), lambda j: (0, j)),
        out_shape=jax.ShapeDtypeStruct((d, D_IN_PAD), BF16),
        compiler_params=_cparams(("arbitrary",)),
        name="cast_w_in",
    )(w_t_all)


def _rmsnorm_kernel(x_ref, w_ref, o_ref):
    x = x_ref[...]
    o_ref[...] = x * lax.rsqrt(jnp.mean(x * x, axis=-1, keepdims=True) + NORM_EPS) * w_ref[...]


def _rmsnorm(x, w):
    m, d = x.shape
    return pl.pallas_call(
        _rmsnorm_kernel,
        out_shape=jax.ShapeDtypeStruct((m, d), F32),
        name="rmsnorm_rows",
    )(x, w.reshape(1, d))


def _proj_prep_kernel(*refs, tm, chunk, tiles_per_seq, single, emit_xn):
    (x_ref, nw_ref, w_ref, prev_ref, mu_ref, w0_ref, a0_ref, kk_ref, ka_ref, w2_ref, a2_ref, bd_ref) = refs[:12]
    refs = list(refs[12:])
    tri_ref = None if single else refs.pop(0)
    o_a, o_b, o_c, o_d, o_v, o_g, zr_ref, zm_ref, xbc_ref, dt_ref = refs[:10]
    refs = refs[10:]
    xn_ref = refs.pop(0) if emit_xn else None
    carry_ref = None if single else refs.pop(0)

    x = x_ref[...]
    xn = x * lax.rsqrt(jnp.mean(x * x, axis=-1, keepdims=True) + NORM_EPS) * nw_ref[...]
    if emit_xn:
        xn_ref[...] = xn
    xb = xn.astype(BF16)
    seg = lambda lo, hi: _dot(xb, w_ref[:, lo:hi])
    if not single:
        first_tile = pl.program_id(0) % tiles_per_seq == 0

        @pl.when(first_tile)
        def _():
            carry_ref[...] = jnp.broadcast_to(prev_ref[0], carry_ref.shape)

    def mixed(lo, hi):
        p = seg(lo, hi)
        if single:
            shifted = prev_ref[:, lo:hi]
        else:
            shifted = jnp.where(_iota(p.shape, 0) == 0, carry_ref[0:1, lo:hi], pltpu.roll(p, 1, 0))
            carry_ref[:, lo:hi] = jnp.broadcast_to(p[tm - 1:tm, :], (carry_ref.shape[0], hi - lo))
        return p + (shifted - p) * mu_ref[:, lo:hi]

    def head_sum(t):
        hi, lo = _pieces(t, 2)
        cols = []
        for j in range(D_MODEL // 256):
            sl = slice(j * 256, (j + 1) * 256)
            cols.append(_dot(hi[:, sl], bd_ref[...]) + _dot(lo[:, sl], bd_ref[...]))
        return jnp.concatenate(cols, axis=1)

    wal = mixed(3 * D_MODEL, R_SHIFT)
    k = mixed(D_MODEL, 2 * D_MODEL)
    zr_ref[...] = seg(*SEG_ZR)
    dw = _dot(jnp.tanh(wal).astype(BF16), w2_ref[...])
    da = _dot(wal.astype(BF16), a2_ref[...])
    zm_ref[...] = seg(*SEG_ZM)
    logw = -DECAY_SCALE * _sigmoid(w0_ref[...] + dw)
    a = _sigmoid(a0_ref[...] + da)
    kk = k * kk_ref[...]
    if not single:
        cum = sum(_dot(tri_ref[...], piece) for piece in _pieces(logw, 2))
    kap = kk / jnp.maximum(jnp.sqrt(head_sum(kk * kk)), 1e-12)
    r = mixed(0, D_MODEL)
    v = mixed(2 * D_MODEL, 3 * D_MODEL)
    xbc_ref[...] = seg(*SEG_XBC)
    dt_ref[...] = seg(*SEG_DT)
    k2 = k * (1.0 + (a - 1.0) * ka_ref[...])
    o_v[...] = v
    if single:
        o_a[...] = kap
        o_b[...] = kap * a
        o_c[...] = k2
        o_d[...] = r
        o_g[...] = jnp.exp(logw)
    else:
        e_inv = jnp.exp(-cum)
        e_cum = jnp.exp(cum)
        o_a[...] = kap * jnp.exp(cum - logw)
        o_b[...] = kap * a * e_inv
        o_c[...] = k2 * e_inv
        o_d[...] = r * e_cum
        for j in range(tm // chunk):
            o_g[j] = e_cum[j * chunk + chunk - 1:j * chunk + chunk, :]


def _proj_prep(x, prev, lp, *, seq_len, tm, emit_xn=False):
    m, d = x.shape
    single = seq_len == 1
    chunk = 1 if single else WKV_CHUNK
    tiles_per_seq = 1 if single else seq_len // tm
    vec = lambda a, n: a.reshape(1, n)
    bd = (jnp.arange(256)[:, None] // HEAD == jnp.arange(256)[None, :] // HEAD).astype(BF16)
    ins = [x, vec(lp["norm_w"], d), lp["w_in"], prev, vec(lp["mu_shift"], R_SHIFT), vec(lp["w0"], D_MODEL),
           vec(lp["a0"], D_MODEL), vec(lp["k_k"], D_MODEL), vec(lp["k_a"], D_MODEL), lp["w2p"], lp["a2p"], bd]
    full = lambda shape: pl.BlockSpec(shape, lambda i: tuple(0 for _ in shape))
    if single:
        prev_spec = pl.BlockSpec((tm, R_SHIFT), lambda i: (i, 0))
    else:
        prev_spec = pl.BlockSpec((1, 1, R_SHIFT), lambda i: (i // tiles_per_seq, 0, 0))
    in_specs = [pl.BlockSpec((tm, d), lambda i: (i, 0)), full((1, d)), full((d, D_IN_PAD)), prev_spec,
                full((1, R_SHIFT)), full((1, D_MODEL)), full((1, D_MODEL)), full((1, D_MODEL)),
                full((1, D_MODEL)), full((2 * LORA, D_MODEL)), full((2 * LORA, D_MODEL)), full((256, 256))]
    scratch = []
    if not single:
        t = jnp.arange(tm)
        tri = ((t[:, None] // chunk == t[None, :] // chunk) & (t[None, :] <= t[:, None])).astype(BF16)
        ins.append(tri)
        in_specs.append(full((tm, tm)))
        scratch.append(pltpu.VMEM((8, R_SHIFT), F32))
    rows = lambda w: pl.BlockSpec((tm, w), lambda i: (i, 0))
    shp = lambda w: jax.ShapeDtypeStruct((m, w), F32)
    if single:
        g_spec, g_shape = rows(D_MODEL), shp(D_MODEL)
    else:
        g_spec = pl.BlockSpec((tm // chunk, 1, D_MODEL), lambda i: (i, 0, 0))
        g_shape = jax.ShapeDtypeStruct((m // chunk, 1, D_MODEL), F32)
    widths = [D_MODEL, D_MODEL, CONV_DIM, DT_PAD] + ([d] if emit_xn else [])
    return pl.pallas_call(
        functools.partial(_proj_prep_kernel, tm=tm, chunk=chunk, tiles_per_seq=tiles_per_seq, single=single,
                          emit_xn=emit_xn),
        grid=(m // tm,),
        in_specs=in_specs,
        out_specs=[rows(D_MODEL)] * 5 + [g_spec] + [rows(w) for w in widths],
        out_shape=[shp(D_MODEL)] * 5 + [g_shape] + [shp(w) for w in widths],
        scratch_shapes=scratch,
        compiler_params=_cparams(("arbitrary",)),
        name="proj_prep",
    )(*ins)


def _stack(x):
    first = _iota(x.shape, 1) < HEAD
    return jnp.concatenate([jnp.where(first, x, 0.0), jnp.where(first, 0.0, x)], axis=0)


def _head_sum_nat(x):
    first = _iota(x.shape, 1) < HEAD
    s_first = jnp.sum(jnp.where(first, x, 0.0), axis=-1, keepdims=True)
    s_all = jnp.sum(x, axis=-1, keepdims=True)
    return jnp.where(first, s_first, s_all - s_first)


def _wkv_bonus(rs, ks, vs, rk_row):
    return jnp.sum(rs * ks * rk_row, axis=-1, keepdims=True) * vs


def _wkv_post(o_s, bonus, lw_row, lb_row, z, rows):
    fold = lambda t: t[0:rows] + t[rows:2 * rows]
    o = fold(o_s)
    dev = o - _head_sum_nat(o) * (1.0 / HEAD)
    var = _head_sum_nat(dev * dev) * (1.0 / HEAD)
    y = dev * lax.rsqrt(var + LNX_EPS) * lw_row + lb_row + fold(bonus)
    return y * (z * _sigmoid(z))


def _each(f, *lists):
    return [f(*a) for a in zip(*lists)]


def _wkv_pairs_chunk(kt, bh, kh, rt, v, z, g_row, rk_row, lw_row, lb_row, s0, chunk):
    n2 = 2 * chunk
    bf = lambda t: t.astype(BF16)
    xs, bs, ks, rs, vs = (_each(_stack, t) for t in (kt, bh, kh, rt, v))
    xb, bb, kb, rb, vb = (_each(bf, t) for t in (xs, bs, ks, rs, vs))
    bonus = _each(_wkv_bonus, rs, ks, vs, rk_row)

    row = _iota((n2, n2), 0)
    col = _iota((n2, n2), 1)
    same = (row >> (chunk.bit_length() - 1)) == (col >> (chunk.bit_length() - 1))
    rt_ = row & (chunk - 1)
    ct_ = col & (chunk - 1)
    strict = same & (ct_ < rt_)
    incl = same & (ct_ <= rt_)
    lower = lambda a: jnp.where(strict, a, 0.0)
    lower_d = lambda a: jnp.where(incl, a, 0.0)

    gram = _each(lambda x, r, k, b: _dot_nt(jnp.concatenate([x, r], axis=0), jnp.concatenate([k, b], axis=0)),
                 xb, rb, kb, bb)
    a_kb = _each(lambda g: lower(g[:n2, n2:]), gram)
    a_kr = _each(lambda g: bf(jnp.concatenate([lower(g[:n2, :n2]), lower_d(g[n2:, :n2])], axis=0)), gram)
    a_rb = _each(lambda g: bf(lower_d(g[n2:, n2:])), gram)

    eye = (row == col).astype(F32)
    pair2 = (rt_ | 1) == (ct_ | 1)
    t_inv = _each(lambda a: eye - jnp.where(pair2, a, 0.0), a_kb)
    s = 2
    while s < chunk:
        off = same & ((rt_ & ~(2 * s - 1)) == (ct_ & ~(2 * s - 1))) & ((rt_ & s) != 0) & ((ct_ & s) == 0)
        b_off = _each(lambda a: bf(jnp.where(off, a, 0.0)), a_kb)
        t_b = _each(bf, t_inv)
        bt = _each(lambda b, t: bf(_dot(b, t)), b_off, t_b)
        t_inv = _each(lambda t, tb, x: t - _dot(tb, x), t_inv, t_b, bt)
        s *= 2
    t_b = _each(bf, t_inv)

    s0_b = _each(bf, s0)
    xa = _each(_dot, a_kr, vb)
    zs = _each(lambda x, r, sb: _dot_nt(jnp.concatenate([x, r], axis=0), sb), xb, rb, s0_b)
    u_b = _each(lambda t, z_, a: bf(_dot(t, bf(z_[:n2] + a[:n2]))), t_b, zs, xa)
    o_s = _each(lambda z_, a, arb, u: z_[n2:] + a[n2:] - _dot(arb, u), zs, xa, a_rb, u_b)
    n_t = _each(lambda v_, u, k, b: _dot_tn(jnp.concatenate([v_, -u], axis=0), jnp.concatenate([k, b], axis=0)),
                vb, u_b, kb, bb)
    s1 = _each(lambda s_, n, g: (s_ + n) * g, s0, n_t, g_row)
    y = _each(lambda *a: _wkv_post(*a, chunk), o_s, bonus, lw_row, lb_row, z)
    return y, s1


def _wkv_chunk_kernel(kt_ref, bh_ref, kh_ref, rt_ref, v_ref, z_ref, gc_ref, rk_ref, lw_ref, lb_ref,
                      y_ref, sout_ref, s_scr, *, chunk, nseq):
    c = pl.program_id(1)

    @pl.when(c == 0)
    def _():
        s_scr[...] = jnp.zeros(s_scr.shape, F32)

    items = [(i, slice(g * PAIR, (g + 1) * PAIR)) for i in range(nseq) for g in range(N_PAIRS)]
    tiles = lambda ref: [ref[i, :, ln] for i, ln in items]
    params = lambda ref: [ref[:, ln] for _, ln in items]
    y, s1 = _wkv_pairs_chunk(tiles(kt_ref), tiles(bh_ref), tiles(kh_ref), tiles(rt_ref), tiles(v_ref),
                             tiles(z_ref), [gc_ref[i, 0, :, ln] for i, ln in items], params(rk_ref),
                             params(lw_ref), params(lb_ref), [s_scr[n] for n in range(len(items))], chunk)
    for n, (i, ln) in enumerate(items):
        y_ref[i, :, ln] = y[n].astype(y_ref.dtype)
        s_scr[n] = s1[n]

    @pl.when(c == pl.num_programs(1) - 1)
    def _():
        for n in range(len(items)):
            i, g = divmod(n, N_PAIRS)
            sout_ref[i, 2 * g] = s_scr[n, 0:HEAD, 0:HEAD]
            sout_ref[i, 2 * g + 1] = s_scr[n, HEAD:PAIR, HEAD:PAIR]


def _wkv_chunk(kt, bh, kh, rt, v, z, gc, lp, *, batch, seq_len):
    chunk = WKV_CHUNK
    nseq = WKV_SEQS_PER_STEP
    nc = seq_len // chunk
    seq = lambda a: a.reshape(batch, seq_len, D_MODEL)
    tok = pl.BlockSpec((nseq, chunk, D_MODEL), lambda b, c: (b, c, 0))
    par = pl.BlockSpec((1, D_MODEL), lambda b, c: (0, 0))
    st = pl.BlockSpec((nseq, N_HEADS, HEAD, HEAD), lambda b, c: (b, 0, 0, 0))
    y, s_bd = pl.pallas_call(
        functools.partial(_wkv_chunk_kernel, chunk=chunk, nseq=nseq),
        grid=(batch // nseq, nc),
        in_specs=[tok] * 6 + [pl.BlockSpec((nseq, 1, 1, D_MODEL), lambda b, c: (b, c, 0, 0)), par, par, par],
        out_specs=[tok, st],
        out_shape=[jax.ShapeDtypeStruct((batch, seq_len, D_MODEL), BF16),
                   jax.ShapeDtypeStruct((batch, N_HEADS, HEAD, HEAD), F32)],
        scratch_shapes=[pltpu.VMEM((nseq * N_PAIRS, PAIR, PAIR), F32)],
        compiler_params=_cparams(("arbitrary", "arbitrary")),
        name="rwkv_chunk",
    )(seq(kt), seq(bh), seq(kh), seq(rt), seq(v), seq(z), gc.reshape(batch, nc, 1, D_MODEL),
      lp["r_k"].reshape(1, D_MODEL), lp["lnx_w"].reshape(1, D_MODEL), lp["lnx_b"].reshape(1, D_MODEL))
    return y.reshape(batch * seq_len, D_MODEL), s_bd


def _cols_of(tile):
    pad = jnp.zeros((LANES - tile.shape[0], LANES), F32)
    return jnp.concatenate([tile, pad], axis=0).T


def _wkv_step_kernel(*refs, aliased):
    (kap_ref, b_ref, k_ref, r_ref, v_ref, w_ref, z_ref, rk_ref, lw_ref, lb_ref, s0_ref) = refs[:11]
    y_ref, sout_ref, vt_scr, ot_scr = refs[12:] if aliased else refs[11:]
    tr = lambda ref: ref[...].T
    kap_t, b_t, k_t, r_t, w_t = tr(kap_ref), tr(b_ref), tr(k_ref), tr(r_ref), tr(w_ref)
    vt_scr[...] = tr(v_ref)

    def body(vi, carry):
        for hh in range(2):
            rows = slice(hh * HEAD, (hh + 1) * HEAD)
            s = s0_ref[hh, vi]
            s_kk = jnp.sum(s * kap_t[rows], axis=0, keepdims=True)
            v_row = vt_scr[pl.ds(hh * HEAD + vi, 1), :]
            s1 = s * w_t[rows] - s_kk * b_t[rows] + v_row * k_t[rows]
            sout_ref[hh, vi] = s1
            ot_scr[pl.ds(hh * HEAD + vi, 1), :] = jnp.sum(s1 * r_t[rows], axis=0, keepdims=True)
        return carry

    lax.fori_loop(0, HEAD, body, 0)
    o = ot_scr[...].T
    dev = o - _head_sum_nat(o) * (1.0 / HEAD)
    var = _head_sum_nat(dev * dev) * (1.0 / HEAD)
    bonus = _head_sum_nat(r_ref[...] * k_ref[...] * rk_ref[...]) * v_ref[...]
    z = z_ref[...]
    y = dev * lax.rsqrt(var + LNX_EPS) * lw_ref[...] + lb_ref[...] + bonus
    y_ref[...] = (y * (z * _sigmoid(z))).astype(y_ref.dtype)


def _wkv_step(kap, b, k2, r, v, w, z, lp, state_all_t, layer, prev_out):
    m = kap.shape[0]
    tok = pl.BlockSpec((m, PAIR), lambda g: (0, g))
    par = pl.BlockSpec((1, PAIR), lambda g: (0, g))
    st = pl.BlockSpec((None, 2, HEAD, HEAD, m), lambda g: (layer, g, 0, 0, 0))
    ins = [kap, b, k2, r, v, w, z, lp["r_k"].reshape(1, D_MODEL), lp["lnx_w"].reshape(1, D_MODEL),
           lp["lnx_b"].reshape(1, D_MODEL), state_all_t]
    in_specs = [tok] * 7 + [par, par, par, st]
    aliases = {}
    if prev_out is not None:
        ins.append(prev_out)
        in_specs.append(pl.BlockSpec(memory_space=pl.ANY))
        aliases = {len(ins) - 1: 1}
    return pl.pallas_call(
        functools.partial(_wkv_step_kernel, aliased=prev_out is not None),
        grid=(N_PAIRS,),
        in_specs=in_specs,
        out_specs=[tok, st],
        out_shape=[jax.ShapeDtypeStruct((m, D_MODEL), F32), jax.ShapeDtypeStruct(state_all_t.shape, F32)],
        scratch_shapes=[pltpu.VMEM((PAIR, m), F32), pltpu.VMEM((PAIR, m), F32)],
        input_output_aliases=aliases,
        compiler_params=_cparams(("arbitrary",)),
        name="rwkv_step",
    )(*ins)


def _gate_norm(y, xs, zm, dsk_row, gn_row):
    y = (y + dsk_row * xs) * (zm * _sigmoid(zm))
    half = D_MODEL // N_GROUPS
    outs = []
    for g in range(N_GROUPS):
        yg = y[:, g * half:(g + 1) * half]
        outs.append(yg * lax.rsqrt(jnp.mean(yg * yg, axis=-1, keepdims=True) + NORM_EPS))
    return jnp.concatenate(outs, axis=1) * gn_row


def _ssd_chunk_kernel(*refs, q, has_state):
    (xbc_ref, zm_ref, dt_ref, cw_ref, cb_ref, dtb_ref, alog_ref, dsk_ref, gn_ref, e_ref, et_ref) = refs[:11]
    refs = refs[11:]
    if has_state:
        c0_ref, h0_ref, y_ref, cout_ref, hout_ref, ext, h_scr = refs
    else:
        y_ref, cout_ref, hout_ref, ext, h_scr = refs
    c = pl.program_id(1)

    @pl.when(c == 0)
    def _():
        if has_state:
            ext[0:8, :] = c0_ref[0]
            h_scr[...] = h0_ref[0]
        else:
            ext[0:8, :] = jnp.zeros((8, CONV_DIM), F32)
            h_scr[...] = jnp.zeros(h_scr.shape, F32)

    ext[8:8 + q, :] = xbc_ref[...]
    u = cb_ref[...] + cw_ref[3:4, :] * ext[8:8 + q, :]
    for i in range(CONV_W - 1):
        u = u + cw_ref[i:i + 1, :] * ext[5 + i:5 + i + q, :]
    tail = ext[q:q + 8, :]
    ext[0:8, :] = tail
    u = u * _sigmoid(u)
    xs = u[:, :D_MODEL]
    bm = [u[:, D_MODEL + g * D_STATE:D_MODEL + (g + 1) * D_STATE].astype(BF16) for g in range(N_GROUPS)]
    cm = [u[:, D_MODEL + (N_GROUPS + g) * D_STATE:D_MODEL + (N_GROUPS + g + 1) * D_STATE].astype(BF16)
          for g in range(N_GROUPS)]

    dt = _softplus(dt_ref[...] + dtb_ref[...])
    d_a = dt * (-jnp.exp(alog_ref[...]))
    row = _iota((q, q), 0)
    col = _iota((q, q), 1)
    causal = col <= row
    tril = causal.astype(BF16)
    cs = sum(_dot(tril, piece) for piece in _pieces(d_a, 3))
    eye = (row == col).astype(BF16)
    cs_t = sum(_dot_tn(piece, eye) for piece in _pieces(cs, 3))
    last = cs[q - 1:q, :]
    expand = lambda t: sum(_dot(piece, e_ref[...]) for piece in _pieces(t, 2))
    dt_x = expand(dt)
    dte_x = expand(jnp.exp(last - cs))
    ecs_x = expand(jnp.exp(cs))
    xdt = xs * dt_x
    cd = jnp.exp(cs_t[:, q - 1:q])
    cd_b = jnp.broadcast_to(cd, (LANES, D_STATE))
    decay_full = sum(_dot(et_ref[...], piece) for piece in _pieces(cd_b, 2))

    pairs = list(range(N_PAIRS))
    grp = lambda pi: pi // (N_PAIRS // N_GROUPS)
    rows = lambda pi: slice(pi * PAIR, (pi + 1) * PAIR)
    cb = [_dot_nt(cm[g], bm[g]) for g in range(N_GROUPS)]

    def decay_weighted(pi):
        gs = []
        for hh in (2 * pi, 2 * pi + 1):
            seg = cs[:, hh:hh + 1] - cs_t[hh:hh + 1, :]
            gs.append(cb[grp(pi)] * jnp.exp(jnp.where(causal, seg, -1e30)))
        return jnp.concatenate(gs, axis=1).astype(BF16)

    g_pair = [decay_weighted(pi) for pi in pairs]
    h_in = [h_scr[rows(pi), :] for pi in pairs]
    y_diag = [_dot(g_pair[pi], _stack(xdt[:, rows(pi)]).astype(BF16)) for pi in pairs]
    y_off = [_dot_nt(cm[grp(pi)], h_in[pi].astype(BF16)) for pi in pairs]
    states = [_dot_tn((xdt[:, rows(pi)] * dte_x[:, rows(pi)]).astype(BF16), bm[grp(pi)]) for pi in pairs]
    for pi in pairs:
        h_scr[rows(pi), :] = decay_full[rows(pi), :] * h_in[pi] + states[pi]
    ys = [y_diag[pi] + y_off[pi] * ecs_x[:, rows(pi)] for pi in pairs]
    y = jnp.concatenate(ys, axis=1)
    y_ref[...] = _gate_norm(y, xs, zm_ref[...], dsk_ref[...], gn_ref[...]).astype(y_ref.dtype)

    @pl.when(c == pl.num_programs(1) - 1)
    def _():
        cout_ref[0] = ext[8 + q - (CONV_W - 1):8 + q, :]
        hout_ref[0] = h_scr[...]


def _ssd_consts(lp):
    heads = jnp.arange(LANES)[:, None]
    cols = jnp.arange(D_MODEL)[None, :] // HEAD
    e_mat = (heads == cols).astype(BF16)
    pad = lambda a: jnp.pad(a.reshape(1, N_HEADS), ((0, 0), (0, DT_PAD - N_HEADS)))
    return dict(cw=lp["conv_w"], cb=lp["conv_b"].reshape(1, CONV_DIM), dtb=pad(lp["dt_bias"]),
                alog=pad(lp["a_log"]), dsk=jnp.repeat(lp["d_skip"], HEAD).reshape(1, D_MODEL),
                gn=lp["gnorm_w"].reshape(1, D_MODEL), e=e_mat, et=e_mat.T)


def _ssd_chunk(xbc, zm, dt, lp, c0_pad, h0, *, batch, seq_len):
    q = SSD_CHUNK
    nq = seq_len // q
    m = batch * seq_len
    has_state = h0 is not None
    k = _ssd_consts(lp)
    full = lambda shape: pl.BlockSpec(shape, lambda b, c: tuple(0 for _ in shape))
    rows = lambda w: pl.BlockSpec((q, w), lambda b, c: (b * nq + c, 0))
    ins = [xbc, zm, dt, k["cw"], k["cb"], k["dtb"], k["alog"], k["dsk"], k["gn"], k["e"], k["et"]]
    in_specs = [rows(CONV_DIM), rows(D_MODEL), rows(DT_PAD), full((CONV_W, CONV_DIM)), full((1, CONV_DIM)),
                full((1, DT_PAD)), full((1, DT_PAD)), full((1, D_MODEL)), full((1, D_MODEL)),
                full((LANES, D_MODEL)), full((D_MODEL, LANES))]
    if has_state:
        ins += [c0_pad, h0]
        in_specs += [pl.BlockSpec((1, 8, CONV_DIM), lambda b, c: (b, 0, 0)),
                     pl.BlockSpec((1, D_MODEL, D_STATE), lambda b, c: (b, 0, 0))]
    return pl.pallas_call(
        functools.partial(_ssd_chunk_kernel, q=q, has_state=has_state),
        grid=(batch, nq),
        in_specs=in_specs,
        out_specs=[rows(D_MODEL), pl.BlockSpec((1, CONV_W - 1, CONV_DIM), lambda b, c: (b, 0, 0)),
                   pl.BlockSpec((1, D_MODEL, D_STATE), lambda b, c: (b, 0, 0))],
        out_shape=[jax.ShapeDtypeStruct((m, D_MODEL), BF16),
                   jax.ShapeDtypeStruct((batch, CONV_W - 1, CONV_DIM), F32),
                   jax.ShapeDtypeStruct((batch, D_MODEL, D_STATE), F32)],
        scratch_shapes=[pltpu.VMEM((q + 8, CONV_DIM), F32), pltpu.VMEM((D_MODEL, D_STATE), F32)],
        compiler_params=_cparams(("arbitrary", "arbitrary")),
        name="ssd_chunk",
    )(*ins)


def _ssd_step_kernel(*refs, nseq, aliased):
    (xbc_ref, c_ref, zm_ref, dt_ref, cw_ref, cb_ref, dtb_ref, alog_ref, dsk_ref, gn_ref, e_ref,
     h0_ref) = refs[:12]
    y_ref, cout_ref, hout_ref = refs[14:17] if aliased else refs[12:15]
    rowid = _iota((nseq, CONV_DIM), 0)
    u = cb_ref[...] + cw_ref[3:4, :] * xbc_ref[...]
    for i in range(CONV_W - 1):
        prev_rows = jnp.zeros((nseq, CONV_DIM), F32)
        for j in range(nseq):
            prev_rows = jnp.where(rowid == j, c_ref[j, i:i + 1, :], prev_rows)
        u = u + cw_ref[i:i + 1, :] * prev_rows
    for j in range(nseq):
        cout_ref[j, 0:CONV_W - 2, :] = c_ref[j, 1:CONV_W - 1, :]
        cout_ref[j, CONV_W - 2:CONV_W - 1, :] = xbc_ref[j:j + 1, :]
    u = u * _sigmoid(u)
    xs = u[:, :D_MODEL]
    dt = _softplus(dt_ref[...] + dtb_ref[...])
    ed = jnp.exp(dt * (-jnp.exp(alog_ref[...])))
    expand = lambda t: sum(_dot(piece, e_ref[...]) for piece in _pieces(t, 2))
    xdt = xs * expand(dt)
    lane = _iota((PAIR, LANES), 1)
    ys = []
    for p in range(N_PAIRS):
        g = p // (N_PAIRS // N_GROUPS)
        pl_ = slice(p * PAIR, (p + 1) * PAIR)
        b_rows = u[:, D_MODEL + g * D_STATE:D_MODEL + (g + 1) * D_STATE]
        c_rows = u[:, D_MODEL + (N_GROUPS + g) * D_STATE:D_MODEL + (N_GROUPS + g + 1) * D_STATE]
        x_cols = _cols_of(xdt[:, pl_])
        upper = _iota((PAIR, D_STATE), 0) < HEAD
        decay = lambda j: jnp.where(upper, ed[j:j + 1, 2 * p:2 * p + 1], ed[j:j + 1, 2 * p + 1:2 * p + 2])
        h1s = [decay(j) * h0_ref[j, pl_, :] + x_cols[:, j:j + 1] * b_rows[j:j + 1, :]
               for j in range(nseq)]
        for j in range(nseq):
            hout_ref[j, pl_, :] = h1s[j]
        y_cols = [jnp.sum(h1s[j] * c_rows[j:j + 1, :], axis=-1, keepdims=True) for j in range(nseq)]
        y_t = jnp.zeros((PAIR, LANES), F32)
        for j in range(nseq):
            y_t = jnp.where(lane == j, y_cols[j], y_t)
        ys.append(y_t.T[0:nseq, :])
    y = jnp.concatenate(ys, axis=1)
    y_ref[...] = _gate_norm(y, xs, zm_ref[...], dsk_ref[...], gn_ref[...]).astype(y_ref.dtype)


def _ssd_step(xbc, zm, dt, lp, conv_all, ssm_all, layer, prev_conv, prev_ssm):
    m = xbc.shape[0]
    nseq = 8
    k = _ssd_consts(lp)
    full = lambda shape: pl.BlockSpec(shape, lambda i: tuple(0 for _ in shape))
    rows = lambda w: pl.BlockSpec((nseq, w), lambda i: (i, 0))
    cst = pl.BlockSpec((None, nseq, CONV_W - 1, CONV_DIM), lambda i: (layer, i, 0, 0))
    hst = pl.BlockSpec((None, nseq, D_MODEL, D_STATE), lambda i: (layer, i, 0, 0))
    ins = [xbc, conv_all, zm, dt, k["cw"], k["cb"], k["dtb"], k["alog"], k["dsk"], k["gn"], k["e"], ssm_all]
    in_specs = [rows(CONV_DIM), cst, rows(D_MODEL), rows(DT_PAD), full((CONV_W, CONV_DIM)),
                full((1, CONV_DIM)), full((1, DT_PAD)), full((1, DT_PAD)), full((1, D_MODEL)),
                full((1, D_MODEL)), full((LANES, D_MODEL)), hst]
    aliases = {}
    aliased = prev_conv is not None
    if aliased:
        ins += [prev_conv, prev_ssm]
        in_specs += [pl.BlockSpec(memory_space=pl.ANY)] * 2
        aliases = {len(ins) - 2: 1, len(ins) - 1: 2}
    return pl.pallas_call(
        functools.partial(_ssd_step_kernel, nseq=nseq, aliased=aliased),
        grid=(m // nseq,),
        in_specs=in_specs,
        out_specs=[rows(D_MODEL), cst, hst],
        out_shape=[jax.ShapeDtypeStruct((m, D_MODEL), F32), jax.ShapeDtypeStruct(conv_all.shape, F32),
                   jax.ShapeDtypeStruct(ssm_all.shape, F32)],
        input_output_aliases=aliases,
        compiler_params=_cparams(("arbitrary",)),
        name="ssd_step",
    )(*ins)


def _out_kernel(*refs, final):
    if final:
        x_ref, yr_ref, ym_ref, w_ref, fw_ref, o_ref = refs
    else:
        x_ref, yr_ref, ym_ref, w_ref, o_ref = refs
    out = (x_ref[...] + _dot(yr_ref[...].astype(BF16), w_ref[0:D_MODEL, :])
           + _dot(ym_ref[...].astype(BF16), w_ref[D_MODEL:2 * D_MODEL, :]))
    if final:
        out = out * lax.rsqrt(jnp.mean(out * out, axis=-1, keepdims=True) + NORM_EPS) * fw_ref[...]
    o_ref[...] = out


def _out_proj(x, yr, ym, w_bf, final_w, tm):
    m, d = x.shape
    final = final_w is not None
    tok = pl.BlockSpec((tm, d), lambda i: (i, 0))
    ins = [x, yr, ym, w_bf] + ([final_w.reshape(1, d)] if final else [])
    in_specs = [tok, tok, tok, pl.BlockSpec((2 * d, d), lambda i: (0, 0))]
    if final:
        in_specs.append(pl.BlockSpec((1, d), lambda i: (0, 0)))
    return pl.pallas_call(
        functools.partial(_out_kernel, final=final),
        grid=(m // tm,),
        in_specs=in_specs,
        out_specs=tok,
        out_shape=jax.ShapeDtypeStruct((m, d), F32),
        compiler_params=_cparams(("arbitrary",)),
        name="out_proj",
    )(*ins)


def _layer_prompt(x, lp, final_w, *, batch, seq_len):
    tm = 256
    prev = jnp.zeros((batch, 1, R_SHIFT), F32)
    kt, bh, kh, rt, v, gc, z_r, z_m, xbc, dt = _proj_prep(x, prev, lp, seq_len=seq_len, tm=tm)
    xn_last = _rmsnorm(x.reshape(batch, seq_len, D_MODEL)[:, -1], lp["norm_w"])
    y_r, s_bd = _wkv_chunk(kt, bh, kh, rt, v, z_r, gc, lp, batch=batch, seq_len=seq_len)
    y_m, conv_new, ssm_new = _ssd_chunk(xbc, z_m, dt, lp, None, None, batch=batch, seq_len=seq_len)
    x_new = _out_proj(x, y_r, y_m, lp["w_out"], final_w, 512)
    return (x_new, xn_last, s_bd, conv_new,
            ssm_new.reshape(batch, N_HEADS, HEAD, D_STATE))


def _layer_sample(x, states, layer, prev_outs, lp, final_w):
    m = x.shape[0]
    tm = m
    shift_all, wkv_all, conv_all, ssm_all = states
    (prev,) = _proj(shift_all[layer], None, lp["w_in"], (SEG_SH,), tm)
    kap, b, k2, r, v, w, z_r, z_m, xbc, dt, xn = _proj_prep(x, prev, lp, seq_len=1, tm=tm, emit_xn=True)
    y_r, wkv_out = _wkv_step(kap, b, k2, r, v, w, z_r, lp, wkv_all, layer, prev_outs[0])
    y_m, conv_out, ssm_out = _ssd_step(xbc, z_m, dt, lp, conv_all, ssm_all, layer, prev_outs[1], prev_outs[2])
    x_new = _out_proj(x, y_r, y_m, lp["w_out"], final_w, tm)
    return x_new, xn, (wkv_out, conv_out, ssm_out)


def _layer_params(params, l):
    lp = {k: v[l] for k, v in params.items() if k != "w_in"}
    lp["w_in"] = _cast_w_in(jnp.swapaxes(params["w_in"], 1, 2), l)
    lp["w_out"] = lp["w_out"].astype(BF16)
    zeros = jnp.zeros((LORA, D_MODEL), F32)
    lp["w2p"] = jnp.concatenate([lp["w_lora2"], zeros], axis=0).astype(BF16)
    lp["a2p"] = jnp.concatenate([zeros, lp["a_lora2"]], axis=0).astype(BF16)
    return lp


def kernel(x_prompt, x_sample, state_shift, state_wkv, state_conv, state_ssm, norm_w, w_in, mu_shift, w0,
           w_lora2, a0, a_lora2, k_k, k_a, r_k, lnx_w, lnx_b, conv_w, conv_b, dt_bias, a_log, d_skip,
           gnorm_w, w_out, final_norm_w):
    params = dict(norm_w=norm_w, w_in=w_in, mu_shift=mu_shift, w0=w0, w_lora2=w_lora2, a0=a0,
                  a_lora2=a_lora2, k_k=k_k, k_a=k_a, r_k=r_k, lnx_w=lnx_w, lnx_b=lnx_b, conv_w=conv_w,
                  conv_b=conv_b, dt_bias=dt_bias, a_log=a_log, d_skip=d_skip, gnorm_w=gnorm_w, w_out=w_out)
    depth = norm_w.shape[0]
    bp, lseq, d = x_prompt.shape
    bs = x_sample.shape[0]
    xp = x_prompt.reshape(bp * lseq, d)
    xs = x_sample.reshape(bs, d)
    sample_states = (state_shift, jnp.transpose(state_wkv, (0, 2, 3, 4, 1)), state_conv,
                     state_ssm.reshape(depth, bs, N_HEADS * HEAD, D_STATE))
    p_states, s_shift = [], []
    s_outs = (None, None, None)
    for l in range(depth):
        lp = _layer_params(params, l)
        fw = final_norm_w if l == depth - 1 else None
        xp, *st = _layer_prompt(xp, lp, fw, batch=bp, seq_len=lseq)
        p_states.append(st)
        xs, xn, s_outs = _layer_sample(xs, sample_states, l, s_outs, lp, fw)
        s_shift.append(xn)
    stack = lambda states, i: jnp.stack([s[i] for s in states])
    return (xp.reshape(bp, lseq, d), xs.reshape(bs, 1, d),
            stack(p_states, 0), stack(p_states, 1), stack(p_states, 2), stack(p_states, 3),
            jnp.stack(s_shift), jnp.transpose(s_outs[0], (0, 4, 1, 2, 3)), s_outs[1],
            s_outs[2].reshape(state_ssm.shape))
```

```python
import functools

import jax
import jax.numpy as jnp
from jax import lax
from jax.experimental import pallas as pl
from jax.experimental.pallas import tpu as pltpu

F32 = jnp.float32
BF16 = jnp.bfloat16

D_MODEL = 1024
HEAD = 64
N_HEADS = 16
LORA = 64
R_SHIFT = 3 * D_MODEL + 2 * LORA
D_STATE = 128
N_GROUPS = 2
CONV_W = 4
CONV_DIM = D_MODEL + 2 * N_GROUPS * D_STATE
DT_PAD = 128
SEG_SH = (0, R_SHIFT)
SEG_ZR = (R_SHIFT, R_SHIFT + D_MODEL)
SEG_ZM = (SEG_ZR[1], SEG_ZR[1] + D_MODEL)
SEG_XBC = (SEG_ZM[1], SEG_ZM[1] + CONV_DIM)
SEG_DT = (SEG_XBC[1], SEG_XBC[1] + DT_PAD)
D_IN_PAD = SEG_DT[1]
NORM_EPS = 1e-5
LNX_EPS = 64e-5
DECAY_SCALE = 0.6065306597126334
LANES = 128
PAIR = 2 * HEAD
N_PAIRS = N_HEADS // 2
WKV_CHUNK = 64
SSD_CHUNK = 128
WKV_SEQS_PER_STEP = 2
VMEM_LIMIT = 56 * 1024 * 1024


def _cparams(sem):
    return pltpu.CompilerParams(dimension_semantics=sem, vmem_limit_bytes=VMEM_LIMIT)


def _dot(a, b):
    return jnp.dot(a, b, preferred_element_type=F32)


def _dot_nt(a, b):
    return lax.dot_general(a, b, (((1,), (1,)), ((), ())), preferred_element_type=F32)


def _dot_tn(a, b):
    return lax.dot_general(a, b, (((0,), (0,)), ((), ())), preferred_element_type=F32)


def _pieces(x, n):
    out, r = [], x
    for _ in range(n):
        p = r.astype(BF16)
        out.append(p)
        r = r - p.astype(F32)
    return out


def _sigmoid(x):
    return 1.0 / (1.0 + jnp.exp(-x))


def _softplus(x):
    return jnp.maximum(x, 0.0) + jnp.log(1.0 + jnp.exp(-jnp.abs(x)))


def _iota(shape, dim):
    return lax.broadcasted_iota(jnp.int32, shape, dim)


def _proj_kernel(*refs, segs, norm, emit_xn):
    refs = list(refs)
    x_ref = refs.pop(0)
    nw_ref = refs.pop(0) if norm else None
    w_ref = refs.pop(0)
    x = x_ref[...]
    if norm:
        xn = x * lax.rsqrt(jnp.mean(x * x, axis=-1, keepdims=True) + NORM_EPS) * nw_ref[...]
    else:
        xn = x
    xb = xn.astype(BF16)
    for (lo, hi), o_ref in zip(segs, refs):
        o_ref[...] = _dot(xb, w_ref[:, lo:hi])
    if emit_xn:
        refs[len(segs)][...] = xn


def _proj(x, norm_w, w_bf, segs, tm, emit_xn=False):
    m, d = x.shape
    n = w_bf.shape[1]
    norm = norm_w is not None
    ins = [x] + ([norm_w.reshape(1, d)] if norm else []) + [w_bf]
    in_specs = [pl.BlockSpec((tm, d), lambda i: (i, 0))]
    if norm:
        in_specs.append(pl.BlockSpec((1, d), lambda i: (0, 0)))
    in_specs.append(pl.BlockSpec((d, n), lambda i: (0, 0)))
    widths = [hi - lo for lo, hi in segs] + ([d] if emit_xn else [])
    return pl.pallas_call(
        functools.partial(_proj_kernel, segs=tuple(segs), norm=norm, emit_xn=emit_xn),
        grid=(m // tm,),
        in_specs=in_specs,
        out_specs=[pl.BlockSpec((tm, w), lambda i: (i, 0)) for w in widths],
        out_shape=[jax.ShapeDtypeStruct((m, w), F32) for w in widths],
        compiler_params=_cparams(("arbitrary",)),
        name="norm_proj",
    )(*ins)


def _cast_w_kernel(wt_ref, o_ref, *, n_valid):
    rows = _iota(wt_ref.shape, 0) + pl.program_id(0) * wt_ref.shape[0]
    w_t = jnp.where(rows < n_valid, wt_ref[...], 0.0)
    o_ref[...] = w_t.T.astype(BF16)


def _cast_w_in(w_t_all, layer):
    _, n, d = w_t_all.shape
    tn = 256
    return pl.pallas_call(
        functools.partial(_cast_w_kernel, n_valid=n),
        grid=(D_IN_PAD // tn,),
        in_specs=[pl.BlockSpec((None, tn, d), lambda j: (layer, j, 0))],
        out_specs=pl.BlockSpec((d, tn), lambda j: (0, j)),
        out_shape=jax.ShapeDtypeStruct((d, D_IN_PAD), BF16),
        compiler_params=_cparams(("arbitrary",)),
        name="cast_w_in",
    )(w_t_all)


def _rmsnorm_kernel(x_ref, w_ref, o_ref):
    x = x_ref[...]
    o_ref[...] = x * lax.rsqrt(jnp.mean(x * x, axis=-1, keepdims=True) + NORM_EPS) * w_ref[...]


def _rmsnorm(x, w):
    m, d = x.shape
    return pl.pallas_call(
        _rmsnorm_kernel,
        out_shape=jax.ShapeDtypeStruct((m, d), F32),
        name="rmsnorm_rows",
    )(x, w.reshape(1, d))


def _proj_prep_kernel(*refs, tm, chunk, tiles_per_seq, single, emit_xn):
    (x_ref, nw_ref, w_ref, prev_ref, mu_ref, w0_ref, a0_ref, kk_ref, ka_ref, w2_ref, a2_ref, bd_ref) = refs[:12]
    refs = list(refs[12:])
    tri_ref = None if single else refs.pop(0)
    o_a, o_b, o_c, o_d, o_v, o_g, zr_ref, zm_ref, xbc_ref, dt_ref = refs[:10]
    refs = refs[10:]
    xn_ref = refs.pop(0) if emit_xn else None
    carry_ref = None if single else refs.pop(0)

    x = x_ref[...]
    xn = x * lax.rsqrt(jnp.mean(x * x, axis=-1, keepdims=True) + NORM_EPS) * nw_ref[...]
    if emit_xn:
        xn_ref[...] = xn
    xb = xn.astype(BF16)
    seg = lambda lo, hi: _dot(xb, w_ref[:, lo:hi])
    if not single:
        first_tile = pl.program_id(0) % tiles_per_seq == 0

        @pl.when(first_tile)
        def _():
            carry_ref[...] = jnp.broadcast_to(prev_ref[0], carry_ref.shape)

    def mixed(lo, hi):
        p = seg(lo, hi)
        if single:
            shifted = prev_ref[:, lo:hi]
        else:
            shifted = jnp.where(_iota(p.shape, 0) == 0, carry_ref[0:1, lo:hi], pltpu.roll(p, 1, 0))
            carry_ref[:, lo:hi] = jnp.broadcast_to(p[tm - 1:tm, :], (carry_ref.shape[0], hi - lo))
        return p + (shifted - p) * mu_ref[:, lo:hi]

    def head_sum(t):
        hi, lo = _pieces(t, 2)
        cols = []
        for j in range(D_MODEL // 256):
            sl = slice(j * 256, (j + 1) * 256)
            cols.append(_dot(hi[:, sl], bd_ref[...]) + _dot(lo[:, sl], bd_ref[...]))
        return jnp.concatenate(cols, axis=1)

    wal = mixed(3 * D_MODEL, R_SHIFT)
    k = mixed(D_MODEL, 2 * D_MODEL)
    zr_ref[...] = seg(*SEG_ZR)
    dw = _dot(jnp.tanh(wal).astype(BF16), w2_ref[...])
    da = _dot(wal.astype(BF16), a2_ref[...])
    zm_ref[...] = seg(*SEG_ZM)
    logw = -DECAY_SCALE * _sigmoid(w0_ref[...] + dw)
    a = _sigmoid(a0_ref[...] + da)
    kk = k * kk_ref[...]
    if not single:
        cum = sum(_dot(tri_ref[...], piece) for piece in _pieces(logw, 2))
    kap = kk / jnp.maximum(jnp.sqrt(head_sum(kk * kk)), 1e-12)
    r = mixed(0, D_MODEL)
    v = mixed(2 * D_MODEL, 3 * D_MODEL)
    xbc_ref[...] = seg(*SEG_XBC)
    dt_ref[...] = seg(*SEG_DT)
    k2 = k * (1.0 + (a - 1.0) * ka_ref[...])
    o_v[...] = v
    if single:
        o_a[...] = kap
        o_b[...] = kap * a
        o_c[...] = k2
        o_d[...] = r
        o_g[...] = jnp.exp(logw)
    else:
        e_inv = jnp.exp(-cum)
        e_cum = jnp.exp(cum)
        o_a[...] = kap * jnp.exp(cum - logw)
        o_b[...] = kap * a * e_inv
        o_c[...] = k2 * e_inv
        o_d[...] = r * e_cum
        for j in range(tm // chunk):
            o_g[j] = e_cum[j * chunk + chunk - 1:j * chunk + chunk, :]


def _proj_prep(x, prev, lp, *, seq_len, tm, emit_xn=False):
    m, d = x.shape
    single = seq_len == 1
    chunk = 1 if single else WKV_CHUNK
    tiles_per_seq = 1 if single else seq_len // tm
    vec = lambda a, n: a.reshape(1, n)
    bd = (jnp.arange(256)[:, None] // HEAD == jnp.arange(256)[None, :] // HEAD).astype(BF16)
    ins = [x, vec(lp["norm_w"], d), lp["w_in"], prev, vec(lp["mu_shift"], R_SHIFT), vec(lp["w0"], D_MODEL),
           vec(lp["a0"], D_MODEL), vec(lp["k_k"], D_MODEL), vec(lp["k_a"], D_MODEL), lp["w2p"], lp["a2p"], bd]
    full = lambda shape: pl.BlockSpec(shape, lambda i: tuple(0 for _ in shape))
    if single:
        prev_spec = pl.BlockSpec((tm, R_SHIFT), lambda i: (i, 0))
    else:
        prev_spec = pl.BlockSpec((1, 1, R_SHIFT), lambda i: (i // tiles_per_seq, 0, 0))
    in_specs = [pl.BlockSpec((tm, d), lambda i: (i, 0)), full((1, d)), full((d, D_IN_PAD)), prev_spec,
                full((1, R_SHIFT)), full((1, D_MODEL)), full((1, D_MODEL)), full((1, D_MODEL)),
                full((1, D_MODEL)), full((2 * LORA, D_MODEL)), full((2 * LORA, D_MODEL)), full((256, 256))]
    scratch = []
    if not single:
        t = jnp.arange(tm)
        tri = ((t[:, None] // chunk == t[None, :] // chunk) & (t[None, :] <= t[:, None])).astype(BF16)
        ins.append(tri)
        in_specs.append(full((tm, tm)))
        scratch.append(pltpu.VMEM((8, R_SHIFT), F32))
    rows = lambda w: pl.BlockSpec((tm, w), lambda i: (i, 0))
    shp = lambda w: jax.ShapeDtypeStruct((m, w), F32)
    if single:
        g_spec, g_shape = rows(D_MODEL), shp(D_MODEL)
    else:
        g_spec = pl.BlockSpec((tm // chunk, 1, D_MODEL), lambda i: (i, 0, 0))
        g_shape = jax.ShapeDtypeStruct((m // chunk, 1, D_MODEL), F32)
    widths = [D_MODEL, D_MODEL, CONV_DIM, DT_PAD] + ([d] if emit_xn else [])
    return pl.pallas_call(
        functools.partial(_proj_prep_kernel, tm=tm, chunk=chunk, tiles_per_seq=tiles_per_seq, single=single,
                          emit_xn=emit_xn),
        grid=(m // tm,),
        in_specs=in_specs,
        out_specs=[rows(D_MODEL)] * 5 + [g_spec] + [rows(w) for w in widths],
        out_shape=[shp(D_MODEL)] * 5 + [g_shape] + [shp(w) for w in widths],
        scratch_shapes=scratch,
        compiler_params=_cparams(("arbitrary",)),
        name="proj_prep",
    )(*ins)


def _stack(x):
    first = _iota(x.shape, 1) < HEAD
    return jnp.concatenate([jnp.where(first, x, 0.0), jnp.where(first, 0.0, x)], axis=0)


def _head_sum_nat(x):
    first = _iota(x.shape, 1) < HEAD
    s_first = jnp.sum(jnp.where(first, x, 0.0), axis=-1, keepdims=True)
    s_all = jnp.sum(x, axis=-1, keepdims=True)
    return jnp.where(first, s_first, s_all - s_first)


def _wkv_bonus(rs, ks, vs, rk_row):
    return jnp.sum(rs * ks * rk_row, axis=-1, keepdims=True) * vs


def _wkv_post(o_s, bonus, lw_row, lb_row, z, rows):
    fold = lambda t: t[0:rows] + t[rows:2 * rows]
    o = fold(o_s)
    dev = o - _head_sum_nat(o) * (1.0 / HEAD)
    var = _head_sum_nat(dev * dev) * (1.0 / HEAD)
    y = dev * lax.rsqrt(var + LNX_EPS) * lw_row + lb_row + fold(bonus)
    return y * (z * _sigmoid(z))


def _each(f, *lists):
    return [f(*a) for a in zip(*lists)]


def _wkv_pairs_chunk(kt, bh, kh, rt, v, z, g_row, rk_row, lw_row, lb_row, s0, chunk):
    n2 = 2 * chunk
    bf = lambda t: t.astype(BF16)
    xs, bs, ks, rs, vs = (_each(_stack, t) for t in (kt, bh, kh, rt, v))
    xb, bb, kb, rb, vb = (_each(bf, t) for t in (xs, bs, ks, rs, vs))
    bonus = _each(_wkv_bonus, rs, ks, vs, rk_row)

    row = _iota((n2, n2), 0)
    col = _iota((n2, n2), 1)
    same = (row >> (chunk.bit_length() - 1)) == (col >> (chunk.bit_length() - 1))
    rt_ = row & (chunk - 1)
    ct_ = col & (chunk - 1)
    strict = same & (ct_ < rt_)
    incl = same & (ct_ <= rt_)
    lower = lambda a: jnp.where(strict, a, 0.0)
    lower_d = lambda a: jnp.where(incl, a, 0.0)

    gram = _each(lambda x, r, k, b: _dot_nt(jnp.concatenate([x, r], axis=0), jnp.concatenate([k, b], axis=0)),
                 xb, rb, kb, bb)
    a_kb = _each(lambda g: lower(g[:n2, n2:]), gram)
    a_kr = _each(lambda g: bf(jnp.concatenate([lower(g[:n2, :n2]), lower_d(g[n2:, :n2])], axis=0)), gram)
    a_rb = _each(lambda g: bf(lower_d(g[n2:, n2:])), gram)

    eye = (row == col).astype(F32)
    pair2 = (rt_ | 1) == (ct_ | 1)
    t_inv = _each(lambda a: eye - jnp.where(pair2, a, 0.0), a_kb)
    s = 2
    while s < chunk:
        off = same & ((rt_ & ~(2 * s - 1)) == (ct_ & ~(2 * s - 1))) & ((rt_ & s) != 0) & ((ct_ & s) == 0)
        b_off = _each(lambda a: bf(jnp.where(off, a, 0.0)), a_kb)
        t_b = _each(bf, t_inv)
        bt = _each(lambda b, t: bf(_dot(b, t)), b_off, t_b)
        t_inv = _each(lambda t, tb, x: t - _dot(tb, x), t_inv, t_b, bt)
        s *= 2
    t_b = _each(bf, t_inv)

    s0_b = _each(bf, s0)
    xa = _each(_dot, a_kr, vb)
    zs = _each(lambda x, r, sb: _dot_nt(jnp.concatenate([x, r], axis=0), sb), xb, rb, s0_b)
    u_b = _each(lambda t, z_, a: bf(_dot(t, bf(z_[:n2] + a[:n2]))), t_b, zs, xa)
    o_s = _each(lambda z_, a, arb, u: z_[n2:] + a[n2:] - _dot(arb, u), zs, xa, a_rb, u_b)
    n_t = _each(lambda v_, u, k, b: _dot_tn(jnp.concatenate([v_, -u], axis=0), jnp.concatenate([k, b], axis=0)),
                vb, u_b, kb, bb)
    s1 = _each(lambda s_, n, g: (s_ + n) * g, s0, n_t, g_row)
    y = _each(lambda *a: _wkv_post(*a, chunk), o_s, bonus, lw_row, lb_row, z)
    return y, s1


def _wkv_chunk_kernel(kt_ref, bh_ref, kh_ref, rt_ref, v_ref, z_ref, gc_ref, rk_ref, lw_ref, lb_ref,
                      y_ref, sout_ref, s_scr, *, chunk, nseq):
    c = pl.program_id(1)

    @pl.when(c == 0)
    def _():
        s_scr[...] = jnp.zeros(s_scr.shape, F32)

    items = [(i, slice(g * PAIR, (g + 1) * PAIR)) for i in range(nseq) for g in range(N_PAIRS)]
    tiles = lambda ref: [ref[i, :, ln] for i, ln in items]
    params = lambda ref: [ref[:, ln] for _, ln in items]
    y, s1 = _wkv_pairs_chunk(tiles(kt_ref), tiles(bh_ref), tiles(kh_ref), tiles(rt_ref), tiles(v_ref),
                             tiles(z_ref), [gc_ref[i, 0, :, ln] for i, ln in items], params(rk_ref),
                             params(lw_ref), params(lb_ref), [s_scr[n] for n in range(len(items))], chunk)
    for n, (i, ln) in enumerate(items):
        y_ref[i, :, ln] = y[n].astype(y_ref.dtype)
        s_scr[n] = s1[n]

    @pl.when(c == pl.num_programs(1) - 1)
    def _():
        for n in range(len(items)):
            i, g = divmod(n, N_PAIRS)
            sout_ref[i, 2 * g] = s_scr[n, 0:HEAD, 0:HEAD]
            sout_ref[i, 2 * g + 1] = s_scr[n, HEAD:PAIR, HEAD:PAIR]


def _wkv_chunk(kt, bh, kh, rt, v, z, gc, lp, *, batch, seq_len):
    chunk = WKV_CHUNK
    nseq = WKV_SEQS_PER_STEP
    nc = seq_len // chunk
    seq = lambda a: a.reshape(batch, seq_len, D_MODEL)
    tok = pl.BlockSpec((nseq, chunk, D_MODEL), lambda b, c: (b, c, 0))
    par = pl.BlockSpec((1, D_MODEL), lambda b, c: (0, 0))
    st = pl.BlockSpec((nseq, N_HEADS, HEAD, HEAD), lambda b, c: (b, 0, 0, 0))
    y, s_bd = pl.pallas_call(
        functools.partial(_wkv_chunk_kernel, chunk=chunk, nseq=nseq),
        grid=(batch // nseq, nc),
        in_specs=[tok] * 6 + [pl.BlockSpec((nseq, 1, 1, D_MODEL), lambda b, c: (b, c, 0, 0)), par, par, par],
        out_specs=[tok, st],
        out_shape=[jax.ShapeDtypeStruct((batch, seq_len, D_MODEL), BF16),
                   jax.ShapeDtypeStruct((batch, N_HEADS, HEAD, HEAD), F32)],
        scratch_shapes=[pltpu.VMEM((nseq * N_PAIRS, PAIR, PAIR), F32)],
        compiler_params=_cparams(("arbitrary", "arbitrary")),
        name="rwkv_chunk",
    )(seq(kt), seq(bh), seq(kh), seq(rt), seq(v), seq(z), gc.reshape(batch, nc, 1, D_MODEL),
      lp["r_k"].reshape(1, D_MODEL), lp["lnx_w"].reshape(1, D_MODEL), lp["lnx_b"].reshape(1, D_MODEL))
    return y.reshape(batch * seq_len, D_MODEL), s_bd


def _cols_of(tile):
    pad = jnp.zeros((LANES - tile.shape[0], LANES), F32)
    return jnp.concatenate([tile, pad], axis=0).T


def _wkv_step_kernel(*refs, layer, first):
    (kap_ref, b_ref, k_ref, r_ref, v_ref, w_ref, z_ref, rk_ref, lw_ref, lb_ref, s0_ref) = refs[:11]
    y_ref, sout_all, vt_scr, ot_scr = refs[11:] if first else refs[12:]
    sout_ref = sout_all.at[layer] if first else sout_all
    if first:
        for other in range(sout_all.shape[0]):
            if other != layer:
                sout_all[other] = jnp.zeros(sout_all.shape[1:], F32)
    tr = lambda ref: ref[...].T
    kap_t, b_t, k_t, r_t, w_t = tr(kap_ref), tr(b_ref), tr(k_ref), tr(r_ref), tr(w_ref)
    vt_scr[...] = tr(v_ref)

    def body(vi, carry):
        for hh in range(2):
            rows = slice(hh * HEAD, (hh + 1) * HEAD)
            s = s0_ref[hh, vi]
            s_kk = jnp.sum(s * kap_t[rows], axis=0, keepdims=True)
            v_row = vt_scr[pl.ds(hh * HEAD + vi, 1), :]
            s1 = s * w_t[rows] - s_kk * b_t[rows] + v_row * k_t[rows]
            sout_ref[hh, vi] = s1
            ot_scr[pl.ds(hh * HEAD + vi, 1), :] = jnp.sum(s1 * r_t[rows], axis=0, keepdims=True)
        return carry

    lax.fori_loop(0, HEAD, body, 0)
    o = ot_scr[...].T
    dev = o - _head_sum_nat(o) * (1.0 / HEAD)
    var = _head_sum_nat(dev * dev) * (1.0 / HEAD)
    bonus = _head_sum_nat(r_ref[...] * k_ref[...] * rk_ref[...]) * v_ref[...]
    z = z_ref[...]
    y = dev * lax.rsqrt(var + LNX_EPS) * lw_ref[...] + lb_ref[...] + bonus
    y_ref[...] = (y * (z * _sigmoid(z))).astype(y_ref.dtype)


def _wkv_step(kap, b, k2, r, v, w, z, lp, state_all_t, layer, prev_out):
    m = kap.shape[0]
    tok = pl.BlockSpec((m, PAIR), lambda g: (0, g))
    par = pl.BlockSpec((1, PAIR), lambda g: (0, g))
    depth = state_all_t.shape[0]
    first = prev_out is None
    st = pl.BlockSpec((None, 2, HEAD, HEAD, m), lambda g: (layer, g, 0, 0, 0))
    st_all = pl.BlockSpec((depth, 2, HEAD, HEAD, m), lambda g: (0, g, 0, 0, 0))
    ins = [kap, b, k2, r, v, w, z, lp["r_k"].reshape(1, D_MODEL), lp["lnx_w"].reshape(1, D_MODEL),
           lp["lnx_b"].reshape(1, D_MODEL), state_all_t]
    in_specs = [tok] * 7 + [par, par, par, st]
    aliases = {}
    if not first:
        ins.append(prev_out)
        in_specs.append(pl.BlockSpec(memory_space=pl.ANY))
        aliases = {len(ins) - 1: 1}
    return pl.pallas_call(
        functools.partial(_wkv_step_kernel, layer=layer, first=first),
        grid=(N_PAIRS,),
        in_specs=in_specs,
        out_specs=[tok, st_all if first else st],
        out_shape=[jax.ShapeDtypeStruct((m, D_MODEL), F32), jax.ShapeDtypeStruct(state_all_t.shape, F32)],
        scratch_shapes=[pltpu.VMEM((PAIR, m), F32), pltpu.VMEM((PAIR, m), F32)],
        input_output_aliases=aliases,
        compiler_params=_cparams(("arbitrary",)),
        name="rwkv_step",
    )(*ins)


def _gate_norm(y, xs, zm, dsk_row, gn_row):
    y = (y + dsk_row * xs) * (zm * _sigmoid(zm))
    half = D_MODEL // N_GROUPS
    outs = []
    for g in range(N_GROUPS):
        yg = y[:, g * half:(g + 1) * half]
        outs.append(yg * lax.rsqrt(jnp.mean(yg * yg, axis=-1, keepdims=True) + NORM_EPS))
    return jnp.concatenate(outs, axis=1) * gn_row


def _ssd_chunk_kernel(xbc_ref, zm_ref, dt_ref, cw_ref, cb_ref, dtb_ref, alog_ref, dsk_ref, gn_ref, e_ref, et_ref,
                      y_ref, cout_ref, hout_ref, ext, h_scr, *, q):
    c = pl.program_id(1)

    @pl.when(c == 0)
    def _():
        ext[0:8, :] = jnp.zeros((8, CONV_DIM), F32)
        h_scr[...] = jnp.zeros(h_scr.shape, F32)

    ext[8:8 + q, :] = xbc_ref[...]
    u = cb_ref[...] + cw_ref[3:4, :] * ext[8:8 + q, :]
    for i in range(CONV_W - 1):
        u = u + cw_ref[i:i + 1, :] * ext[5 + i:5 + i + q, :]
    tail = ext[q:q + 8, :]
    ext[0:8, :] = tail
    u = u * _sigmoid(u)
    xs = u[:, :D_MODEL]
    bm = [u[:, D_MODEL + g * D_STATE:D_MODEL + (g + 1) * D_STATE].astype(BF16) for g in range(N_GROUPS)]
    cm = [u[:, D_MODEL + (N_GROUPS + g) * D_STATE:D_MODEL + (N_GROUPS + g + 1) * D_STATE].astype(BF16)
          for g in range(N_GROUPS)]

    dt = _softplus(dt_ref[...] + dtb_ref[...])
    d_a = dt * (-jnp.exp(alog_ref[...]))
    row = _iota((q, q), 0)
    col = _iota((q, q), 1)
    causal = col <= row
    tril = causal.astype(BF16)
    cs = sum(_dot(tril, piece) for piece in _pieces(d_a, 3))
    eye = (row == col).astype(BF16)
    cs_t = sum(_dot_tn(piece, eye) for piece in _pieces(cs, 3))
    last = cs[q - 1:q, :]
    expand = lambda t: sum(_dot(piece, e_ref[...]) for piece in _pieces(t, 2))
    dt_x = expand(dt)
    dte_x = expand(jnp.exp(last - cs))
    ecs_x = expand(jnp.exp(cs))
    xdt = xs * dt_x
    cd = jnp.exp(cs_t[:, q - 1:q])
    cd_b = jnp.broadcast_to(cd, (LANES, D_STATE))
    decay_full = sum(_dot(et_ref[...], piece) for piece in _pieces(cd_b, 2))

    pairs = list(range(N_PAIRS))
    grp = lambda pi: pi // (N_PAIRS // N_GROUPS)
    rows = lambda pi: slice(pi * PAIR, (pi + 1) * PAIR)
    cb = [_dot_nt(cm[g], bm[g]) for g in range(N_GROUPS)]

    def decay_weighted(pi):
        gs = []
        for hh in (2 * pi, 2 * pi + 1):
            seg = cs[:, hh:hh + 1] - cs_t[hh:hh + 1, :]
            gs.append(cb[grp(pi)] * jnp.exp(jnp.where(causal, seg, -1e30)))
        return jnp.concatenate(gs, axis=1).astype(BF16)

    g_pair = [decay_weighted(pi) for pi in pairs]
    h_in = [h_scr[rows(pi), :] for pi in pairs]
    y_diag = [_dot(g_pair[pi], _stack(xdt[:, rows(pi)]).astype(BF16)) for pi in pairs]
    y_off = [_dot_nt(cm[grp(pi)], h_in[pi].astype(BF16)) for pi in pairs]
    states = [_dot_tn((xdt[:, rows(pi)] * dte_x[:, rows(pi)]).astype(BF16), bm[grp(pi)]) for pi in pairs]
    for pi in pairs:
        h_scr[rows(pi), :] = decay_full[rows(pi), :] * h_in[pi] + states[pi]
    ys = [y_diag[pi] + y_off[pi] * ecs_x[:, rows(pi)] for pi in pairs]
    y = jnp.concatenate(ys, axis=1)
    y_ref[...] = _gate_norm(y, xs, zm_ref[...], dsk_ref[...], gn_ref[...]).astype(y_ref.dtype)

    @pl.when(c == pl.num_programs(1) - 1)
    def _():
        cout_ref[0] = ext[8 + q - (CONV_W - 1):8 + q, :]
        hout_ref[0] = h_scr[...]


def _ssd_consts(lp):
    heads = jnp.arange(LANES)[:, None]
    cols = jnp.arange(D_MODEL)[None, :] // HEAD
    e_mat = (heads == cols).astype(BF16)
    pad = lambda a: jnp.pad(a.reshape(1, N_HEADS), ((0, 0), (0, DT_PAD - N_HEADS)))
    return dict(cw=lp["conv_w"], cb=lp["conv_b"].reshape(1, CONV_DIM), dtb=pad(lp["dt_bias"]),
                alog=pad(lp["a_log"]), dsk=jnp.repeat(lp["d_skip"], HEAD).reshape(1, D_MODEL),
                gn=lp["gnorm_w"].reshape(1, D_MODEL), e=e_mat, et=e_mat.T)


def _ssd_chunk(xbc, zm, dt, lp, *, batch, seq_len):
    q = SSD_CHUNK
    nq = seq_len // q
    m = batch * seq_len
    k = _ssd_consts(lp)
    full = lambda shape: pl.BlockSpec(shape, lambda b, c: tuple(0 for _ in shape))
    rows = lambda w: pl.BlockSpec((q, w), lambda b, c: (b * nq + c, 0))
    ins = [xbc, zm, dt, k["cw"], k["cb"], k["dtb"], k["alog"], k["dsk"], k["gn"], k["e"], k["et"]]
    in_specs = [rows(CONV_DIM), rows(D_MODEL), rows(DT_PAD), full((CONV_W, CONV_DIM)), full((1, CONV_DIM)),
                full((1, DT_PAD)), full((1, DT_PAD)), full((1, D_MODEL)), full((1, D_MODEL)),
                full((LANES, D_MODEL)), full((D_MODEL, LANES))]
    return pl.pallas_call(
        functools.partial(_ssd_chunk_kernel, q=q),
        grid=(batch, nq),
        in_specs=in_specs,
        out_specs=[rows(D_MODEL), pl.BlockSpec((1, CONV_W - 1, CONV_DIM), lambda b, c: (b, 0, 0)),
                   pl.BlockSpec((1, D_MODEL, D_STATE), lambda b, c: (b, 0, 0))],
        out_shape=[jax.ShapeDtypeStruct((m, D_MODEL), BF16),
                   jax.ShapeDtypeStruct((batch, CONV_W - 1, CONV_DIM), F32),
                   jax.ShapeDtypeStruct((batch, D_MODEL, D_STATE), F32)],
        scratch_shapes=[pltpu.VMEM((q + 8, CONV_DIM), F32), pltpu.VMEM((D_MODEL, D_STATE), F32)],
        compiler_params=_cparams(("arbitrary", "arbitrary")),
        name="ssd_chunk",
    )(*ins)


def _ssd_step_kernel(*refs, nseq, layer, first):
    (xbc_ref, c_ref, zm_ref, dt_ref, cw_ref, cb_ref, dtb_ref, alog_ref, dsk_ref, gn_ref, e_ref,
     h0_ref) = refs[:12]
    y_ref, cout_all, hout_all = refs[12:15] if first else refs[14:17]
    cout_ref = cout_all.at[layer] if first else cout_all
    hout_ref = hout_all.at[layer] if first else hout_all
    if first:
        for other in range(cout_all.shape[0]):
            if other != layer:
                cout_all[other] = jnp.zeros(cout_all.shape[1:], F32)
                hout_all[other] = jnp.zeros(hout_all.shape[1:], F32)
    rowid = _iota((nseq, CONV_DIM), 0)
    u = cb_ref[...] + cw_ref[3:4, :] * xbc_ref[...]
    for i in range(CONV_W - 1):
        prev_rows = jnp.zeros((nseq, CONV_DIM), F32)
        for j in range(nseq):
            prev_rows = jnp.where(rowid == j, c_ref[j, i:i + 1, :], prev_rows)
        u = u + cw_ref[i:i + 1, :] * prev_rows
    for j in range(nseq):
        cout_ref[j, 0:CONV_W - 2, :] = c_ref[j, 1:CONV_W - 1, :]
        cout_ref[j, CONV_W - 2:CONV_W - 1, :] = xbc_ref[j:j + 1, :]
    u = u * _sigmoid(u)
    xs = u[:, :D_MODEL]
    dt = _softplus(dt_ref[...] + dtb_ref[...])
    ed = jnp.exp(dt * (-jnp.exp(alog_ref[...])))
    expand = lambda t: sum(_dot(piece, e_ref[...]) for piece in _pieces(t, 2))
    xdt = xs * expand(dt)
    lane = _iota((PAIR, LANES), 1)
    ys = []
    for p in range(N_PAIRS):
        g = p // (N_PAIRS // N_GROUPS)
        pl_ = slice(p * PAIR, (p + 1) * PAIR)
        b_rows = u[:, D_MODEL + g * D_STATE:D_MODEL + (g + 1) * D_STATE]
        c_rows = u[:, D_MODEL + (N_GROUPS + g) * D_STATE:D_MODEL + (N_GROUPS + g + 1) * D_STATE]
        x_cols = _cols_of(xdt[:, pl_])
        upper = _iota((PAIR, D_STATE), 0) < HEAD
        decay = lambda j: jnp.where(upper, ed[j:j + 1, 2 * p:2 * p + 1], ed[j:j + 1, 2 * p + 1:2 * p + 2])
        h1s = [decay(j) * h0_ref[j, pl_, :] + x_cols[:, j:j + 1] * b_rows[j:j + 1, :]
               for j in range(nseq)]
        for j in range(nseq):
            hout_ref[j, pl_, :] = h1s[j]
        y_cols = [jnp.sum(h1s[j] * c_rows[j:j + 1, :], axis=-1, keepdims=True) for j in range(nseq)]
        y_t = jnp.zeros((PAIR, LANES), F32)
        for j in range(nseq):
            y_t = jnp.where(lane == j, y_cols[j], y_t)
        ys.append(y_t.T[0:nseq, :])
    y = jnp.concatenate(ys, axis=1)
    y_ref[...] = _gate_norm(y, xs, zm_ref[...], dsk_ref[...], gn_ref[...]).astype(y_ref.dtype)


def _ssd_step(xbc, zm, dt, lp, conv_all, ssm_all, layer, prev_conv, prev_ssm):
    m = xbc.shape[0]
    nseq = 8
    k = _ssd_consts(lp)
    full = lambda shape: pl.BlockSpec(shape, lambda i: tuple(0 for _ in shape))
    rows = lambda w: pl.BlockSpec((nseq, w), lambda i: (i, 0))
    cst = pl.BlockSpec((None, nseq, CONV_W - 1, CONV_DIM), lambda i: (layer, i, 0, 0))
    hst = pl.BlockSpec((None, nseq, D_MODEL, D_STATE), lambda i: (layer, i, 0, 0))
    depth = conv_all.shape[0]
    cst_all = pl.BlockSpec((depth, nseq, CONV_W - 1, CONV_DIM), lambda i: (0, i, 0, 0))
    hst_all = pl.BlockSpec((depth, nseq, D_MODEL, D_STATE), lambda i: (0, i, 0, 0))
    ins = [xbc, conv_all, zm, dt, k["cw"], k["cb"], k["dtb"], k["alog"], k["dsk"], k["gn"], k["e"], ssm_all]
    in_specs = [rows(CONV_DIM), cst, rows(D_MODEL), rows(DT_PAD), full((CONV_W, CONV_DIM)),
                full((1, CONV_DIM)), full((1, DT_PAD)), full((1, DT_PAD)), full((1, D_MODEL)),
                full((1, D_MODEL)), full((LANES, D_MODEL)), hst]
    aliases = {}
    first = prev_conv is None
    if not first:
        ins += [prev_conv, prev_ssm]
        in_specs += [pl.BlockSpec(memory_space=pl.ANY)] * 2
        aliases = {len(ins) - 2: 1, len(ins) - 1: 2}
    return pl.pallas_call(
        functools.partial(_ssd_step_kernel, nseq=nseq, layer=layer, first=first),
        grid=(m // nseq,),
        in_specs=in_specs,
        out_specs=[rows(D_MODEL), cst_all if first else cst, hst_all if first else hst],
        out_shape=[jax.ShapeDtypeStruct((m, D_MODEL), F32), jax.ShapeDtypeStruct(conv_all.shape, F32),
                   jax.ShapeDtypeStruct(ssm_all.shape, F32)],
        input_output_aliases=aliases,
        compiler_params=_cparams(("arbitrary",)),
        name="ssd_step",
    )(*ins)


def _out_kernel(*refs, final):
    if final:
        x_ref, yr_ref, ym_ref, w_ref, fw_ref, o_ref = refs
    else:
        x_ref, yr_ref, ym_ref, w_ref, o_ref = refs
    out = (x_ref[...] + _dot(yr_ref[...].astype(BF16), w_ref[0:D_MODEL, :])
           + _dot(ym_ref[...].astype(BF16), w_ref[D_MODEL:2 * D_MODEL, :]))
    if final:
        out = out * lax.rsqrt(jnp.mean(out * out, axis=-1, keepdims=True) + NORM_EPS) * fw_ref[...]
    o_ref[...] = out


def _out_proj(x, yr, ym, w_bf, final_w, tm):
    m, d = x.shape
    final = final_w is not None
    tok = pl.BlockSpec((tm, d), lambda i: (i, 0))
    ins = [x, yr, ym, w_bf] + ([final_w.reshape(1, d)] if final else [])
    in_specs = [tok, tok, tok, pl.BlockSpec((2 * d, d), lambda i: (0, 0))]
    if final:
        in_specs.append(pl.BlockSpec((1, d), lambda i: (0, 0)))
    return pl.pallas_call(
        functools.partial(_out_kernel, final=final),
        grid=(m // tm,),
        in_specs=in_specs,
        out_specs=tok,
        out_shape=jax.ShapeDtypeStruct((m, d), F32),
        compiler_params=_cparams(("arbitrary",)),
        name="out_proj",
    )(*ins)


def _layer_prompt(x, lp, final_w, *, batch, seq_len):
    tm = 256
    prev = jnp.zeros((batch, 1, R_SHIFT), F32)
    kt, bh, kh, rt, v, gc, z_r, z_m, xbc, dt = _proj_prep(x, prev, lp, seq_len=seq_len, tm=tm)
    xn_last = _rmsnorm(x.reshape(batch, seq_len, D_MODEL)[:, -1], lp["norm_w"])
    y_r, s_bd = _wkv_chunk(kt, bh, kh, rt, v, z_r, gc, lp, batch=batch, seq_len=seq_len)
    y_m, conv_new, ssm_new = _ssd_chunk(xbc, z_m, dt, lp, batch=batch, seq_len=seq_len)
    x_new = _out_proj(x, y_r, y_m, lp["w_out"], final_w, 512)
    return (x_new, xn_last, s_bd, conv_new,
            ssm_new.reshape(batch, N_HEADS, HEAD, D_STATE))


def _layer_sample(x, states, layer, prev_outs, lp, final_w):
    m = x.shape[0]
    tm = m
    shift_all, wkv_all, conv_all, ssm_all = states
    (prev,) = _proj(shift_all[layer], None, lp["w_in"], (SEG_SH,), tm)
    kap, b, k2, r, v, w, z_r, z_m, xbc, dt, xn = _proj_prep(x, prev, lp, seq_len=1, tm=tm, emit_xn=True)
    y_r, wkv_out = _wkv_step(kap, b, k2, r, v, w, z_r, lp, wkv_all, layer, prev_outs[0])
    y_m, conv_out, ssm_out = _ssd_step(xbc, z_m, dt, lp, conv_all, ssm_all, layer, prev_outs[1], prev_outs[2])
    x_new = _out_proj(x, y_r, y_m, lp["w_out"], final_w, tm)
    return x_new, xn, (wkv_out, conv_out, ssm_out)


def _layer_params(params, l):
    lp = {k: v[l] for k, v in params.items() if k != "w_in"}
    lp["w_in"] = _cast_w_in(jnp.swapaxes(params["w_in"], 1, 2), l)
    lp["w_out"] = lp["w_out"].astype(BF16)
    zeros = jnp.zeros((LORA, D_MODEL), F32)
    lp["w2p"] = jnp.concatenate([lp["w_lora2"], zeros], axis=0).astype(BF16)
    lp["a2p"] = jnp.concatenate([zeros, lp["a_lora2"]], axis=0).astype(BF16)
    return lp


def kernel(x_prompt, x_sample, state_shift, state_wkv, state_conv, state_ssm, norm_w, w_in, mu_shift, w0,
           w_lora2, a0, a_lora2, k_k, k_a, r_k, lnx_w, lnx_b, conv_w, conv_b, dt_bias, a_log, d_skip,
           gnorm_w, w_out, final_norm_w):
    params = dict(norm_w=norm_w, w_in=w_in, mu_shift=mu_shift, w0=w0, w_lora2=w_lora2, a0=a0,
                  a_lora2=a_lora2, k_k=k_k, k_a=k_a, r_k=r_k, lnx_w=lnx_w, lnx_b=lnx_b, conv_w=conv_w,
                  conv_b=conv_b, dt_bias=dt_bias, a_log=a_log, d_skip=d_skip, gnorm_w=gnorm_w, w_out=w_out)
    depth = norm_w.shape[0]
    bp, lseq, d = x_prompt.shape
    bs = x_sample.shape[0]
    xp = x_prompt.reshape(bp * lseq, d)
    xs = x_sample.reshape(bs, d)
    sample_states = (state_shift, jnp.transpose(state_wkv, (0, 2, 3, 4, 1)), state_conv,
                     state_ssm.reshape(depth, bs, N_HEADS * HEAD, D_STATE))
    p_states, s_shift = [], []
    s_outs = (None, None, None)
    for l in range(depth):
        lp = _layer_params(params, l)
        fw = final_norm_w if l == depth - 1 else None
        xp, *st = _layer_prompt(xp, lp, fw, batch=bp, seq_len=lseq)
        p_states.append(st)
        xs, xn, s_outs = _layer_sample(xs, sample_states, l, s_outs, lp, fw)
        s_shift.append(xn)
    stack = lambda states, i: jnp.stack([s[i] for s in states])
    return (xp.reshape(bp, lseq, d), xs.reshape(bs, 1, d),
            stack(p_states, 0), stack(p_states, 1), stack(p_states, 2), stack(p_states, 3),
            jnp.stack(s_shift), jnp.transpose(s_outs[0], (0, 4, 1, 2, 3)), s_outs[1],
            s_outs[2].reshape(state_ssm.shape))
```

```python
import functools

import jax
import jax.numpy as jnp
from jax import lax
from jax.experimental import pallas as pl
from jax.experimental.pallas import tpu as pltpu

F32 = jnp.float32
BF16 = jnp.bfloat16

D_MODEL = 1024
HEAD = 64
N_HEADS = 16
LORA = 64
R_SHIFT = 3 * D_MODEL + 2 * LORA
D_STATE = 128
N_GROUPS = 2
CONV_W = 4
CONV_DIM = D_MODEL + 2 * N_GROUPS * D_STATE
DT_PAD = 128
SEG_SH = (0, R_SHIFT)
SEG_ZR = (R_SHIFT, R_SHIFT + D_MODEL)
SEG_ZM = (SEG_ZR[1], SEG_ZR[1] + D_MODEL)
SEG_XBC = (SEG_ZM[1], SEG_ZM[1] + CONV_DIM)
SEG_DT = (SEG_XBC[1], SEG_XBC[1] + DT_PAD)
D_IN_PAD = SEG_DT[1]
NORM_EPS = 1e-5
LNX_EPS = 64e-5
DECAY_SCALE = 0.6065306597126334
LANES = 128
PAIR = 2 * HEAD
N_PAIRS = N_HEADS // 2
WKV_CHUNK = 64
SSD_CHUNK = 128
WKV_SEQS_PER_STEP = 2
VMEM_LIMIT = 56 * 1024 * 1024


def _cparams(sem):
    return pltpu.CompilerParams(dimension_semantics=sem, vmem_limit_bytes=VMEM_LIMIT)


def _dot(a, b):
    return jnp.dot(a, b, preferred_element_type=F32)


def _dot_nt(a, b):
    return lax.dot_general(a, b, (((1,), (1,)), ((), ())), preferred_element_type=F32)


def _dot_tn(a, b):
    return lax.dot_general(a, b, (((0,), (0,)), ((), ())), preferred_element_type=F32)


def _pieces(x, n):
    out, r = [], x
    for _ in range(n):
        p = r.astype(BF16)
        out.append(p)
        r = r - p.astype(F32)
    return out


def _sigmoid(x):
    return 1.0 / (1.0 + jnp.exp(-x))


def _softplus(x):
    return jnp.maximum(x, 0.0) + jnp.log(1.0 + jnp.exp(-jnp.abs(x)))


def _iota(shape, dim):
    return lax.broadcasted_iota(jnp.int32, shape, dim)


def _proj_kernel(*refs, segs, norm, emit_xn):
    refs = list(refs)
    x_ref = refs.pop(0)
    nw_ref = refs.pop(0) if norm else None
    w_ref = refs.pop(0)
    x = x_ref[...]
    if norm:
        xn = x * lax.rsqrt(jnp.mean(x * x, axis=-1, keepdims=True) + NORM_EPS) * nw_ref[...]
    else:
        xn = x
    xb = xn.astype(BF16)
    for (lo, hi), o_ref in zip(segs, refs):
        o_ref[...] = _dot(xb, w_ref[:, lo:hi])
    if emit_xn:
        refs[len(segs)][...] = xn


def _proj(x, norm_w, w_bf, segs, tm, emit_xn=False):
    m, d = x.shape
    n = w_bf.shape[1]
    norm = norm_w is not None
    ins = [x] + ([norm_w.reshape(1, d)] if norm else []) + [w_bf]
    in_specs = [pl.BlockSpec((tm, d), lambda i: (i, 0))]
    if norm:
        in_specs.append(pl.BlockSpec((1, d), lambda i: (0, 0)))
    in_specs.append(pl.BlockSpec((d, n), lambda i: (0, 0)))
    widths = [hi - lo for lo, hi in segs] + ([d] if emit_xn else [])
    return pl.pallas_call(
        functools.partial(_proj_kernel, segs=tuple(segs), norm=norm, emit_xn=emit_xn),
        grid=(m // tm,),
        in_specs=in_specs,
        out_specs=[pl.BlockSpec((tm, w), lambda i: (i, 0)) for w in widths],
        out_shape=[jax.ShapeDtypeStruct((m, w), F32) for w in widths],
        compiler_params=_cparams(("arbitrary",)),
        name="norm_proj",
    )(*ins)


def _cast_w_kernel(wt_ref, o_ref, *, n_valid):
    rows = _iota(wt_ref.shape, 0) + pl.program_id(0) * wt_ref.shape[0]
    w_t = jnp.where(rows < n_valid, wt_ref[...], 0.0)
    o_ref[...] = w_t.T.astype(BF16)


def _cast_w_in(w_t_all, layer):
    _, n, d = w_t_all.shape
    tn = 256
    return pl.pallas_call(
        functools.partial(_cast_w_kernel, n_valid=n),
        grid=(D_IN_PAD // tn,),
        in_specs=[pl.BlockSpec((None, tn, d), lambda j: (layer, j, 0))],
        out_specs=pl.BlockSpec((d, tn), lambda j: (0, j)),
        out_shape=jax.ShapeDtypeStruct((d, D_IN_PAD), BF16),
        compiler_params=_cparams(("arbitrary",)),
        name="cast_w_in",
    )(w_t_all)


def _rmsnorm_kernel(x_ref, w_ref, o_ref):
    x = x_ref[...]
    o_ref[...] = x * lax.rsqrt(jnp.mean(x * x, axis=-1, keepdims=True) + NORM_EPS) * w_ref[...]


def _rmsnorm(x, w):
    m, d = x.shape
    return pl.pallas_call(
        _rmsnorm_kernel,
        out_shape=jax.ShapeDtypeStruct((m, d), F32),
        name="rmsnorm_rows",
    )(x, w.reshape(1, d))


def _proj_prep_kernel(*refs, tm, chunk, tiles_per_seq, single, emit_xn):
    (x_ref, nw_ref, w_ref, prev_ref, mu_ref, w0_ref, a0_ref, kk_ref, ka_ref, w2_ref, a2_ref, bd_ref) = refs[:12]
    refs = list(refs[12:])
    tri_ref = None if single else refs.pop(0)
    o_a, o_b, o_c, o_d, o_v, o_g, zr_ref, zm_ref, xbc_ref, dt_ref = refs[:10]
    refs = refs[10:]
    xn_ref = refs.pop(0) if emit_xn else None
    carry_ref = None if single else refs.pop(0)

    x = x_ref[...]
    xn = x * lax.rsqrt(jnp.mean(x * x, axis=-1, keepdims=True) + NORM_EPS) * nw_ref[...]
    if emit_xn:
        xn_ref[...] = xn
    xb = xn.astype(BF16)
    seg = lambda lo, hi: _dot(xb, w_ref[:, lo:hi])
    if not single:
        first_tile = pl.program_id(0) % tiles_per_seq == 0

        @pl.when(first_tile)
        def _():
            carry_ref[...] = jnp.broadcast_to(prev_ref[0], carry_ref.shape)

    def mixed(lo, hi):
        p = seg(lo, hi)
        if single:
            shifted = prev_ref[:, lo:hi]
        else:
            shifted = jnp.where(_iota(p.shape, 0) == 0, carry_ref[0:1, lo:hi], pltpu.roll(p, 1, 0))
            carry_ref[:, lo:hi] = jnp.broadcast_to(p[tm - 1:tm, :], (carry_ref.shape[0], hi - lo))
        return p + (shifted - p) * mu_ref[:, lo:hi]

    def head_sum(t):
        hi, lo = _pieces(t, 2)
        cols = []
        for j in range(D_MODEL // 256):
            sl = slice(j * 256, (j + 1) * 256)
            cols.append(_dot(hi[:, sl], bd_ref[...]) + _dot(lo[:, sl], bd_ref[...]))
        return jnp.concatenate(cols, axis=1)

    wal = mixed(3 * D_MODEL, R_SHIFT)
    k = mixed(D_MODEL, 2 * D_MODEL)
    zr_ref[...] = seg(*SEG_ZR)
    dw = _dot(jnp.tanh(wal).astype(BF16), w2_ref[...])
    da = _dot(wal.astype(BF16), a2_ref[...])
    zm_ref[...] = seg(*SEG_ZM)
    logw = -DECAY_SCALE * _sigmoid(w0_ref[...] + dw)
    a = _sigmoid(a0_ref[...] + da)
    kk = k * kk_ref[...]
    if not single:
        cum = sum(_dot(tri_ref[...], piece) for piece in _pieces(logw, 2))
    kap = kk / jnp.maximum(jnp.sqrt(head_sum(kk * kk)), 1e-12)
    r = mixed(0, D_MODEL)
    v = mixed(2 * D_MODEL, 3 * D_MODEL)
    xbc_ref[...] = seg(*SEG_XBC)
    dt_ref[...] = seg(*SEG_DT)
    k2 = k * (1.0 + (a - 1.0) * ka_ref[...])
    o_v[...] = v
    if single:
        o_a[...] = kap
        o_b[...] = kap * a
        o_c[...] = k2
        o_d[...] = r
        o_g[...] = jnp.exp(logw)
    else:
        e_inv = jnp.exp(-cum)
        e_cum = jnp.exp(cum)
        o_a[...] = kap * jnp.exp(cum - logw)
        o_b[...] = kap * a * e_inv
        o_c[...] = k2 * e_inv
        o_d[...] = r * e_cum
        for j in range(tm // chunk):
            o_g[j] = e_cum[j * chunk + chunk - 1:j * chunk + chunk, :]


def _proj_prep(x, prev, lp, *, seq_len, tm, emit_xn=False):
    m, d = x.shape
    single = seq_len == 1
    chunk = 1 if single else WKV_CHUNK
    tiles_per_seq = 1 if single else seq_len // tm
    vec = lambda a, n: a.reshape(1, n)
    bd = (jnp.arange(256)[:, None] // HEAD == jnp.arange(256)[None, :] // HEAD).astype(BF16)
    ins = [x, vec(lp["norm_w"], d), lp["w_in"], prev, vec(lp["mu_shift"], R_SHIFT), vec(lp["w0"], D_MODEL),
           vec(lp["a0"], D_MODEL), vec(lp["k_k"], D_MODEL), vec(lp["k_a"], D_MODEL), lp["w2p"], lp["a2p"], bd]
    full = lambda shape: pl.BlockSpec(shape, lambda i: tuple(0 for _ in shape))
    if single:
        prev_spec = pl.BlockSpec((tm, R_SHIFT), lambda i: (i, 0))
    else:
        prev_spec = pl.BlockSpec((1, 1, R_SHIFT), lambda i: (i // tiles_per_seq, 0, 0))
    in_specs = [pl.BlockSpec((tm, d), lambda i: (i, 0)), full((1, d)), full((d, D_IN_PAD)), prev_spec,
                full((1, R_SHIFT)), full((1, D_MODEL)), full((1, D_MODEL)), full((1, D_MODEL)),
                full((1, D_MODEL)), full((2 * LORA, D_MODEL)), full((2 * LORA, D_MODEL)), full((256, 256))]
    scratch = []
    if not single:
        t = jnp.arange(tm)
        tri = ((t[:, None] // chunk == t[None, :] // chunk) & (t[None, :] <= t[:, None])).astype(BF16)
        ins.append(tri)
        in_specs.append(full((tm, tm)))
        scratch.append(pltpu.VMEM((8, R_SHIFT), F32))
    rows = lambda w: pl.BlockSpec((tm, w), lambda i: (i, 0))
    shp = lambda w: jax.ShapeDtypeStruct((m, w), F32)
    if single:
        g_spec, g_shape = rows(D_MODEL), shp(D_MODEL)
    else:
        g_spec = pl.BlockSpec((tm // chunk, 1, D_MODEL), lambda i: (i, 0, 0))
        g_shape = jax.ShapeDtypeStruct((m // chunk, 1, D_MODEL), F32)
    widths = [D_MODEL, D_MODEL, CONV_DIM, DT_PAD] + ([d] if emit_xn else [])
    return pl.pallas_call(
        functools.partial(_proj_prep_kernel, tm=tm, chunk=chunk, tiles_per_seq=tiles_per_seq, single=single,
                          emit_xn=emit_xn),
        grid=(m // tm,),
        in_specs=in_specs,
        out_specs=[rows(D_MODEL)] * 5 + [g_spec] + [rows(w) for w in widths],
        out_shape=[shp(D_MODEL)] * 5 + [g_shape] + [shp(w) for w in widths],
        scratch_shapes=scratch,
        compiler_params=_cparams(("arbitrary",)),
        name="proj_prep",
    )(*ins)


def _stack(x):
    first = _iota(x.shape, 1) < HEAD
    return jnp.concatenate([jnp.where(first, x, 0.0), jnp.where(first, 0.0, x)], axis=0)


def _head_sum_nat(x):
    first = _iota(x.shape, 1) < HEAD
    s_first = jnp.sum(jnp.where(first, x, 0.0), axis=-1, keepdims=True)
    s_all = jnp.sum(x, axis=-1, keepdims=True)
    return jnp.where(first, s_first, s_all - s_first)


def _wkv_bonus(rs, ks, vs, rk_row):
    return jnp.sum(rs * ks * rk_row, axis=-1, keepdims=True) * vs


def _wkv_post(o_s, bonus, lw_row, lb_row, z, rows):
    fold = lambda t: t[0:rows] + t[rows:2 * rows]
    o = fold(o_s)
    dev = o - _head_sum_nat(o) * (1.0 / HEAD)
    var = _head_sum_nat(dev * dev) * (1.0 / HEAD)
    y = dev * lax.rsqrt(var + LNX_EPS) * lw_row + lb_row + fold(bonus)
    return y * (z * _sigmoid(z))


def _each(f, *lists):
    return [f(*a) for a in zip(*lists)]


def _wkv_pairs_chunk(kt, bh, kh, rt, v, z, g_row, rk_row, lw_row, lb_row, s0, chunk):
    n2 = 2 * chunk
    bf = lambda t: t.astype(BF16)
    xs, bs, ks, rs, vs = (_each(_stack, t) for t in (kt, bh, kh, rt, v))
    xb, bb, kb, rb, vb = (_each(bf, t) for t in (xs, bs, ks, rs, vs))
    bonus = _each(_wkv_bonus, rs, ks, vs, rk_row)

    row = _iota((n2, n2), 0)
    col = _iota((n2, n2), 1)
    same = (row >> (chunk.bit_length() - 1)) == (col >> (chunk.bit_length() - 1))
    rt_ = row & (chunk - 1)
    ct_ = col & (chunk - 1)
    strict = same & (ct_ < rt_)
    incl = same & (ct_ <= rt_)
    lower = lambda a: jnp.where(strict, a, 0.0)
    lower_d = lambda a: jnp.where(incl, a, 0.0)

    gram = _each(lambda x, r, k, b: _dot_nt(jnp.concatenate([x, r], axis=0), jnp.concatenate([k, b], axis=0)),
                 xb, rb, kb, bb)
    a_kb = _each(lambda g: lower(g[:n2, n2:]), gram)
    a_kr = _each(lambda g: bf(jnp.concatenate([lower(g[:n2, :n2]), lower_d(g[n2:, :n2])], axis=0)), gram)
    a_rb = _each(lambda g: bf(lower_d(g[n2:, n2:])), gram)

    eye = (row == col).astype(F32)
    pair2 = (rt_ | 1) == (ct_ | 1)
    t_inv = _each(lambda a: eye - jnp.where(pair2, a, 0.0), a_kb)
    s = 2
    while s < chunk:
        off = same & ((rt_ & ~(2 * s - 1)) == (ct_ & ~(2 * s - 1))) & ((rt_ & s) != 0) & ((ct_ & s) == 0)
        b_off = _each(lambda a: bf(jnp.where(off, a, 0.0)), a_kb)
        t_b = _each(bf, t_inv)
        bt = _each(lambda b, t: bf(_dot(b, t)), b_off, t_b)
        t_inv = _each(lambda t, tb, x: t - _dot(tb, x), t_inv, t_b, bt)
        s *= 2
    t_b = _each(bf, t_inv)

    s0_b = _each(bf, s0)
    xa = _each(_dot, a_kr, vb)
    zs = _each(lambda x, r, sb: _dot_nt(jnp.concatenate([x, r], axis=0), sb), xb, rb, s0_b)
    u_b = _each(lambda t, z_, a: bf(_dot(t, bf(z_[:n2] + a[:n2]))), t_b, zs, xa)
    o_s = _each(lambda z_, a, arb, u: z_[n2:] + a[n2:] - _dot(arb, u), zs, xa, a_rb, u_b)
    n_t = _each(lambda v_, u, k, b: _dot_tn(jnp.concatenate([v_, -u], axis=0), jnp.concatenate([k, b], axis=0)),
                vb, u_b, kb, bb)
    s1 = _each(lambda s_, n, g: (s_ + n) * g, s0, n_t, g_row)
    y = _each(lambda *a: _wkv_post(*a, chunk), o_s, bonus, lw_row, lb_row, z)
    return y, s1


def _wkv_chunk_kernel(kt_ref, bh_ref, kh_ref, rt_ref, v_ref, z_ref, gc_ref, rk_ref, lw_ref, lb_ref,
                      y_ref, sout_ref, s_scr, *, chunk, nseq):
    c = pl.program_id(1)

    @pl.when(c == 0)
    def _():
        s_scr[...] = jnp.zeros(s_scr.shape, F32)

    items = [(i, slice(g * PAIR, (g + 1) * PAIR)) for i in range(nseq) for g in range(N_PAIRS)]
    tiles = lambda ref: [ref[i, :, ln] for i, ln in items]
    params = lambda ref: [ref[:, ln] for _, ln in items]
    y, s1 = _wkv_pairs_chunk(tiles(kt_ref), tiles(bh_ref), tiles(kh_ref), tiles(rt_ref), tiles(v_ref),
                             tiles(z_ref), [gc_ref[i, 0, :, ln] for i, ln in items], params(rk_ref),
                             params(lw_ref), params(lb_ref), [s_scr[n] for n in range(len(items))], chunk)
    for n, (i, ln) in enumerate(items):
        y_ref[i, :, ln] = y[n].astype(y_ref.dtype)
        s_scr[n] = s1[n]

    @pl.when(c == pl.num_programs(1) - 1)
    def _():
        for n in range(len(items)):
            i, g = divmod(n, N_PAIRS)
            sout_ref[i, 2 * g] = s_scr[n, 0:HEAD, 0:HEAD]
            sout_ref[i, 2 * g + 1] = s_scr[n, HEAD:PAIR, HEAD:PAIR]


def _wkv_chunk(kt, bh, kh, rt, v, z, gc, lp, *, batch, seq_len):
    chunk = WKV_CHUNK
    nseq = WKV_SEQS_PER_STEP
    nc = seq_len // chunk
    seq = lambda a: a.reshape(batch, seq_len, D_MODEL)
    tok = pl.BlockSpec((nseq, chunk, D_MODEL), lambda b, c: (b, c, 0))
    par = pl.BlockSpec((1, D_MODEL), lambda b, c: (0, 0))
    st = pl.BlockSpec((nseq, N_HEADS, HEAD, HEAD), lambda b, c: (b, 0, 0, 0))
    y, s_bd = pl.pallas_call(
        functools.partial(_wkv_chunk_kernel, chunk=chunk, nseq=nseq),
        grid=(batch // nseq, nc),
        in_specs=[tok] * 6 + [pl.BlockSpec((nseq, 1, 1, D_MODEL), lambda b, c: (b, c, 0, 0)), par, par, par],
        out_specs=[tok, st],
        out_shape=[jax.ShapeDtypeStruct((batch, seq_len, D_MODEL), BF16),
                   jax.ShapeDtypeStruct((batch, N_HEADS, HEAD, HEAD), F32)],
        scratch_shapes=[pltpu.VMEM((nseq * N_PAIRS, PAIR, PAIR), F32)],
        compiler_params=_cparams(("arbitrary", "arbitrary")),
        name="rwkv_chunk",
    )(seq(kt), seq(bh), seq(kh), seq(rt), seq(v), seq(z), gc.reshape(batch, nc, 1, D_MODEL),
      lp["r_k"].reshape(1, D_MODEL), lp["lnx_w"].reshape(1, D_MODEL), lp["lnx_b"].reshape(1, D_MODEL))
    return y.reshape(batch * seq_len, D_MODEL), s_bd


def _cols_of(tile):
    pad = jnp.zeros((LANES - tile.shape[0], LANES), F32)
    return jnp.concatenate([tile, pad], axis=0).T


def _wkv_step_kernel(*refs, layer, first):
    (kap_ref, b_ref, k_ref, r_ref, v_ref, w_ref, z_ref, rk_ref, lw_ref, lb_ref, s0_ref) = refs[:11]
    y_ref, sout_all, vt_scr, ot_scr = refs[11:] if first else refs[12:]
    sout_ref = sout_all.at[layer] if first else sout_all
    if first:
        for other in range(sout_all.shape[0]):
            if other != layer:
                sout_all[other] = jnp.zeros(sout_all.shape[1:], F32)
    tr = lambda ref: ref[...].T
    kap_t, b_t, k_t, r_t, w_t = tr(kap_ref), tr(b_ref), tr(k_ref), tr(r_ref), tr(w_ref)
    vt_scr[...] = tr(v_ref)

    def body(vi, carry):
        for hh in range(2):
            rows = slice(hh * HEAD, (hh + 1) * HEAD)
            s = s0_ref[hh, vi]
            s_kk = jnp.sum(s * kap_t[rows], axis=0, keepdims=True)
            v_row = vt_scr[pl.ds(hh * HEAD + vi, 1), :]
            s1 = s * w_t[rows] - s_kk * b_t[rows] + v_row * k_t[rows]
            sout_ref[hh, vi] = s1
            ot_scr[pl.ds(hh * HEAD + vi, 1), :] = jnp.sum(s1 * r_t[rows], axis=0, keepdims=True)
        return carry

    lax.fori_loop(0, HEAD, body, 0)
    o = ot_scr[...].T
    dev = o - _head_sum_nat(o) * (1.0 / HEAD)
    var = _head_sum_nat(dev * dev) * (1.0 / HEAD)
    bonus = _head_sum_nat(r_ref[...] * k_ref[...] * rk_ref[...]) * v_ref[...]
    z = z_ref[...]
    y = dev * lax.rsqrt(var + LNX_EPS) * lw_ref[...] + lb_ref[...] + bonus
    y_ref[...] = (y * (z * _sigmoid(z))).astype(y_ref.dtype)


def _wkv_step(kap, b, k2, r, v, w, z, lp, state_all_t, layer, prev_out):
    m = kap.shape[0]
    tok = pl.BlockSpec((m, PAIR), lambda g: (0, g))
    par = pl.BlockSpec((1, PAIR), lambda g: (0, g))
    depth = state_all_t.shape[0]
    first = prev_out is None
    st = pl.BlockSpec((None, 2, HEAD, HEAD, m), lambda g: (layer, g, 0, 0, 0))
    st_all = pl.BlockSpec((depth, 2, HEAD, HEAD, m), lambda g: (0, g, 0, 0, 0))
    ins = [kap, b, k2, r, v, w, z, lp["r_k"].reshape(1, D_MODEL), lp["lnx_w"].reshape(1, D_MODEL),
           lp["lnx_b"].reshape(1, D_MODEL), state_all_t]
    in_specs = [tok] * 7 + [par, par, par, st]
    aliases = {}
    if not first:
        ins.append(prev_out)
        in_specs.append(pl.BlockSpec(memory_space=pl.ANY))
        aliases = {len(ins) - 1: 1}
    return pl.pallas_call(
        functools.partial(_wkv_step_kernel, layer=layer, first=first),
        grid=(N_PAIRS,),
        in_specs=in_specs,
        out_specs=[tok, st_all if first else st],
        out_shape=[jax.ShapeDtypeStruct((m, D_MODEL), F32), jax.ShapeDtypeStruct(state_all_t.shape, F32)],
        scratch_shapes=[pltpu.VMEM((PAIR, m), F32), pltpu.VMEM((PAIR, m), F32)],
        input_output_aliases=aliases,
        compiler_params=_cparams(("arbitrary",)),
        name="rwkv_step",
    )(*ins)


def _gate_norm(y, xs, zm, dsk_row, gn_row):
    y = (y + dsk_row * xs) * (zm * _sigmoid(zm))
    half = D_MODEL // N_GROUPS
    outs = []
    for g in range(N_GROUPS):
        yg = y[:, g * half:(g + 1) * half]
        outs.append(yg * lax.rsqrt(jnp.mean(yg * yg, axis=-1, keepdims=True) + NORM_EPS))
    return jnp.concatenate(outs, axis=1) * gn_row


def _ssd_chunk_kernel(xbc_ref, zm_ref, dt_ref, cw_ref, cb_ref, dtb_ref, alog_ref, dsk_ref, gn_ref, e_ref, et_ref,
                      y_ref, cout_ref, hout_ref, ext, h_scr, *, q):
    c = pl.program_id(1)

    @pl.when(c == 0)
    def _():
        ext[0:8, :] = jnp.zeros((8, CONV_DIM), F32)
        h_scr[...] = jnp.zeros(h_scr.shape, F32)

    ext[8:8 + q, :] = xbc_ref[...]
    u = cb_ref[...] + cw_ref[3:4, :] * ext[8:8 + q, :]
    for i in range(CONV_W - 1):
        u = u + cw_ref[i:i + 1, :] * ext[5 + i:5 + i + q, :]
    tail = ext[q:q + 8, :]
    ext[0:8, :] = tail
    u = u * _sigmoid(u)
    xs = u[:, :D_MODEL]
    bm = [u[:, D_MODEL + g * D_STATE:D_MODEL + (g + 1) * D_STATE].astype(BF16) for g in range(N_GROUPS)]
    cm = [u[:, D_MODEL + (N_GROUPS + g) * D_STATE:D_MODEL + (N_GROUPS + g + 1) * D_STATE].astype(BF16)
          for g in range(N_GROUPS)]

    dt = _softplus(dt_ref[...] + dtb_ref[...])
    d_a = dt * (-jnp.exp(alog_ref[...]))
    row = _iota((q, q), 0)
    col = _iota((q, q), 1)
    causal = col <= row
    tril = causal.astype(BF16)
    cs = sum(_dot(tril, piece) for piece in _pieces(d_a, 3))
    eye = (row == col).astype(BF16)
    cs_t = sum(_dot_tn(piece, eye) for piece in _pieces(cs, 3))
    last = cs[q - 1:q, :]
    expand = lambda t: sum(_dot(piece, e_ref[...]) for piece in _pieces(t, 2))
    dt_x = expand(dt)
    dte_x = expand(jnp.exp(last - cs))
    ecs_x = expand(jnp.exp(cs))
    xdt = xs * dt_x
    cd = jnp.exp(cs_t[:, q - 1:q])
    cd_b = jnp.broadcast_to(cd, (LANES, D_STATE))
    decay_full = sum(_dot(et_ref[...], piece) for piece in _pieces(cd_b, 2))

    pairs = list(range(N_PAIRS))
    grp = lambda pi: pi // (N_PAIRS // N_GROUPS)
    rows = lambda pi: slice(pi * PAIR, (pi + 1) * PAIR)
    cb = [_dot_nt(cm[g], bm[g]) for g in range(N_GROUPS)]

    def decay_weighted(pi):
        gs = []
        for hh in (2 * pi, 2 * pi + 1):
            seg = cs[:, hh:hh + 1] - cs_t[hh:hh + 1, :]
            gs.append(cb[grp(pi)] * jnp.exp(jnp.where(causal, seg, -1e30)))
        return jnp.concatenate(gs, axis=1).astype(BF16)

    g_pair = [decay_weighted(pi) for pi in pairs]
    h_in = [h_scr[rows(pi), :] for pi in pairs]
    y_diag = [_dot(g_pair[pi], _stack(xdt[:, rows(pi)]).astype(BF16)) for pi in pairs]
    y_off = [_dot_nt(cm[grp(pi)], h_in[pi].astype(BF16)) for pi in pairs]
    states = [_dot_tn((xdt[:, rows(pi)] * dte_x[:, rows(pi)]).astype(BF16), bm[grp(pi)]) for pi in pairs]
    for pi in pairs:
        h_scr[rows(pi), :] = decay_full[rows(pi), :] * h_in[pi] + states[pi]
    ys = [y_diag[pi] + y_off[pi] * ecs_x[:, rows(pi)] for pi in pairs]
    y = jnp.concatenate(ys, axis=1)
    y_ref[...] = _gate_norm(y, xs, zm_ref[...], dsk_ref[...], gn_ref[...]).astype(y_ref.dtype)

    @pl.when(c == pl.num_programs(1) - 1)
    def _():
        cout_ref[0] = ext[8 + q - (CONV_W - 1):8 + q, :]
        hout_ref[0] = h_scr[...]


def _ssd_consts(lp):
    heads = jnp.arange(LANES)[:, None]
    cols = jnp.arange(D_MODEL)[None, :] // HEAD
    e_mat = (heads == cols).astype(BF16)
    pad = lambda a: jnp.pad(a.reshape(1, N_HEADS), ((0, 0), (0, DT_PAD - N_HEADS)))
    return dict(cw=lp["conv_w"], cb=lp["conv_b"].reshape(1, CONV_DIM), dtb=pad(lp["dt_bias"]),
                alog=pad(lp["a_log"]), dsk=jnp.repeat(lp["d_skip"], HEAD).reshape(1, D_MODEL),
                gn=lp["gnorm_w"].reshape(1, D_MODEL), e=e_mat, et=e_mat.T)


def _ssd_chunk(xbc, zm, dt, lp, *, batch, seq_len):
    q = SSD_CHUNK
    nq = seq_len // q
    m = batch * seq_len
    k = _ssd_consts(lp)
    full = lambda shape: pl.BlockSpec(shape, lambda b, c: tuple(0 for _ in shape))
    rows = lambda w: pl.BlockSpec((q, w), lambda b, c: (b * nq + c, 0))
    ins = [xbc, zm, dt, k["cw"], k["cb"], k["dtb"], k["alog"], k["dsk"], k["gn"], k["e"], k["et"]]
    in_specs = [rows(CONV_DIM), rows(D_MODEL), rows(DT_PAD), full((CONV_W, CONV_DIM)), full((1, CONV_DIM)),
                full((1, DT_PAD)), full((1, DT_PAD)), full((1, D_MODEL)), full((1, D_MODEL)),
                full((LANES, D_MODEL)), full((D_MODEL, LANES))]
    return pl.pallas_call(
        functools.partial(_ssd_chunk_kernel, q=q),
        grid=(batch, nq),
        in_specs=in_specs,
        out_specs=[rows(D_MODEL), pl.BlockSpec((1, CONV_W - 1, CONV_DIM), lambda b, c: (b, 0, 0)),
                   pl.BlockSpec((1, D_MODEL, D_STATE), lambda b, c: (b, 0, 0))],
        out_shape=[jax.ShapeDtypeStruct((m, D_MODEL), BF16),
                   jax.ShapeDtypeStruct((batch, CONV_W - 1, CONV_DIM), F32),
                   jax.ShapeDtypeStruct((batch, D_MODEL, D_STATE), F32)],
        scratch_shapes=[pltpu.VMEM((q + 8, CONV_DIM), F32), pltpu.VMEM((D_MODEL, D_STATE), F32)],
        compiler_params=_cparams(("arbitrary", "arbitrary")),
        name="ssd_chunk",
    )(*ins)


def _ssd_step_kernel(*refs, nseq, layer, first):
    (xbc_ref, c_ref, zm_ref, dt_ref, cw_ref, cb_ref, dtb_ref, alog_ref, dsk_ref, gn_ref, e_ref,
     h0_ref) = refs[:12]
    y_ref, cout_all, hout_all = refs[12:15] if first else refs[14:17]
    cout_ref = cout_all.at[layer] if first else cout_all
    hout_ref = hout_all.at[layer] if first else hout_all
    if first:
        for other in range(cout_all.shape[0]):
            if other != layer:
                cout_all[other] = jnp.zeros(cout_all.shape[1:], F32)
                hout_all[other] = jnp.zeros(hout_all.shape[1:], F32)
    rowid = _iota((nseq, CONV_DIM), 0)
    u = cb_ref[...] + cw_ref[3:4, :] * xbc_ref[...]
    for i in range(CONV_W - 1):
        prev_rows = jnp.zeros((nseq, CONV_DIM), F32)
        for j in range(nseq):
            prev_rows = jnp.where(rowid == j, c_ref[j, i:i + 1, :], prev_rows)
        u = u + cw_ref[i:i + 1, :] * prev_rows
    for j in range(nseq):
        cout_ref[j, 0:CONV_W - 2, :] = c_ref[j, 1:CONV_W - 1, :]
        cout_ref[j, CONV_W - 2:CONV_W - 1, :] = xbc_ref[j:j + 1, :]
    u = u * _sigmoid(u)
    xs = u[:, :D_MODEL]
    dt = _softplus(dt_ref[...] + dtb_ref[...])
    ed = jnp.exp(dt * (-jnp.exp(alog_ref[...])))
    expand = lambda t: sum(_dot(piece, e_ref[...]) for piece in _pieces(t, 2))
    xdt = xs * expand(dt)
    lane = _iota((PAIR, LANES), 1)
    sel_row = _iota((LANES, D_STATE), 0)
    ys = []
    for p in range(N_PAIRS):
        g = p // (N_PAIRS // N_GROUPS)
        pl_ = slice(p * PAIR, (p + 1) * PAIR)
        b_rows = u[:, D_MODEL + g * D_STATE:D_MODEL + (g + 1) * D_STATE]
        c_rows = u[:, D_MODEL + (N_GROUPS + g) * D_STATE:D_MODEL + (N_GROUPS + g + 1) * D_STATE]
        x_cols_b = _cols_of(xdt[:, pl_]).astype(BF16)
        upper = _iota((PAIR, D_STATE), 0) < HEAD
        decay = lambda j: jnp.where(upper, ed[j:j + 1, 2 * p:2 * p + 1], ed[j:j + 1, 2 * p + 1:2 * p + 2])
        outer = [_dot(x_cols_b, jnp.where(sel_row == j, b_rows[j:j + 1, :], 0.0).astype(BF16))
                 for j in range(nseq)]
        h1s = [decay(j) * h0_ref[j, pl_, :] + outer[j] for j in range(nseq)]
        for j in range(nseq):
            hout_ref[j, pl_, :] = h1s[j]
        y_cols = [jnp.sum(h1s[j] * c_rows[j:j + 1, :], axis=-1, keepdims=True) for j in range(nseq)]
        y_t = jnp.zeros((PAIR, LANES), F32)
        for j in range(nseq):
            y_t = jnp.where(lane == j, y_cols[j], y_t)
        ys.append(y_t.T[0:nseq, :])
    y = jnp.concatenate(ys, axis=1)
    y_ref[...] = _gate_norm(y, xs, zm_ref[...], dsk_ref[...], gn_ref[...]).astype(y_ref.dtype)


def _ssd_step(xbc, zm, dt, lp, conv_all, ssm_all, layer, prev_conv, prev_ssm):
    m = xbc.shape[0]
    nseq = 8
    k = _ssd_consts(lp)
    full = lambda shape: pl.BlockSpec(shape, lambda i: tuple(0 for _ in shape))
    rows = lambda w: pl.BlockSpec((nseq, w), lambda i: (i, 0))
    cst = pl.BlockSpec((None, nseq, CONV_W - 1, CONV_DIM), lambda i: (layer, i, 0, 0))
    hst = pl.BlockSpec((None, nseq, D_MODEL, D_STATE), lambda i: (layer, i, 0, 0))
    depth = conv_all.shape[0]
    cst_all = pl.BlockSpec((depth, nseq, CONV_W - 1, CONV_DIM), lambda i: (0, i, 0, 0))
    hst_all = pl.BlockSpec((depth, nseq, D_MODEL, D_STATE), lambda i: (0, i, 0, 0))
    ins = [xbc, conv_all, zm, dt, k["cw"], k["cb"], k["dtb"], k["alog"], k["dsk"], k["gn"], k["e"], ssm_all]
    in_specs = [rows(CONV_DIM), cst, rows(D_MODEL), rows(DT_PAD), full((CONV_W, CONV_DIM)),
                full((1, CONV_DIM)), full((1, DT_PAD)), full((1, DT_PAD)), full((1, D_MODEL)),
                full((1, D_MODEL)), full((LANES, D_MODEL)), hst]
    aliases = {}
    first = prev_conv is None
    if not first:
        ins += [prev_conv, prev_ssm]
        in_specs += [pl.BlockSpec(memory_space=pl.ANY)] * 2
        aliases = {len(ins) - 2: 1, len(ins) - 1: 2}
    return pl.pallas_call(
        functools.partial(_ssd_step_kernel, nseq=nseq, layer=layer, first=first),
        grid=(m // nseq,),
        in_specs=in_specs,
        out_specs=[rows(D_MODEL), cst_all if first else cst, hst_all if first else hst],
        out_shape=[jax.ShapeDtypeStruct((m, D_MODEL), F32), jax.ShapeDtypeStruct(conv_all.shape, F32),
                   jax.ShapeDtypeStruct(ssm_all.shape, F32)],
        input_output_aliases=aliases,
        compiler_params=_cparams(("arbitrary",)),
        name="ssd_step",
    )(*ins)


def _out_kernel(*refs, final):
    if final:
        x_ref, yr_ref, ym_ref, w_ref, fw_ref, o_ref = refs
    else:
        x_ref, yr_ref, ym_ref, w_ref, o_ref = refs
    out = (x_ref[...] + _dot(yr_ref[...].astype(BF16), w_ref[0:D_MODEL, :])
           + _dot(ym_ref[...].astype(BF16), w_ref[D_MODEL:2 * D_MODEL, :]))
    if final:
        out = out * lax.rsqrt(jnp.mean(out * out, axis=-1, keepdims=True) + NORM_EPS) * fw_ref[...]
    o_ref[...] = out


def _out_proj(x, yr, ym, w_bf, final_w, tm):
    m, d = x.shape
    final = final_w is not None
    tok = pl.BlockSpec((tm, d), lambda i: (i, 0))
    ins = [x, yr, ym, w_bf] + ([final_w.reshape(1, d)] if final else [])
    in_specs = [tok, tok, tok, pl.BlockSpec((2 * d, d), lambda i: (0, 0))]
    if final:
        in_specs.append(pl.BlockSpec((1, d), lambda i: (0, 0)))
    return pl.pallas_call(
        functools.partial(_out_kernel, final=final),
        grid=(m // tm,),
        in_specs=in_specs,
        out_specs=tok,
        out_shape=jax.ShapeDtypeStruct((m, d), F32),
        compiler_params=_cparams(("arbitrary",)),
        name="out_proj",
    )(*ins)


def _layer_prompt(x, lp, final_w, *, batch, seq_len):
    tm = 256
    prev = jnp.zeros((batch, 1, R_SHIFT), F32)
    kt, bh, kh, rt, v, gc, z_r, z_m, xbc, dt = _proj_prep(x, prev, lp, seq_len=seq_len, tm=tm)
    xn_last = _rmsnorm(x.reshape(batch, seq_len, D_MODEL)[:, -1], lp["norm_w"])
    y_r, s_bd = _wkv_chunk(kt, bh, kh, rt, v, z_r, gc, lp, batch=batch, seq_len=seq_len)
    y_m, conv_new, ssm_new = _ssd_chunk(xbc, z_m, dt, lp, batch=batch, seq_len=seq_len)
    x_new = _out_proj(x, y_r, y_m, lp["w_out"], final_w, 512)
    return (x_new, xn_last, s_bd, conv_new,
            ssm_new.reshape(batch, N_HEADS, HEAD, D_STATE))


def _layer_sample(x, states, layer, prev_outs, lp, final_w):
    m = x.shape[0]
    tm = m
    shift_all, wkv_all, conv_all, ssm_all = states
    (prev,) = _proj(shift_all[layer], None, lp["w_in"], (SEG_SH,), tm)
    kap, b, k2, r, v, w, z_r, z_m, xbc, dt, xn = _proj_prep(x, prev, lp, seq_len=1, tm=tm, emit_xn=True)
    y_r, wkv_out = _wkv_step(kap, b, k2, r, v, w, z_r, lp, wkv_all, layer, prev_outs[0])
    y_m, conv_out, ssm_out = _ssd_step(xbc, z_m, dt, lp, conv_all, ssm_all, layer, prev_outs[1], prev_outs[2])
    x_new = _out_proj(x, y_r, y_m, lp["w_out"], final_w, tm)
    return x_new, xn, (wkv_out, conv_out, ssm_out)


def _layer_params(params, l):
    lp = {k: v[l] for k, v in params.items() if k != "w_in"}
    lp["w_in"] = _cast_w_in(jnp.swapaxes(params["w_in"], 1, 2), l)
    lp["w_out"] = lp["w_out"].astype(BF16)
    zeros = jnp.zeros((LORA, D_MODEL), F32)
    lp["w2p"] = jnp.concatenate([lp["w_lora2"], zeros], axis=0).astype(BF16)
    lp["a2p"] = jnp.concatenate([zeros, lp["a_lora2"]], axis=0).astype(BF16)
    return lp


def kernel(x_prompt, x_sample, state_shift, state_wkv, state_conv, state_ssm, norm_w, w_in, mu_shift, w0,
           w_lora2, a0, a_lora2, k_k, k_a, r_k, lnx_w, lnx_b, conv_w, conv_b, dt_bias, a_log, d_skip,
           gnorm_w, w_out, final_norm_w):
    params = dict(norm_w=norm_w, w_in=w_in, mu_shift=mu_shift, w0=w0, w_lora2=w_lora2, a0=a0,
                  a_lora2=a_lora2, k_k=k_k, k_a=k_a, r_k=r_k, lnx_w=lnx_w, lnx_b=lnx_b, conv_w=conv_w,
                  conv_b=conv_b, dt_bias=dt_bias, a_log=a_log, d_skip=d_skip, gnorm_w=gnorm_w, w_out=w_out)
    depth = norm_w.shape[0]
    bp, lseq, d = x_prompt.shape
    bs = x_sample.shape[0]
    xp = x_prompt.reshape(bp * lseq, d)
    xs = x_sample.reshape(bs, d)
    sample_states = (state_shift, jnp.transpose(state_wkv, (0, 2, 3, 4, 1)), state_conv,
                     state_ssm.reshape(depth, bs, N_HEADS * HEAD, D_STATE))
    p_states, s_shift = [], []
    s_outs = (None, None, None)
    for l in range(depth):
        lp = _layer_params(params, l)
        fw = final_norm_w if l == depth - 1 else None
        xp, *st = _layer_prompt(xp, lp, fw, batch=bp, seq_len=lseq)
        p_states.append(st)
        xs, xn, s_outs = _layer_sample(xs, sample_states, l, s_outs, lp, fw)
        s_shift.append(xn)
    stack = lambda states, i: jnp.stack([s[i] for s in states])
    return (xp.reshape(bp, lseq, d), xs.reshape(bs, 1, d),
            stack(p_states, 0), stack(p_states, 1), stack(p_states, 2), stack(p_states, 3),
            jnp.stack(s_shift), jnp.transpose(s_outs[0], (0, 4, 1, 2, 3)), s_outs[1],
            s_outs[2].reshape(state_ssm.shape))
```

```python
import functools

import jax
import jax.numpy as jnp
from jax import lax
from jax.experimental import pallas as pl
from jax.experimental.pallas import tpu as pltpu

F32 = jnp.float32
BF16 = jnp.bfloat16

D_MODEL = 1024
HEAD = 64
N_HEADS = 16
LORA = 64
R_SHIFT = 3 * D_MODEL + 2 * LORA
D_STATE = 128
N_GROUPS = 2
CONV_W = 4
CONV_DIM = D_MODEL + 2 * N_GROUPS * D_STATE
DT_PAD = 128
SEG_SH = (0, R_SHIFT)
SEG_ZR = (R_SHIFT, R_SHIFT + D_MODEL)
SEG_ZM = (SEG_ZR[1], SEG_ZR[1] + D_MODEL)
SEG_XBC = (SEG_ZM[1], SEG_ZM[1] + CONV_DIM)
SEG_DT = (SEG_XBC[1], SEG_XBC[1] + DT_PAD)
D_IN_PAD = SEG_DT[1]
NORM_EPS = 1e-5
LNX_EPS = 64e-5
DECAY_SCALE = 0.6065306597126334
LANES = 128
PAIR = 2 * HEAD
N_PAIRS = N_HEADS // 2
WKV_CHUNK = 64
SSD_CHUNK = 128
WKV_SEQS_PER_STEP = 4
VMEM_LIMIT = 56 * 1024 * 1024


def _cparams(sem):
    return pltpu.CompilerParams(dimension_semantics=sem, vmem_limit_bytes=VMEM_LIMIT)


def _dot(a, b):
    return jnp.dot(a, b, preferred_element_type=F32)


def _dot_nt(a, b):
    return lax.dot_general(a, b, (((1,), (1,)), ((), ())), preferred_element_type=F32)


def _dot_tn(a, b):
    return lax.dot_general(a, b, (((0,), (0,)), ((), ())), preferred_element_type=F32)


def _pieces(x, n):
    out, r = [], x
    for _ in range(n):
        p = r.astype(BF16)
        out.append(p)
        r = r - p.astype(F32)
    return out


def _sigmoid(x):
    return 1.0 / (1.0 + jnp.exp(-x))


def _softplus(x):
    return jnp.maximum(x, 0.0) + jnp.log(1.0 + jnp.exp(-jnp.abs(x)))


def _iota(shape, dim):
    return lax.broadcasted_iota(jnp.int32, shape, dim)


def _proj_kernel(*refs, segs, norm, emit_xn):
    refs = list(refs)
    x_ref = refs.pop(0)
    nw_ref = refs.pop(0) if norm else None
    w_ref = refs.pop(0)
    x = x_ref[...]
    if norm:
        xn = x * lax.rsqrt(jnp.mean(x * x, axis=-1, keepdims=True) + NORM_EPS) * nw_ref[...]
    else:
        xn = x
    xb = xn.astype(BF16)
    for (lo, hi), o_ref in zip(segs, refs):
        o_ref[...] = _dot(xb, w_ref[:, lo:hi])
    if emit_xn:
        refs[len(segs)][...] = xn


def _proj(x, norm_w, w_bf, segs, tm, emit_xn=False):
    m, d = x.shape
    n = w_bf.shape[1]
    norm = norm_w is not None
    ins = [x] + ([norm_w.reshape(1, d)] if norm else []) + [w_bf]
    in_specs = [pl.BlockSpec((tm, d), lambda i: (i, 0))]
    if norm:
        in_specs.append(pl.BlockSpec((1, d), lambda i: (0, 0)))
    in_specs.append(pl.BlockSpec((d, n), lambda i: (0, 0)))
    widths = [hi - lo for lo, hi in segs] + ([d] if emit_xn else [])
    return pl.pallas_call(
        functools.partial(_proj_kernel, segs=tuple(segs), norm=norm, emit_xn=emit_xn),
        grid=(m // tm,),
        in_specs=in_specs,
        out_specs=[pl.BlockSpec((tm, w), lambda i: (i, 0)) for w in widths],
        out_shape=[jax.ShapeDtypeStruct((m, w), F32) for w in widths],
        compiler_params=_cparams(("arbitrary",)),
        name="norm_proj",
    )(*ins)


def _cast_w_kernel(wt_ref, o_ref, *, n_valid):
    rows = _iota(wt_ref.shape, 0) + pl.program_id(0) * wt_ref.shape[0]
    w_t = jnp.where(rows < n_valid, wt_ref[...], 0.0)
    o_ref[...] = w_t.T.astype(BF16)


def _cast_w_in(w_t_all, layer):
    _, n, d = w_t_all.shape
    tn = 256
    return pl.pallas_call(
        functools.partial(_cast_w_kernel, n_valid=n),
        grid=(D_IN_PAD // tn,),
        in_specs=[pl.BlockSpec((None, tn, d), lambda j: (layer, j, 0))],
        out_specs=pl.BlockSpec((d, tn), lambda j: (0, j)),
        out_shape=jax.ShapeDtypeStruct((d, D_IN_PAD), BF16),
        compiler_params=_cparams(("arbitrary",)),
        name="cast_w_in",
    )(w_t_all)


def _rmsnorm_kernel(x_ref, w_ref, o_ref):
    x = x_ref[...]
    o_ref[...] = x * lax.rsqrt(jnp.mean(x * x, axis=-1, keepdims=True) + NORM_EPS) * w_ref[...]


def _rmsnorm(x, w):
    m, d = x.shape
    return pl.pallas_call(
        _rmsnorm_kernel,
        out_shape=jax.ShapeDtypeStruct((m, d), F32),
        name="rmsnorm_rows",
    )(x, w.reshape(1, d))


def _proj_prep_kernel(*refs, tm, chunk, tiles_per_seq, single, emit_xn):
    (x_ref, nw_ref, w_ref, prev_ref, mu_ref, w0_ref, a0_ref, kk_ref, ka_ref, w2_ref, a2_ref, bd_ref) = refs[:12]
    refs = list(refs[12:])
    tri_ref = None if single else refs.pop(0)
    o_a, o_b, o_c, o_d, o_v, o_g, zr_ref, zm_ref, xbc_ref, dt_ref = refs[:10]
    refs = refs[10:]
    xn_ref = refs.pop(0) if emit_xn else None
    carry_ref = None if single else refs.pop(0)

    x = x_ref[...]
    xn = x * lax.rsqrt(jnp.mean(x * x, axis=-1, keepdims=True) + NORM_EPS) * nw_ref[...]
    if emit_xn:
        xn_ref[...] = xn
    xb = xn.astype(BF16)
    seg = lambda lo, hi: _dot(xb, w_ref[:, lo:hi])
    if not single:
        first_tile = pl.program_id(0) % tiles_per_seq == 0

        @pl.when(first_tile)
        def _():
            carry_ref[...] = jnp.broadcast_to(prev_ref[0], carry_ref.shape)

    def mixed(lo, hi):
        p = seg(lo, hi)
        if single:
            shifted = prev_ref[:, lo:hi]
        else:
            shifted = jnp.where(_iota(p.shape, 0) == 0, carry_ref[0:1, lo:hi], pltpu.roll(p, 1, 0))
            carry_ref[:, lo:hi] = jnp.broadcast_to(p[tm - 1:tm, :], (carry_ref.shape[0], hi - lo))
        return p + (shifted - p) * mu_ref[:, lo:hi]

    def head_sum(t):
        hi, lo = _pieces(t, 2)
        cols = []
        for j in range(D_MODEL // 256):
            sl = slice(j * 256, (j + 1) * 256)
            cols.append(_dot(hi[:, sl], bd_ref[...]) + _dot(lo[:, sl], bd_ref[...]))
        return jnp.concatenate(cols, axis=1)

    wal = mixed(3 * D_MODEL, R_SHIFT)
    k = mixed(D_MODEL, 2 * D_MODEL)
    zr_ref[...] = seg(*SEG_ZR)
    dw = _dot(jnp.tanh(wal).astype(BF16), w2_ref[...])
    da = _dot(wal.astype(BF16), a2_ref[...])
    zm_ref[...] = seg(*SEG_ZM)
    logw = -DECAY_SCALE * _sigmoid(w0_ref[...] + dw)
    a = _sigmoid(a0_ref[...] + da)
    kk = k * kk_ref[...]
    if not single:
        cum = sum(_dot(tri_ref[...], piece) for piece in _pieces(logw, 2))
    kap = kk / jnp.maximum(jnp.sqrt(head_sum(kk * kk)), 1e-12)
    r = mixed(0, D_MODEL)
    v = mixed(2 * D_MODEL, 3 * D_MODEL)
    xbc_ref[...] = seg(*SEG_XBC)
    dt_ref[...] = seg(*SEG_DT)
    k2 = k * (1.0 + (a - 1.0) * ka_ref[...])
    o_v[...] = v
    if single:
        o_a[...] = kap
        o_b[...] = kap * a
        o_c[...] = k2
        o_d[...] = r
        o_g[...] = jnp.exp(logw)
    else:
        e_inv = jnp.exp(-cum)
        e_cum = jnp.exp(cum)
        o_a[...] = kap * jnp.exp(cum - logw)
        o_b[...] = kap * a * e_inv
        o_c[...] = k2 * e_inv
        o_d[...] = r * e_cum
        for j in range(tm // chunk):
            o_g[j] = e_cum[j * chunk + chunk - 1:j * chunk + chunk, :]


def _proj_prep(x, prev, lp, *, seq_len, tm, emit_xn=False):
    m, d = x.shape
    single = seq_len == 1
    chunk = 1 if single else WKV_CHUNK
    tiles_per_seq = 1 if single else seq_len // tm
    vec = lambda a, n: a.reshape(1, n)
    bd = (jnp.arange(256)[:, None] // HEAD == jnp.arange(256)[None, :] // HEAD).astype(BF16)
    ins = [x, vec(lp["norm_w"], d), lp["w_in"], prev, vec(lp["mu_shift"], R_SHIFT), vec(lp["w0"], D_MODEL),
           vec(lp["a0"], D_MODEL), vec(lp["k_k"], D_MODEL), vec(lp["k_a"], D_MODEL), lp["w2p"], lp["a2p"], bd]
    full = lambda shape: pl.BlockSpec(shape, lambda i: tuple(0 for _ in shape))
    if single:
        prev_spec = pl.BlockSpec((tm, R_SHIFT), lambda i: (i, 0))
    else:
        prev_spec = pl.BlockSpec((1, 1, R_SHIFT), lambda i: (i // tiles_per_seq, 0, 0))
    in_specs = [pl.BlockSpec((tm, d), lambda i: (i, 0)), full((1, d)), full((d, D_IN_PAD)), prev_spec,
                full((1, R_SHIFT)), full((1, D_MODEL)), full((1, D_MODEL)), full((1, D_MODEL)),
                full((1, D_MODEL)), full((2 * LORA, D_MODEL)), full((2 * LORA, D_MODEL)), full((256, 256))]
    scratch = []
    if not single:
        t = jnp.arange(tm)
        tri = ((t[:, None] // chunk == t[None, :] // chunk) & (t[None, :] <= t[:, None])).astype(BF16)
        ins.append(tri)
        in_specs.append(full((tm, tm)))
        scratch.append(pltpu.VMEM((8, R_SHIFT), F32))
    rows = lambda w: pl.BlockSpec((tm, w), lambda i: (i, 0))
    shp = lambda w: jax.ShapeDtypeStruct((m, w), F32)
    if single:
        g_spec, g_shape = rows(D_MODEL), shp(D_MODEL)
    else:
        g_spec = pl.BlockSpec((tm // chunk, 1, D_MODEL), lambda i: (i, 0, 0))
        g_shape = jax.ShapeDtypeStruct((m // chunk, 1, D_MODEL), F32)
    widths = [D_MODEL, D_MODEL, CONV_DIM, DT_PAD] + ([d] if emit_xn else [])
    return pl.pallas_call(
        functools.partial(_proj_prep_kernel, tm=tm, chunk=chunk, tiles_per_seq=tiles_per_seq, single=single,
                          emit_xn=emit_xn),
        grid=(m // tm,),
        in_specs=in_specs,
        out_specs=[rows(D_MODEL)] * 5 + [g_spec] + [rows(w) for w in widths],
        out_shape=[shp(D_MODEL)] * 5 + [g_shape] + [shp(w) for w in widths],
        scratch_shapes=scratch,
        compiler_params=_cparams(("arbitrary",)),
        name="proj_prep",
    )(*ins)


def _stack(x):
    first = _iota(x.shape, 1) < HEAD
    return jnp.concatenate([jnp.where(first, x, 0.0), jnp.where(first, 0.0, x)], axis=0)


def _head_sum_nat(x):
    first = _iota(x.shape, 1) < HEAD
    s_first = jnp.sum(jnp.where(first, x, 0.0), axis=-1, keepdims=True)
    s_all = jnp.sum(x, axis=-1, keepdims=True)
    return jnp.where(first, s_first, s_all - s_first)


def _wkv_bonus(rs, ks, vs, rk_row):
    return jnp.sum(rs * ks * rk_row, axis=-1, keepdims=True) * vs


def _wkv_post(o_s, bonus, lw_row, lb_row, z, rows):
    fold = lambda t: t[0:rows] + t[rows:2 * rows]
    o = fold(o_s)
    dev = o - _head_sum_nat(o) * (1.0 / HEAD)
    var = _head_sum_nat(dev * dev) * (1.0 / HEAD)
    y = dev * lax.rsqrt(var + LNX_EPS) * lw_row + lb_row + fold(bonus)
    return y * (z * _sigmoid(z))


def _each(f, *lists):
    return [f(*a) for a in zip(*lists)]


def _wkv_pairs_chunk(kt, bh, kh, rt, v, z, g_row, rk_row, lw_row, lb_row, s0, chunk):
    n2 = 2 * chunk
    bf = lambda t: t.astype(BF16)
    xs, bs, ks, rs, vs = (_each(_stack, t) for t in (kt, bh, kh, rt, v))
    xb, bb, kb, rb, vb = (_each(bf, t) for t in (xs, bs, ks, rs, vs))
    bonus = _each(_wkv_bonus, rs, ks, vs, rk_row)

    row = _iota((n2, n2), 0)
    col = _iota((n2, n2), 1)
    same = (row >> (chunk.bit_length() - 1)) == (col >> (chunk.bit_length() - 1))
    rt_ = row & (chunk - 1)
    ct_ = col & (chunk - 1)
    strict = same & (ct_ < rt_)
    incl = same & (ct_ <= rt_)
    lower = lambda a: jnp.where(strict, a, 0.0)
    lower_d = lambda a: jnp.where(incl, a, 0.0)

    gram = _each(lambda x, r, k, b: _dot_nt(jnp.concatenate([x, r], axis=0), jnp.concatenate([k, b], axis=0)),
                 xb, rb, kb, bb)
    a_kb = _each(lambda g: lower(g[:n2, n2:]), gram)
    a_kr = _each(lambda g: bf(jnp.concatenate([lower(g[:n2, :n2]), lower_d(g[n2:, :n2])], axis=0)), gram)
    a_rb = _each(lambda g: bf(lower_d(g[n2:, n2:])), gram)

    eye = (row == col).astype(F32)
    pair2 = (rt_ | 1) == (ct_ | 1)
    t_inv = _each(lambda a: eye - jnp.where(pair2, a, 0.0), a_kb)
    s = 2
    while s < chunk:
        off = same & ((rt_ & ~(2 * s - 1)) == (ct_ & ~(2 * s - 1))) & ((rt_ & s) != 0) & ((ct_ & s) == 0)
        b_off = _each(lambda a: bf(jnp.where(off, a, 0.0)), a_kb)
        t_b = _each(bf, t_inv)
        bt = _each(lambda b, t: bf(_dot(b, t)), b_off, t_b)
        t_inv = _each(lambda t, tb, x: t - _dot(tb, x), t_inv, t_b, bt)
        s *= 2
    t_b = _each(bf, t_inv)

    s0_b = _each(bf, s0)
    xa = _each(_dot, a_kr, vb)
    zs = _each(lambda x, r, sb: _dot_nt(jnp.concatenate([x, r], axis=0), sb), xb, rb, s0_b)
    u_b = _each(lambda t, z_, a: bf(_dot(t, bf(z_[:n2] + a[:n2]))), t_b, zs, xa)
    o_s = _each(lambda z_, a, arb, u: z_[n2:] + a[n2:] - _dot(arb, u), zs, xa, a_rb, u_b)
    n_t = _each(lambda v_, u, k, b: _dot_tn(jnp.concatenate([v_, -u], axis=0), jnp.concatenate([k, b], axis=0)),
                vb, u_b, kb, bb)
    s1 = _each(lambda s_, n, g: (s_ + n) * g, s0, n_t, g_row)
    y = _each(lambda *a: _wkv_post(*a, chunk), o_s, bonus, lw_row, lb_row, z)
    return y, s1


def _wkv_chunk_kernel(kt_ref, bh_ref, kh_ref, rt_ref, v_ref, z_ref, gc_ref, rk_ref, lw_ref, lb_ref,
                      y_ref, sout_ref, s_scr, *, chunk, nseq):
    c = pl.program_id(1)

    @pl.when(c == 0)
    def _():
        s_scr[...] = jnp.zeros(s_scr.shape, F32)

    items = [(i, slice(g * PAIR, (g + 1) * PAIR)) for i in range(nseq) for g in range(N_PAIRS)]
    tiles = lambda ref: [ref[i, :, ln] for i, ln in items]
    params = lambda ref: [ref[:, ln] for _, ln in items]
    y, s1 = _wkv_pairs_chunk(tiles(kt_ref), tiles(bh_ref), tiles(kh_ref), tiles(rt_ref), tiles(v_ref),
                             tiles(z_ref), [gc_ref[i, 0, :, ln] for i, ln in items], params(rk_ref),
                             params(lw_ref), params(lb_ref), [s_scr[n] for n in range(len(items))], chunk)
    for n, (i, ln) in enumerate(items):
        y_ref[i, :, ln] = y[n].astype(y_ref.dtype)
        s_scr[n] = s1[n]

    @pl.when(c == pl.num_programs(1) - 1)
    def _():
        for n in range(len(items)):
            i, g = divmod(n, N_PAIRS)
            sout_ref[i, 2 * g] = s_scr[n, 0:HEAD, 0:HEAD]
            sout_ref[i, 2 * g + 1] = s_scr[n, HEAD:PAIR, HEAD:PAIR]


def _wkv_chunk(kt, bh, kh, rt, v, z, gc, lp, *, batch, seq_len):
    chunk = WKV_CHUNK
    nseq = WKV_SEQS_PER_STEP
    nc = seq_len // chunk
    seq = lambda a: a.reshape(batch, seq_len, D_MODEL)
    tok = pl.BlockSpec((nseq, chunk, D_MODEL), lambda b, c: (b, c, 0))
    par = pl.BlockSpec((1, D_MODEL), lambda b, c: (0, 0))
    st = pl.BlockSpec((nseq, N_HEADS, HEAD, HEAD), lambda b, c: (b, 0, 0, 0))
    y, s_bd = pl.pallas_call(
        functools.partial(_wkv_chunk_kernel, chunk=chunk, nseq=nseq),
        grid=(batch // nseq, nc),
        in_specs=[tok] * 6 + [pl.BlockSpec((nseq, 1, 1, D_MODEL), lambda b, c: (b, c, 0, 0)), par, par, par],
        out_specs=[tok, st],
        out_shape=[jax.ShapeDtypeStruct((batch, seq_len, D_MODEL), BF16),
                   jax.ShapeDtypeStruct((batch, N_HEADS, HEAD, HEAD), F32)],
        scratch_shapes=[pltpu.VMEM((nseq * N_PAIRS, PAIR, PAIR), F32)],
        compiler_params=_cparams(("arbitrary", "arbitrary")),
        name="rwkv_chunk",
    )(seq(kt), seq(bh), seq(kh), seq(rt), seq(v), seq(z), gc.reshape(batch, nc, 1, D_MODEL),
      lp["r_k"].reshape(1, D_MODEL), lp["lnx_w"].reshape(1, D_MODEL), lp["lnx_b"].reshape(1, D_MODEL))
    return y.reshape(batch * seq_len, D_MODEL), s_bd


def _cols_of(tile):
    pad = jnp.zeros((LANES - tile.shape[0], LANES), F32)
    return jnp.concatenate([tile, pad], axis=0).T


def _wkv_step_kernel(*refs, layer, first):
    (kap_ref, b_ref, k_ref, r_ref, v_ref, w_ref, z_ref, rk_ref, lw_ref, lb_ref, s0_ref) = refs[:11]
    y_ref, sout_all, vt_scr, ot_scr = refs[11:] if first else refs[12:]
    sout_ref = sout_all.at[layer] if first else sout_all
    if first:
        for other in range(sout_all.shape[0]):
            if other != layer:
                sout_all[other] = jnp.zeros(sout_all.shape[1:], F32)
    tr = lambda ref: ref[...].T
    kap_t, b_t, k_t, r_t, w_t = tr(kap_ref), tr(b_ref), tr(k_ref), tr(r_ref), tr(w_ref)
    vt_scr[...] = tr(v_ref)

    def body(vi, carry):
        for hh in range(2):
            rows = slice(hh * HEAD, (hh + 1) * HEAD)
            s = s0_ref[hh, vi]
            s_kk = jnp.sum(s * kap_t[rows], axis=0, keepdims=True)
            v_row = vt_scr[pl.ds(hh * HEAD + vi, 1), :]
            s1 = s * w_t[rows] - s_kk * b_t[rows] + v_row * k_t[rows]
            sout_ref[hh, vi] = s1
            ot_scr[pl.ds(hh * HEAD + vi, 1), :] = jnp.sum(s1 * r_t[rows], axis=0, keepdims=True)
        return carry

    lax.fori_loop(0, HEAD, body, 0)
    o = ot_scr[...].T
    dev = o - _head_sum_nat(o) * (1.0 / HEAD)
    var = _head_sum_nat(dev * dev) * (1.0 / HEAD)
    bonus = _head_sum_nat(r_ref[...] * k_ref[...] * rk_ref[...]) * v_ref[...]
    z = z_ref[...]
    y = dev * lax.rsqrt(var + LNX_EPS) * lw_ref[...] + lb_ref[...] + bonus
    y_ref[...] = (y * (z * _sigmoid(z))).astype(y_ref.dtype)


def _wkv_step(kap, b, k2, r, v, w, z, lp, state_all_t, layer, prev_out):
    m = kap.shape[0]
    tok = pl.BlockSpec((m, PAIR), lambda g: (0, g))
    par = pl.BlockSpec((1, PAIR), lambda g: (0, g))
    depth = state_all_t.shape[0]
    first = prev_out is None
    st = pl.BlockSpec((None, 2, HEAD, HEAD, m), lambda g: (layer, g, 0, 0, 0))
    st_all = pl.BlockSpec((depth, 2, HEAD, HEAD, m), lambda g: (0, g, 0, 0, 0))
    ins = [kap, b, k2, r, v, w, z, lp["r_k"].reshape(1, D_MODEL), lp["lnx_w"].reshape(1, D_MODEL),
           lp["lnx_b"].reshape(1, D_MODEL), state_all_t]
    in_specs = [tok] * 7 + [par, par, par, st]
    aliases = {}
    if not first:
        ins.append(prev_out)
        in_specs.append(pl.BlockSpec(memory_space=pl.ANY))
        aliases = {len(ins) - 1: 1}
    return pl.pallas_call(
        functools.partial(_wkv_step_kernel, layer=layer, first=first),
        grid=(N_PAIRS,),
        in_specs=in_specs,
        out_specs=[tok, st_all if first else st],
        out_shape=[jax.ShapeDtypeStruct((m, D_MODEL), F32), jax.ShapeDtypeStruct(state_all_t.shape, F32)],
        scratch_shapes=[pltpu.VMEM((PAIR, m), F32), pltpu.VMEM((PAIR, m), F32)],
        input_output_aliases=aliases,
        compiler_params=_cparams(("arbitrary",)),
        name="rwkv_step",
    )(*ins)


def _gate_norm(y, xs, zm, dsk_row, gn_row):
    y = (y + dsk_row * xs) * (zm * _sigmoid(zm))
    half = D_MODEL // N_GROUPS
    outs = []
    for g in range(N_GROUPS):
        yg = y[:, g * half:(g + 1) * half]
        outs.append(yg * lax.rsqrt(jnp.mean(yg * yg, axis=-1, keepdims=True) + NORM_EPS))
    return jnp.concatenate(outs, axis=1) * gn_row


def _ssd_chunk_kernel(xbc_ref, zm_ref, dt_ref, cw_ref, cb_ref, dtb_ref, alog_ref, dsk_ref, gn_ref, e_ref, et_ref,
                      y_ref, cout_ref, hout_ref, ext, h_scr, *, q):
    c = pl.program_id(1)

    @pl.when(c == 0)
    def _():
        ext[0:8, :] = jnp.zeros((8, CONV_DIM), F32)
        h_scr[...] = jnp.zeros(h_scr.shape, F32)

    ext[8:8 + q, :] = xbc_ref[...]
    u = cb_ref[...] + cw_ref[3:4, :] * ext[8:8 + q, :]
    for i in range(CONV_W - 1):
        u = u + cw_ref[i:i + 1, :] * ext[5 + i:5 + i + q, :]
    tail = ext[q:q + 8, :]
    ext[0:8, :] = tail
    u = u * _sigmoid(u)
    xs = u[:, :D_MODEL]
    bm = [u[:, D_MODEL + g * D_STATE:D_MODEL + (g + 1) * D_STATE].astype(BF16) for g in range(N_GROUPS)]
    cm = [u[:, D_MODEL + (N_GROUPS + g) * D_STATE:D_MODEL + (N_GROUPS + g + 1) * D_STATE].astype(BF16)
          for g in range(N_GROUPS)]

    dt = _softplus(dt_ref[...] + dtb_ref[...])
    d_a = dt * (-jnp.exp(alog_ref[...]))
    row = _iota((q, q), 0)
    col = _iota((q, q), 1)
    causal = col <= row
    tril = causal.astype(BF16)
    cs = sum(_dot(tril, piece) for piece in _pieces(d_a, 3))
    eye = (row == col).astype(BF16)
    cs_t = sum(_dot_tn(piece, eye) for piece in _pieces(cs, 3))
    last = cs[q - 1:q, :]
    expand = lambda t: sum(_dot(piece, e_ref[...]) for piece in _pieces(t, 2))
    dt_x = expand(dt)
    dte_x = expand(jnp.exp(last - cs))
    ecs_x = expand(jnp.exp(cs))
    xdt = xs * dt_x
    cd = jnp.exp(cs_t[:, q - 1:q])
    cd_b = jnp.broadcast_to(cd, (LANES, D_STATE))
    decay_full = sum(_dot(et_ref[...], piece) for piece in _pieces(cd_b, 2))

    pairs = list(range(N_PAIRS))
    grp = lambda pi: pi // (N_PAIRS // N_GROUPS)
    rows = lambda pi: slice(pi * PAIR, (pi + 1) * PAIR)
    cb = [_dot_nt(cm[g], bm[g]) for g in range(N_GROUPS)]

    def decay_weighted(pi):
        gs = []
        for hh in (2 * pi, 2 * pi + 1):
            seg = cs[:, hh:hh + 1] - cs_t[hh:hh + 1, :]
            gs.append(cb[grp(pi)] * jnp.exp(jnp.where(causal, seg, -1e30)))
        return jnp.concatenate(gs, axis=1).astype(BF16)

    g_pair = [decay_weighted(pi) for pi in pairs]
    h_in = [h_scr[rows(pi), :] for pi in pairs]
    y_diag = [_dot(g_pair[pi], _stack(xdt[:, rows(pi)]).astype(BF16)) for pi in pairs]
    y_off = [_dot_nt(cm[grp(pi)], h_in[pi].astype(BF16)) for pi in pairs]
    states = [_dot_tn((xdt[:, rows(pi)] * dte_x[:, rows(pi)]).astype(BF16), bm[grp(pi)]) for pi in pairs]
    for pi in pairs:
        h_scr[rows(pi), :] = decay_full[rows(pi), :] * h_in[pi] + states[pi]
    ys = [y_diag[pi] + y_off[pi] * ecs_x[:, rows(pi)] for pi in pairs]
    y = jnp.concatenate(ys, axis=1)
    y_ref[...] = _gate_norm(y, xs, zm_ref[...], dsk_ref[...], gn_ref[...]).astype(y_ref.dtype)

    @pl.when(c == pl.num_programs(1) - 1)
    def _():
        cout_ref[0] = ext[8 + q - (CONV_W - 1):8 + q, :]
        hout_ref[0] = h_scr[...]


def _ssd_consts(lp):
    heads = jnp.arange(LANES)[:, None]
    cols = jnp.arange(D_MODEL)[None, :] // HEAD
    e_mat = (heads == cols).astype(BF16)
    pad = lambda a: jnp.pad(a.reshape(1, N_HEADS), ((0, 0), (0, DT_PAD - N_HEADS)))
    return dict(cw=lp["conv_w"], cb=lp["conv_b"].reshape(1, CONV_DIM), dtb=pad(lp["dt_bias"]),
                alog=pad(lp["a_log"]), dsk=jnp.repeat(lp["d_skip"], HEAD).reshape(1, D_MODEL),
                gn=lp["gnorm_w"].reshape(1, D_MODEL), e=e_mat, et=e_mat.T)


def _ssd_chunk(xbc, zm, dt, lp, *, batch, seq_len):
    q = SSD_CHUNK
    nq = seq_len // q
    m = batch * seq_len
    k = _ssd_consts(lp)
    full = lambda shape: pl.BlockSpec(shape, lambda b, c: tuple(0 for _ in shape))
    rows = lambda w: pl.BlockSpec((q, w), lambda b, c: (b * nq + c, 0))
    ins = [xbc, zm, dt, k["cw"], k["cb"], k["dtb"], k["alog"], k["dsk"], k["gn"], k["e"], k["et"]]
    in_specs = [rows(CONV_DIM), rows(D_MODEL), rows(DT_PAD), full((CONV_W, CONV_DIM)), full((1, CONV_DIM)),
                full((1, DT_PAD)), full((1, DT_PAD)), full((1, D_MODEL)), full((1, D_MODEL)),
                full((LANES, D_MODEL)), full((D_MODEL, LANES))]
    return pl.pallas_call(
        functools.partial(_ssd_chunk_kernel, q=q),
        grid=(batch, nq),
        in_specs=in_specs,
        out_specs=[rows(D_MODEL), pl.BlockSpec((1, CONV_W - 1, CONV_DIM), lambda b, c: (b, 0, 0)),
                   pl.BlockSpec((1, D_MODEL, D_STATE), lambda b, c: (b, 0, 0))],
        out_shape=[jax.ShapeDtypeStruct((m, D_MODEL), BF16),
                   jax.ShapeDtypeStruct((batch, CONV_W - 1, CONV_DIM), F32),
                   jax.ShapeDtypeStruct((batch, D_MODEL, D_STATE), F32)],
        scratch_shapes=[pltpu.VMEM((q + 8, CONV_DIM), F32), pltpu.VMEM((D_MODEL, D_STATE), F32)],
        compiler_params=_cparams(("arbitrary", "arbitrary")),
        name="ssd_chunk",
    )(*ins)


def _ssd_step_kernel(*refs, nseq, layer, first):
    (xbc_ref, c_ref, zm_ref, dt_ref, cw_ref, cb_ref, dtb_ref, alog_ref, dsk_ref, gn_ref, e_ref,
     h0_ref) = refs[:12]
    y_ref, cout_all, hout_all = refs[12:15] if first else refs[14:17]
    cout_ref = cout_all.at[layer] if first else cout_all
    hout_ref = hout_all.at[layer] if first else hout_all
    if first:
        for other in range(cout_all.shape[0]):
            if other != layer:
                cout_all[other] = jnp.zeros(cout_all.shape[1:], F32)
                hout_all[other] = jnp.zeros(hout_all.shape[1:], F32)
    rowid = _iota((nseq, CONV_DIM), 0)
    u = cb_ref[...] + cw_ref[3:4, :] * xbc_ref[...]
    for i in range(CONV_W - 1):
        prev_rows = jnp.zeros((nseq, CONV_DIM), F32)
        for j in range(nseq):
            prev_rows = jnp.where(rowid == j, c_ref[j, i:i + 1, :], prev_rows)
        u = u + cw_ref[i:i + 1, :] * prev_rows
    for j in range(nseq):
        cout_ref[j, 0:CONV_W - 2, :] = c_ref[j, 1:CONV_W - 1, :]
        cout_ref[j, CONV_W - 2:CONV_W - 1, :] = xbc_ref[j:j + 1, :]
    u = u * _sigmoid(u)
    xs = u[:, :D_MODEL]
    dt = _softplus(dt_ref[...] + dtb_ref[...])
    ed = jnp.exp(dt * (-jnp.exp(alog_ref[...])))
    expand = lambda t: sum(_dot(piece, e_ref[...]) for piece in _pieces(t, 2))
    xdt = xs * expand(dt)
    lane = _iota((PAIR, LANES), 1)
    sel_row = _iota((LANES, D_STATE), 0)
    ys = []
    for p in range(N_PAIRS):
        g = p // (N_PAIRS // N_GROUPS)
        pl_ = slice(p * PAIR, (p + 1) * PAIR)
        b_rows = u[:, D_MODEL + g * D_STATE:D_MODEL + (g + 1) * D_STATE]
        c_rows = u[:, D_MODEL + (N_GROUPS + g) * D_STATE:D_MODEL + (N_GROUPS + g + 1) * D_STATE]
        x_cols_b = _cols_of(xdt[:, pl_]).astype(BF16)
        upper = _iota((PAIR, D_STATE), 0) < HEAD
        decay = lambda j: jnp.where(upper, ed[j:j + 1, 2 * p:2 * p + 1], ed[j:j + 1, 2 * p + 1:2 * p + 2])
        outer = [_dot(x_cols_b, jnp.where(sel_row == j, b_rows[j:j + 1, :], 0.0).astype(BF16))
                 for j in range(nseq)]
        h1s = [decay(j) * h0_ref[j, pl_, :] + outer[j] for j in range(nseq)]
        for j in range(nseq):
            hout_ref[j, pl_, :] = h1s[j]
        y_cols = [jnp.sum(h1s[j] * c_rows[j:j + 1, :], axis=-1, keepdims=True) for j in range(nseq)]
        y_t = jnp.zeros((PAIR, LANES), F32)
        for j in range(nseq):
            y_t = jnp.where(lane == j, y_cols[j], y_t)
        ys.append(y_t.T[0:nseq, :])
    y = jnp.concatenate(ys, axis=1)
    y_ref[...] = _gate_norm(y, xs, zm_ref[...], dsk_ref[...], gn_ref[...]).astype(y_ref.dtype)


def _ssd_step(xbc, zm, dt, lp, conv_all, ssm_all, layer, prev_conv, prev_ssm):
    m = xbc.shape[0]
    nseq = 8
    k = _ssd_consts(lp)
    full = lambda shape: pl.BlockSpec(shape, lambda i: tuple(0 for _ in shape))
    rows = lambda w: pl.BlockSpec((nseq, w), lambda i: (i, 0))
    cst = pl.BlockSpec((None, nseq, CONV_W - 1, CONV_DIM), lambda i: (layer, i, 0, 0))
    hst = pl.BlockSpec((None, nseq, D_MODEL, D_STATE), lambda i: (layer, i, 0, 0))
    depth = conv_all.shape[0]
    cst_all = pl.BlockSpec((depth, nseq, CONV_W - 1, CONV_DIM), lambda i: (0, i, 0, 0))
    hst_all = pl.BlockSpec((depth, nseq, D_MODEL, D_STATE), lambda i: (0, i, 0, 0))
    ins = [xbc, conv_all, zm, dt, k["cw"], k["cb"], k["dtb"], k["alog"], k["dsk"], k["gn"], k["e"], ssm_all]
    in_specs = [rows(CONV_DIM), cst, rows(D_MODEL), rows(DT_PAD), full((CONV_W, CONV_DIM)),
                full((1, CONV_DIM)), full((1, DT_PAD)), full((1, DT_PAD)), full((1, D_MODEL)),
                full((1, D_MODEL)), full((LANES, D_MODEL)), hst]
    aliases = {}
    first = prev_conv is None
    if not first:
        ins += [prev_conv, prev_ssm]
        in_specs += [pl.BlockSpec(memory_space=pl.ANY)] * 2
        aliases = {len(ins) - 2: 1, len(ins) - 1: 2}
    return pl.pallas_call(
        functools.partial(_ssd_step_kernel, nseq=nseq, layer=layer, first=first),
        grid=(m // nseq,),
        in_specs=in_specs,
        out_specs=[rows(D_MODEL), cst_all if first else cst, hst_all if first else hst],
        out_shape=[jax.ShapeDtypeStruct((m, D_MODEL), F32), jax.ShapeDtypeStruct(conv_all.shape, F32),
                   jax.ShapeDtypeStruct(ssm_all.shape, F32)],
        input_output_aliases=aliases,
        compiler_params=_cparams(("arbitrary",)),
        name="ssd_step",
    )(*ins)


def _out_kernel(*refs, final):
    if final:
        x_ref, yr_ref, ym_ref, w_ref, fw_ref, o_ref = refs
    else:
        x_ref, yr_ref, ym_ref, w_ref, o_ref = refs
    out = (x_ref[...] + _dot(yr_ref[...].astype(BF16), w_ref[0:D_MODEL, :])
           + _dot(ym_ref[...].astype(BF16), w_ref[D_MODEL:2 * D_MODEL, :]))
    if final:
        out = out * lax.rsqrt(jnp.mean(out * out, axis=-1, keepdims=True) + NORM_EPS) * fw_ref[...]
    o_ref[...] = out


def _out_proj(x, yr, ym, w_bf, final_w, tm):
    m, d = x.shape
    final = final_w is not None
    tok = pl.BlockSpec((tm, d), lambda i: (i, 0))
    ins = [x, yr, ym, w_bf] + ([final_w.reshape(1, d)] if final else [])
    in_specs = [tok, tok, tok, pl.BlockSpec((2 * d, d), lambda i: (0, 0))]
    if final:
        in_specs.append(pl.BlockSpec((1, d), lambda i: (0, 0)))
    return pl.pallas_call(
        functools.partial(_out_kernel, final=final),
        grid=(m // tm,),
        in_specs=in_specs,
        out_specs=tok,
        out_shape=jax.ShapeDtypeStruct((m, d), F32),
        compiler_params=_cparams(("arbitrary",)),
        name="out_proj",
    )(*ins)


def _layer_prompt(x, lp, final_w, *, batch, seq_len):
    tm = 256
    prev = jnp.zeros((batch, 1, R_SHIFT), F32)
    kt, bh, kh, rt, v, gc, z_r, z_m, xbc, dt = _proj_prep(x, prev, lp, seq_len=seq_len, tm=tm)
    xn_last = _rmsnorm(x.reshape(batch, seq_len, D_MODEL)[:, -1], lp["norm_w"])
    y_r, s_bd = _wkv_chunk(kt, bh, kh, rt, v, z_r, gc, lp, batch=batch, seq_len=seq_len)
    y_m, conv_new, ssm_new = _ssd_chunk(xbc, z_m, dt, lp, batch=batch, seq_len=seq_len)
    x_new = _out_proj(x, y_r, y_m, lp["w_out"], final_w, 512)
    return (x_new, xn_last, s_bd, conv_new,
            ssm_new.reshape(batch, N_HEADS, HEAD, D_STATE))


def _layer_sample(x, states, layer, prev_outs, lp, final_w):
    m = x.shape[0]
    tm = m
    shift_all, wkv_all, conv_all, ssm_all = states
    (prev,) = _proj(shift_all[layer], None, lp["w_in"], (SEG_SH,), tm)
    kap, b, k2, r, v, w, z_r, z_m, xbc, dt, xn = _proj_prep(x, prev, lp, seq_len=1, tm=tm, emit_xn=True)
    y_r, wkv_out = _wkv_step(kap, b, k2, r, v, w, z_r, lp, wkv_all, layer, prev_outs[0])
    y_m, conv_out, ssm_out = _ssd_step(xbc, z_m, dt, lp, conv_all, ssm_all, layer, prev_outs[1], prev_outs[2])
    x_new = _out_proj(x, y_r, y_m, lp["w_out"], final_w, tm)
    return x_new, xn, (wkv_out, conv_out, ssm_out)


def _layer_params(params, l):
    lp = {k: v[l] for k, v in params.items() if k != "w_in"}
    lp["w_in"] = _cast_w_in(jnp.swapaxes(params["w_in"], 1, 2), l)
    lp["w_out"] = lp["w_out"].astype(BF16)
    zeros = jnp.zeros((LORA, D_MODEL), F32)
    lp["w2p"] = jnp.concatenate([lp["w_lora2"], zeros], axis=0).astype(BF16)
    lp["a2p"] = jnp.concatenate([zeros, lp["a_lora2"]], axis=0).astype(BF16)
    return lp


def kernel(x_prompt, x_sample, state_shift, state_wkv, state_conv, state_ssm, norm_w, w_in, mu_shift, w0,
           w_lora2, a0, a_lora2, k_k, k_a, r_k, lnx_w, lnx_b, conv_w, conv_b, dt_bias, a_log, d_skip,
           gnorm_w, w_out, final_norm_w):
    params = dict(norm_w=norm_w, w_in=w_in, mu_shift=mu_shift, w0=w0, w_lora2=w_lora2, a0=a0,
                  a_lora2=a_lora2, k_k=k_k, k_a=k_a, r_k=r_k, lnx_w=lnx_w, lnx_b=lnx_b, conv_w=conv_w,
                  conv_b=conv_b, dt_bias=dt_bias, a_log=a_log, d_skip=d_skip, gnorm_w=gnorm_w, w_out=w_out)
    depth = norm_w.shape[0]
    bp, lseq, d = x_prompt.shape
    bs = x_sample.shape[0]
    xp = x_prompt.reshape(bp * lseq, d)
    xs = x_sample.reshape(bs, d)
    sample_states = (state_shift, jnp.transpose(state_wkv, (0, 2, 3, 4, 1)), state_conv,
                     state_ssm.reshape(depth, bs, N_HEADS * HEAD, D_STATE))
    p_states, s_shift = [], []
    s_outs = (None, None, None)
    for l in range(depth):
        lp = _layer_params(params, l)
        fw = final_norm_w if l == depth - 1 else None
        xp, *st = _layer_prompt(xp, lp, fw, batch=bp, seq_len=lseq)
        p_states.append(st)
        xs, xn, s_outs = _layer_sample(xs, sample_states, l, s_outs, lp, fw)
        s_shift.append(xn)
    stack = lambda states, i: jnp.stack([s[i] for s in states])
    return (xp.reshape(bp, lseq, d), xs.reshape(bs, 1, d),
            stack(p_states, 0), stack(p_states, 1), stack(p_states, 2), stack(p_states, 3),
            jnp.stack(s_shift), jnp.transpose(s_outs[0], (0, 4, 1, 2, 3)), s_outs[1],
            s_outs[2].reshape(state_ssm.shape))
```
